```python
import math
import jax, jax.numpy as jnp
from jax import lax
import numpy as np

D_MODEL = 1024
BATCH = 16
SEQ = 2048
DEPTH = 2
DEC_BATCH = 8
DEC_SEQ = 32
PAST_LEN = 2048

CHUNK = 64
EPS = 1e-6
N_AB = (DEPTH + 1) // 2
N_C = DEPTH // 2
H_A = D_MODEL // 256
DK_A = 128
DV_A = 128
CONV_W = 4
A_QK = H_A * DK_A
A_V = H_A * DV_A
CONV_DIM = 2 * A_QK + A_V
H_B = D_MODEL // 256
HD_B = 128
B_W = H_B * HD_B
FOX_QBLK = 128
HQ_C = D_MODEL // 64
HKV_C = HQ_C // 4
G_C = HQ_C // HKV_C
HD_C = 64
WINDOW = 128
NB_C = WINDOW // CHUNK
H_X = 4
HD_X = 128
MEM_LEN = 256
D_FF = 4 * D_MODEL
AB_IN = CONV_DIM + A_V + 2 * H_A + 3 * B_W + H_B
AB_MIX = A_V + B_W
C_IN = (HQ_C + 2 * HKV_C) * HD_C
C_MIX = HQ_C * HD_C

kernel_name = 'hybrid_stream_gdn_fox_swa_step'


def rmsnorm(x, g):
    xf = x.astype(jnp.float32)
    y = xf * lax.rsqrt(jnp.mean(xf * xf, axis=-1, keepdims=True) + EPS)
    return (y * g.astype(jnp.float32)).astype(x.dtype)


def l2norm(x):
    return x * lax.rsqrt(jnp.sum(x * x, axis=-1, keepdims=True) + EPS)


def causal_conv_silu(u, prev, w):
    L = u.shape[1]
    full = jnp.concatenate([prev.astype(u.dtype), u], axis=1)
    out = full[:, 0:L] * w[0]
    for i in range(1, CONV_W):
        out = out + full[:, i:i + L] * w[i]
    return jax.nn.silu(out), full[:, L:]


def gated_delta_rule(q, k, v, g, beta, s0):
    B, L, H, DK = q.shape
    DV = v.shape[-1]
    C = min(CHUNK, L)
    N = L // C

    def chunks(a):
        return a.reshape(B, N, C, H, a.shape[-1]).transpose(1, 0, 3, 2, 4)

    q = chunks(q * DK ** -0.5)
    k = chunks(k)
    v = chunks(v)
    g = g.reshape(B, N, C, H).transpose(1, 0, 3, 2)
    beta = beta.reshape(B, N, C, H).transpose(1, 0, 3, 2)
    gc = jnp.cumsum(g, axis=-1)
    incl = jnp.tril(jnp.ones((C, C), bool))
    strict = jnp.tril(jnp.ones((C, C), bool), -1)
    decay = jnp.exp(jnp.where(incl, gc[..., :, None] - gc[..., None, :], -jnp.inf))
    kb = k * beta[..., None]
    lmat = jnp.where(strict, jnp.einsum('nbhik,nbhjk->nbhij', kb, k) * decay, 0.0)
    amat = lmat + jnp.eye(C, dtype=jnp.float32)
    rhs = jnp.concatenate([v * beta[..., None], kb * jnp.exp(gc)[..., None]], axis=-1)
    sol = lax.linalg.triangular_solve(amat, rhs, left_side=True, lower=True, unit_diagonal=True)
    u, w = sol[..., :DV], sol[..., DV:]
    attn = jnp.einsum('nbhik,nbhjk->nbhij', q, k) * decay

    def step(S, xs):
        qc, kc, uc, wc, gcc, ac = xs
        v_new = uc - jnp.einsum('bhck,bhkv->bhcv', wc, S)
        o = (jnp.einsum('bhck,bhkv->bhcv', qc * jnp.exp(gcc)[..., None], S)
             + jnp.einsum('bhij,bhjv->bhiv', ac, v_new))
        g_last = gcc[..., -1]
        S = (S * jnp.exp(g_last)[..., None, None]
             + jnp.einsum('bhck,bhcv->bhkv', kc * jnp.exp(g_last[..., None] - gcc)[..., None], v_new))
        return S, o

    S, o = lax.scan(step, s0, (q, k, u, w, gc, attn))
    o = o.transpose(1, 0, 3, 2, 4).reshape(B, L, H, DV)
    return o, S


def fox_attention(q, k, v, logf):
    B, Lq, H, D = q.shape
    Lk = k.shape[1]
    off = Lk - Lq
    F = jnp.cumsum(logf.astype(jnp.float32), axis=1).transpose(0, 2, 1)
    QB = min(FOX_QBLK, Lq)
    nb = Lq // QB
    kpos = jnp.arange(Lk)

    def block(i):
        start = i * QB
        qi = lax.dynamic_slice_in_dim(q, start, QB, axis=1)
        Fq = lax.dynamic_slice_in_dim(F, off + start, QB, axis=2)
        qpos = off + start + jnp.arange(QB)
        s = (jnp.einsum('bqhd,bkhd->bhqk', qi, k).astype(jnp.float32) * D ** -0.5
             + Fq[..., :, None] - F[..., None, :])
        s = jnp.where(kpos[None, :] <= qpos[:, None], s, -jnp.inf)
        p = jax.nn.softmax(s, axis=-1)
        return jnp.einsum('bhqk,bkhd->bqhd', p.astype(v.dtype), v)

    o = lax.map(block, jnp.arange(nb))
    return o.transpose(1, 0, 2, 3, 4).reshape(B, Lq, H, D)


def swa_attend(q, k, v, valid, sinks):
    Cq, KB = q.shape[2], k.shape[2]
    slopes = jnp.exp2(-8.0 * jnp.arange(1, HQ_C + 1, dtype=jnp.float32) / HQ_C).reshape(HKV_C, G_C)
    dist = jnp.abs(jnp.arange(Cq)[:, None] - jnp.arange(KB)[None, :] + WINDOW).astype(jnp.float32)
    s = jnp.einsum('bnqhgd,bnshd->bnhgqs', q, k).astype(jnp.float32) * HD_C ** -0.5
    s = s - slopes[:, :, None, None] * dist
    s = jnp.where(valid[:, None, None, None, :], s, -jnp.inf)
    sink = jnp.broadcast_to(sinks.astype(jnp.float32).reshape(HKV_C, G_C)[:, :, None, None], s.shape[:-1] + (1,))
    p = jax.nn.softmax(jnp.concatenate([s, sink], axis=-1), axis=-1)[..., :-1]
    return jnp.einsum('bnhgqs,bnshd->bnqhgd', p.astype(v.dtype), v)


def ab_mixer(h, conv_prev, s0, past_k, past_v, past_logf,
             w_in, conv_w, a_log, dt_bias, onorm_g, f_bias, w_out):
    B, L, _ = h.shape
    f32 = jnp.float32
    sizes = (CONV_DIM, A_V, H_A, H_A, B_W, B_W, B_W, H_B)
    idx = [sum(sizes[:j]) for j in range(1, len(sizes))]
    qkv_a, z_a, b_a, a_a, q_b, k_b, v_b, f_b = jnp.split(h @ w_in, idx, axis=-1)
    qkv_a, conv_new = causal_conv_silu(qkv_a, conv_prev, conv_w)
    qa, ka, va = jnp.split(qkv_a.astype(f32), [A_QK, 2 * A_QK], axis=-1)
    qa = l2norm(qa.reshape(B, L, H_A, DK_A))
    ka = l2norm(ka.reshape(B, L, H_A, DK_A))
    va = va.reshape(B, L, H_A, DV_A)
    beta = jax.nn.sigmoid(b_a.astype(f32))
    g = -jnp.exp(a_log.astype(f32)) * jax.nn.softplus(a_a.astype(f32) + dt_bias.astype(f32))
    oa, s_new = gated_delta_rule(qa, ka, va, g, beta, s0.astype(f32))
    oa = rmsnorm(oa, onorm_g) * jax.nn.silu(z_a.astype(f32).reshape(B, L, H_A, DV_A))
    oa = oa.reshape(B, L, A_V).astype(h.dtype)
    qb = q_b.reshape(B, L, H_B, HD_B)
    kb = k_b.reshape(B, L, H_B, HD_B)
    vb = v_b.reshape(B, L, H_B, HD_B)
    logf = jax.nn.log_sigmoid(f_b.astype(f32) + f_bias.astype(f32))
    if past_k is None:
        k_all, v_all, lf_all = kb, vb, logf
    else:
        k_all = jnp.concatenate([past_k.astype(kb.dtype), kb], axis=1)
        v_all = jnp.concatenate([past_v.astype(vb.dtype), vb], axis=1)
        lf_all = jnp.concatenate([past_logf.astype(f32), logf], axis=1)
    ob = fox_attention(qb, k_all, v_all, lf_all).reshape(B, L, B_W).astype(h.dtype)
    out = jnp.concatenate([oa, ob], axis=-1) @ w_out
    return out, s_new, conv_new, kb, vb, logf


def c_mixer(h, past_k, past_v, w_in, sinks, w_out):
    B, L, _ = h.shape
    q, k, v = jnp.split(h @ w_in, [HQ_C * HD_C, (HQ_C + HKV_C) * HD_C], axis=-1)
    q = q.reshape(B, L, HKV_C, G_C, HD_C)
    k = k.reshape(B, L, HKV_C, HD_C)
    v = v.reshape(B, L, HKV_C, HD_C)
    if past_k is None:
        N = L // CHUNK
        q_blk = q.reshape(B, N, CHUNK, HKV_C, G_C, HD_C)
        pad = ((0, 0), (WINDOW, 0), (0, 0), (0, 0))
        kc = jnp.pad(k, pad).reshape(B, N + NB_C, CHUNK, HKV_C, HD_C)
        vc = jnp.pad(v, pad).reshape(B, N + NB_C, CHUNK, HKV_C, HD_C)
        k_blk = jnp.concatenate([kc[:, j:j + N] for j in range(NB_C + 1)], axis=2)
        v_blk = jnp.concatenate([vc[:, j:j + N] for j in range(NB_C + 1)], axis=2)
        key_abs = jnp.arange(N)[:, None] * CHUNK - WINDOW + jnp.arange(WINDOW + CHUNK)[None, :]
        valid = key_abs >= 0
        new_k, new_v = k[:, -WINDOW:], v[:, -WINDOW:]
    else:
        q_blk = q[:, None]
        k_blk = jnp.concatenate([past_k.astype(k.dtype), k], axis=1)[:, None]
        v_blk = jnp.concatenate([past_v.astype(v.dtype), v], axis=1)[:, None]
        valid = jnp.ones((1, WINDOW + L), bool)
        new_k, new_v = k, v
    o = swa_attend(q_blk, k_blk, v_blk, valid, sinks).reshape(B, L, C_MIX)
    return o @ w_out, new_k, new_v


def mem_kv(mem, g, w_kv):
    B, M, _ = mem.shape
    mk, mv = jnp.split(rmsnorm(mem, g) @ w_kv, 2, axis=-1)
    return mk.reshape(B, M, H_X, HD_X), mv.reshape(B, M, H_X, HD_X)


def cross_attend(h, mk, mv, wq, wo):
    B, L, _ = h.shape
    q = (h @ wq).reshape(B, L, H_X, HD_X)
    s = jnp.einsum('blhd,bmhd->bhlm', q, mk.astype(q.dtype)).astype(jnp.float32) * HD_X ** -0.5
    p = jax.nn.softmax(s, axis=-1)
    o = jnp.einsum('bhlm,bmhd->blhd', p.astype(h.dtype), mv.astype(h.dtype))
    return o.reshape(B, L, H_X * HD_X) @ wo


def sq_relu_mlp(h, w_up, w_down):
    return jnp.square(jax.nn.relu(h @ w_up)) @ w_down


def setup_inputs(seed: int = 0) -> dict:
    key = jax.random.key(seed)
    ks = iter(jax.random.split(key, 48))
    f32 = jnp.float32

    def nrm(shape, scale=1.0):
        return scale * jax.random.normal(next(ks), shape, f32)

    def gain(shape):
        return 1.0 + 0.02 * jax.random.normal(next(ks), shape, f32)

    dt = jnp.exp(jax.random.uniform(next(ks), (N_AB, H_A), f32, math.log(1e-3), math.log(1e-1)))
    return {
        'x_prompt': nrm((BATCH, SEQ, D_MODEL)),
        'mem_prompt': nrm((BATCH, MEM_LEN, D_MODEL)),
        'x_sample': nrm((DEC_BATCH, DEC_SEQ, D_MODEL)),
        'state_gdn': nrm((N_AB, DEC_BATCH, H_A, DK_A, DV_A), 0.05),
        'state_gdn_conv': nrm((N_AB, DEC_BATCH, CONV_W - 1, CONV_DIM)),
        'cache_fox_k': nrm((N_AB, DEC_BATCH, PAST_LEN, H_B, HD_B)),
        'cache_fox_v': nrm((N_AB, DEC_BATCH, PAST_LEN, H_B, HD_B)),
        'cache_fox_logf': jax.nn.log_sigmoid(3.0 + nrm((N_AB, DEC_BATCH, PAST_LEN, H_B))),
        'cache_swa_k': nrm((N_C, DEC_BATCH, WINDOW, HKV_C, HD_C)),
        'cache_swa_v': nrm((N_C, DEC_BATCH, WINDOW, HKV_C, HD_C)),
        'cache_mem_k': nrm((DEPTH, DEC_BATCH, MEM_LEN, H_X, HD_X)),
        'cache_mem_v': nrm((DEPTH, DEC_BATCH, MEM_LEN, H_X, HD_X)),
        'g_mix': gain((DEPTH, D_MODEL)),
        'w_in_ab': nrm((N_AB, D_MODEL, AB_IN), D_MODEL ** -0.5),
        'conv_w_a': nrm((N_AB, CONV_W, CONV_DIM), CONV_W ** -0.5),
        'a_log_a': jnp.log(jax.random.uniform(next(ks), (N_AB, H_A), f32, 1.0, 16.0)),
        'dt_bias_a': dt + jnp.log(-jnp.expm1(-dt)),
        'onorm_g_a': gain((N_AB, DV_A)),
        'f_bias_b': 3.0 + nrm((N_AB, H_B), 0.5),
        'w_out_ab': nrm((N_AB, AB_MIX, D_MODEL), AB_MIX ** -0.5),
        'w_in_c': nrm((N_C, D_MODEL, C_IN), D_MODEL ** -0.5),
        'sinks_c': nrm((N_C, HQ_C), 0.5),
        'w_out_c': nrm((N_C, C_MIX, D_MODEL), C_MIX ** -0.5),
        'g_mem': gain((DEPTH, D_MODEL)),
        'w_xkv': nrm((DEPTH, D_MODEL, 2 * H_X * HD_X), D_MODEL ** -0.5),
        'g_cross': gain((DEPTH, D_MODEL)),
        'w_xq': nrm((DEPTH, D_MODEL, H_X * HD_X), D_MODEL ** -0.5),
        'w_xo': nrm((DEPTH, H_X * HD_X, D_MODEL), (H_X * HD_X) ** -0.5),
        'g_mlp': gain((DEPTH, D_MODEL)),
        'w_up': nrm((DEPTH, D_MODEL, D_FF), D_MODEL ** -0.5),
        'w_down': nrm((DEPTH, D_FF, D_MODEL), D_FF ** -0.5),
        'g_final': gain((D_MODEL,)),
    }


def reference(x_prompt, mem_prompt, x_sample, state_gdn, state_gdn_conv, cache_fox_k, cache_fox_v,
              cache_fox_logf, cache_swa_k, cache_swa_v, cache_mem_k, cache_mem_v,
              g_mix, w_in_ab, conv_w_a, a_log_a, dt_bias_a, onorm_g_a, f_bias_b, w_out_ab,
              w_in_c, sinks_c, w_out_c, g_mem, w_xkv, g_cross, w_xq, w_xo, g_mlp, w_up, w_down, g_final):
    xp, xs = x_prompt, x_sample
    Bp = xp.shape[0]
    p_s, p_c, p_fk, p_fv, p_fl, p_wk, p_wv, p_mk, p_mv = [], [], [], [], [], [], [], [], []
    s_s, s_c, s_fk, s_fv, s_fl, s_wk, s_wv = [], [], [], [], [], [], []
    for layer in range(DEPTH):
        i = layer // 2
        if layer % 2 == 0:
            wts = (w_in_ab[i], conv_w_a[i], a_log_a[i], dt_bias_a[i], onorm_g_a[i], f_bias_b[i], w_out_ab[i])
            conv0 = jnp.zeros((Bp, CONV_W - 1, CONV_DIM), xp.dtype)
            s0 = jnp.zeros((Bp, H_A, DK_A, DV_A), jnp.float32)
            o, st, cv, fk, fv, fl = ab_mixer(rmsnorm(xp, g_mix[layer]), conv0, s0, None, None, None, *wts)
            xp = xp + o
            p_s.append(st); p_c.append(cv); p_fk.append(fk); p_fv.append(fv); p_fl.append(fl)
            o, st, cv, fk, fv, fl = ab_mixer(rmsnorm(xs, g_mix[layer]), state_gdn_conv[i], state_gdn[i],
                                             cache_fox_k[i], cache_fox_v[i], cache_fox_logf[i], *wts)
            xs = xs + o
            s_s.append(st); s_c.append(cv); s_fk.append(fk); s_fv.append(fv); s_fl.append(fl)
        else:
            o, wk, wv = c_mixer(rmsnorm(xp, g_mix[layer]), None, None, w_in_c[i], sinks_c[i], w_out_c[i])
            xp = xp + o
            p_wk.append(wk); p_wv.append(wv)
            o, wk, wv = c_mixer(rmsnorm(xs, g_mix[layer]), cache_swa_k[i], cache_swa_v[i],
                                w_in_c[i], sinks_c[i], w_out_c[i])
            xs = xs + o
            s_wk.append(wk); s_wv.append(wv)
        mk, mv = mem_kv(mem_prompt, g_mem[layer], w_xkv[layer])
        p_mk.append(mk); p_mv.append(mv)
        xp = xp + cross_attend(rmsnorm(xp, g_cross[layer]), mk, mv, w_xq[layer], w_xo[layer])
        xs = xs + cross_attend(rmsnorm(xs, g_cross[layer]), cache_mem_k[layer], cache_mem_v[layer],
                               w_xq[layer], w_xo[layer])
        xp = xp + sq_relu_mlp(rmsnorm(xp, g_mlp[layer]), w_up[layer], w_down[layer])
        xs = xs + sq_relu_mlp(rmsnorm(xs, g_mlp[layer]), w_up[layer], w_down[layer])
    y_prompt = rmsnorm(xp, g_final)
    y_sample = rmsnorm(xs, g_final)
    return (y_prompt, y_sample,
            jnp.stack(p_s), jnp.stack(p_c), jnp.stack(p_fk), jnp.stack(p_fv), jnp.stack(p_fl),
            jnp.stack(p_wk), jnp.stack(p_wv), jnp.stack(p_mk), jnp.stack(p_mv),
            jnp.stack(s_s), jnp.stack(s_c), jnp.stack(s_fk), jnp.stack(s_fv), jnp.stack(s_fl),
            jnp.stack(s_wk), jnp.stack(s_wv))
```

```python
import functools
import math

import jax
import jax.numpy as jnp
from jax import lax
from jax.experimental import pallas as pl
from jax.experimental.pallas import tpu as pltpu

F32 = jnp.float32
BF16 = jnp.bfloat16
HI = lax.Precision.HIGHEST
EPS = 1e-6
NEG = -1e30

CHUNK = 64
CONV_W = 4
WINDOW = 128
LANE = 128
SUBLANE = 8
VMEM_BYTES = 64 * 1024 * 1024


def _params(semantics, vmem_bytes):
    limit = int(min(max(vmem_bytes * 1.25, 16 * 1024 * 1024), VMEM_BYTES - 8 * 1024 * 1024))
    return pltpu.CompilerParams(dimension_semantics=semantics, vmem_limit_bytes=limit)


def _dot(a, b):
    return jnp.dot(a, b, preferred_element_type=F32)


def _dot_nt(a, b, precision=None):
    return lax.dot_general(a, b, (((1,), (1,)), ((), ())), preferred_element_type=F32, precision=precision)


def _dot_tn(a, b):
    return lax.dot_general(a, b, (((0,), (0,)), ((), ())), preferred_element_type=F32)


def _rms(x, g):
    return x * lax.rsqrt(jnp.mean(x * x, axis=-1, keepdims=True) + EPS) * g


def _sigmoid(x):
    return 1.0 / (1.0 + jnp.exp(-x))


def _softplus(x):
    return jnp.maximum(x, 0.0) + jnp.log(1.0 + jnp.exp(-jnp.abs(x)))


def _token_tile(t, cap):
    tm = min(t, cap)
    assert t % tm == 0
    return tm


def _norm_matmul_body(*refs, groups, gated, col_chunk):
    x_ref, g_ref, w_ref = refs[:3]
    refs = refs[3:]
    if gated:
        gp_ref, refs = refs[0], refs[1:]
    x = x_ref[...]
    yb = _rms(x, g_ref[...]).astype(BF16)
    off, k = 0, 0
    for gi, (n, outs) in enumerate(groups):
        for c in range(0, n, col_chunk):
            cw = min(col_chunk, n - c)
            r = _dot(yb, w_ref[:, off + c:off + c + cw])
            if gated and gi == len(groups) - 1:
                rb = r + gp_ref[0:1, :]
                lane = lax.broadcasted_iota(jnp.int32, rb.shape, 1)
                e = jnp.exp(-jnp.abs(rb))
                l1p = jnp.log(1.0 + e)
                sig = _sigmoid(rb)
                gdec = -jnp.exp(gp_ref[1:2, :]) * (jnp.maximum(rb, 0.0) + l1p)
                logsig = jnp.minimum(rb, 0.0) - l1p
                r = jnp.where(lane < 4, sig, jnp.where(lane < 8, gdec, jnp.where(lane < 12, logsig, 0.0)))
            for j, (dtype, scale) in enumerate(outs):
                o_ref = refs[k + j]
                rr = r if scale == 1.0 else r * scale
                o_ref[:, c:c + cw] = rr.astype(dtype)
        off += n
        k += len(outs)


def norm_matmul(x, g, w, groups, gate_params=None, tile_cap=512):
    t, d = x.shape
    tm = _token_tile(t, tile_cap)
    gated = gate_params is not None
    in_specs = [pl.BlockSpec((tm, d), lambda i: (i, 0)),
                pl.BlockSpec((1, d), lambda i: (0, 0)),
                pl.BlockSpec(w.shape, lambda i: (0, 0))]
    args = [x, g.reshape(1, d), w]
    if gated:
        in_specs.append(pl.BlockSpec(gate_params.shape, lambda i: (0, 0)))
        args.append(gate_params)
    out_shape, out_specs, out_bytes = [], [], 0
    for n, outs in groups:
        for dtype, _ in outs:
            out_shape.append(jax.ShapeDtypeStruct((t, n), dtype))
            out_specs.append(pl.BlockSpec((tm, n), lambda i: (i, 0)))
            out_bytes += tm * n * jnp.dtype(dtype).itemsize
    vmem = 2 * (tm * d * 4 + w.size * 2 + out_bytes) + tm * d * 8
    return pl.pallas_call(
        functools.partial(_norm_matmul_body, groups=groups, gated=gated, col_chunk=512),
        grid=(t // tm,), in_specs=in_specs, out_specs=out_specs, out_shape=out_shape,
        compiler_params=_params(("parallel",), vmem), name="norm_matmul")(*args)


def _matmul_residual_body(*refs, n_in, col_chunk):
    x_ref = refs[0]
    a_refs, w_refs, o_ref = refs[1:1 + n_in], refs[1 + n_in:1 + 2 * n_in], refs[1 + 2 * n_in]
    a_vals = [a[...].astype(BF16) for a in a_refs]
    d = x_ref.shape[-1]
    for c in range(0, d, col_chunk):
        acc = x_ref[:, c:c + col_chunk]
        for a, w in zip(a_vals, w_refs):
            acc = acc + _dot(a, w[:, c:c + col_chunk])
        o_ref[:, c:c + col_chunk] = acc


def matmul_residual(x, a_list, w_list, tile_cap=512):
    t, d = x.shape
    tm = _token_tile(t, tile_cap)
    n_in = len(a_list)
    in_specs = [pl.BlockSpec((tm, d), lambda i: (i, 0))]
    in_specs += [pl.BlockSpec((tm, a.shape[1]), lambda i: (i, 0)) for a in a_list]
    in_specs += [pl.BlockSpec(w.shape, lambda i: (0, 0)) for w in w_list]
    vmem = 2 * (2 * tm * d * 4 + sum(tm * a.shape[1] * a.dtype.itemsize for a in a_list)
                + sum(w.size * 2 for w in w_list))
    return pl.pallas_call(
        functools.partial(_matmul_residual_body, n_in=n_in, col_chunk=512),
        grid=(t // tm,), in_specs=in_specs, out_specs=pl.BlockSpec((tm, d), lambda i: (i, 0)),
        out_shape=jax.ShapeDtypeStruct((t, d), F32),
        compiler_params=_params(("parallel",), vmem), name="matmul_residual")(x, *a_list, *w_list)


def _cross_body(x_ref, g_ref, wq_ref, mk_ref, mv_ref, wo_ref, o_ref, *, heads, hd):
    x = x_ref[...]
    hb = _rms(x, g_ref[...]).astype(BF16)
    q = _dot(hb, wq_ref[...]) * (hd ** -0.5)
    outs = []
    for h in range(heads):
        sl = slice(h * hd, (h + 1) * hd)
        qh = q[:, sl].astype(BF16)
        kh = mk_ref[:, sl].astype(BF16)
        vh = mv_ref[:, sl].astype(BF16)
        s = _dot_nt(qh, kh)
        m = jnp.max(s, axis=-1, keepdims=True)
        p = jnp.exp(s - m)
        l = jnp.sum(p, axis=-1, keepdims=True)
        outs.append((_dot(p.astype(BF16), vh) / l).astype(BF16))
    oc = jnp.concatenate(outs, axis=-1)
    o_ref[...] = x + _dot(oc, wo_ref[...])


def cross_attend(x, g, wq, mk, mv, wo, heads, tile_cap=512):
    b, l, d = x.shape
    m, hw = mk.shape[1], mk.shape[2]
    tm = _token_tile(l, tile_cap)
    vmem = 2 * (2 * tm * d * 4 + 2 * m * hw * 4 + wq.size * 2 + wo.size * 2) + tm * (d * 6 + hw * 8 + m * 8)
    return pl.pallas_call(
        functools.partial(_cross_body, heads=heads, hd=hw // heads),
        grid=(b, l // tm),
        in_specs=[pl.BlockSpec((None, tm, d), lambda i, j: (i, j, 0)),
                  pl.BlockSpec((1, d), lambda i, j: (0, 0)),
                  pl.BlockSpec(wq.shape, lambda i, j: (0, 0)),
                  pl.BlockSpec((None, m, hw), lambda i, j: (i, 0, 0)),
                  pl.BlockSpec((None, m, hw), lambda i, j: (i, 0, 0)),
                  pl.BlockSpec(wo.shape, lambda i, j: (0, 0))],
        out_specs=pl.BlockSpec((None, tm, d), lambda i, j: (i, j, 0)),
        out_shape=jax.ShapeDtypeStruct((b, l, d), F32),
        compiler_params=_params(("parallel", "parallel"), vmem), name="cross_attend")(
            x, g.reshape(1, d), wq, mk, mv, wo)


def _mlp_body(*refs, ff_chunk, final_norm):
    if final_norm:
        x_ref, g_ref, wu_ref, wd_ref, gf_ref, o_ref = refs
    else:
        x_ref, g_ref, wu_ref, wd_ref, o_ref = refs
    x = x_ref[...]
    hb = _rms(x, g_ref[...]).astype(BF16)
    acc = x
    for c in range(0, wu_ref.shape[1], ff_chunk):
        u = jnp.maximum(_dot(hb, wu_ref[:, c:c + ff_chunk]), 0.0)
        acc = acc + _dot((u * u).astype(BF16), wd_ref[c:c + ff_chunk, :])
    if final_norm:
        acc = _rms(acc, gf_ref[...])
    o_ref[...] = acc


def mlp(x, g, w_up, w_down, g_final=None, tile_cap=512):
    t, d = x.shape
    tm = _token_tile(t, tile_cap)
    final_norm = g_final is not None
    in_specs = [pl.BlockSpec((tm, d), lambda i: (i, 0)),
                pl.BlockSpec((1, d), lambda i: (0, 0)),
                pl.BlockSpec(w_up.shape, lambda i: (0, 0)),
                pl.BlockSpec(w_down.shape, lambda i: (0, 0))]
    args = [x, g.reshape(1, d), w_up, w_down]
    if final_norm:
        in_specs.append(pl.BlockSpec((1, d), lambda i: (0, 0)))
        args.append(g_final.reshape(1, d))
    vmem = 2 * (2 * tm * d * 4 + w_up.size * 2 + w_down.size * 2) + tm * (d * 10 + 512 * 8)
    return pl.pallas_call(
        functools.partial(_mlp_body, ff_chunk=512, final_norm=final_norm),
        grid=(t // tm,), in_specs=in_specs, out_specs=pl.BlockSpec((tm, d), lambda i: (i, 0)),
        out_shape=jax.ShapeDtypeStruct((t, d), F32),
        compiler_params=_params(("parallel",), vmem), name="mlp")(*args)


def _cumsum_body(x_ref, o_ref, *, minus_total):
    r, l = x_ref.shape
    row = lax.broadcasted_iota(jnp.int32, (LANE, LANE), 0)
    col = lax.broadcasted_iota(jnp.int32, (LANE, LANE), 1)
    upper = (row <= col).astype(F32)
    carry = jnp.zeros((r, 1), F32)
    for c in range(0, l, LANE):
        y = jnp.dot(x_ref[:, c:c + LANE], upper, preferred_element_type=F32, precision=HI) + carry
        o_ref[:, c:c + LANE] = y
        carry = y[:, LANE - 1:LANE]
    if minus_total:
        o_ref[...] = o_ref[...] - carry


def cumsum_rows(x, minus_total=False):
    r, l = x.shape
    assert l % LANE == 0 and r % SUBLANE == 0
    return pl.pallas_call(
        functools.partial(_cumsum_body, minus_total=minus_total),
        grid=(1,), in_specs=[pl.BlockSpec((r, l), lambda i: (0, 0))],
        out_specs=pl.BlockSpec((r, l), lambda i: (0, 0)),
        out_shape=jax.ShapeDtypeStruct((r, l), F32),
        compiler_params=_params(("arbitrary",), 4 * r * l * 4), name="cumsum_rows")(x)


def _fox_body(*refs, tq, tkp, n_past):
    if n_past:
        q_ref, k_ref, v_ref, fq_ref, fk_ref, kp_ref, vp_ref, fp_ref, o_ref = refs
    else:
        q_ref, k_ref, v_ref, fq_ref, fk_ref, o_ref = refs
    i = pl.program_id(2)
    q = q_ref[...]
    hd = q.shape[-1]
    wq = fq_ref.shape[-1]
    fq_col = jnp.broadcast_to(fq_ref[...], (LANE, wq)).T[:tq, :]

    def bias(s, fk_row):
        w = s.shape[-1]
        fqb = fq_col if w <= LANE else jnp.concatenate([fq_col] * (w // LANE), axis=-1)
        return s + fqb[:, :w] - fk_row

    def update(carry, s, vb):
        m, l, acc = carry
        m_new = jnp.maximum(m, jnp.max(s, axis=-1, keepdims=True))
        alpha = jnp.exp(m - m_new)
        p = jnp.exp(s - m_new)
        l = alpha * l + jnp.sum(p, axis=-1, keepdims=True)
        acc = alpha * acc + _dot(p.astype(BF16), vb)
        return m_new, l, acc

    carry = (jnp.full((tq, 1), NEG, F32), jnp.zeros((tq, 1), F32), jnp.zeros((tq, hd), F32))

    if n_past:
        def past_step(j, carry):
            start = pl.multiple_of(j * tkp, tkp)
            kb = kp_ref[pl.ds(start, tkp), :].astype(BF16)
            vb = vp_ref[pl.ds(start, tkp), :].astype(BF16)
            s = bias(_dot_nt(q, kb), fp_ref[pl.ds(j, 1), :])
            return update(carry, s, vb)
        carry = lax.fori_loop(0, n_past, past_step, carry)

    def self_step(j, carry):
        start = pl.multiple_of(j * tq, tq)
        kb = k_ref[pl.ds(start, tq), :]
        vb = v_ref[pl.ds(start, tq), :]
        s = bias(_dot_nt(q, kb), fk_ref[pl.ds(j, 1), :][:, :tq])
        return update(carry, s, vb)
    carry = lax.fori_loop(0, i, self_step, carry)

    start = pl.multiple_of(i * tq, tq)
    kb = k_ref[pl.ds(start, tq), :]
    vb = v_ref[pl.ds(start, tq), :]
    s = bias(_dot_nt(q, kb), fk_ref[pl.ds(i, 1), :][:, :tq])
    row = lax.broadcasted_iota(jnp.int32, (tq, tq), 0)
    col = lax.broadcasted_iota(jnp.int32, (tq, tq), 1)
    s = jnp.where(col <= row, s, NEG)
    _, l, acc = update(carry, s, vb)
    o_ref[...] = (acc / l).astype(o_ref.dtype)


def fox_attention(q, k, v, f_new, heads, past=None, tq_cap=256, tkp=256):
    b, l, hw = q.shape
    hd = hw // heads
    tq = min(l, tq_cap)
    nq = l // tq
    wq = -(-tq // LANE) * LANE
    if wq != tq:
        assert nq == 1
        f_new = jnp.pad(f_new, ((0, 0), (0, 0), (0, wq - tq)))
    f_blk = f_new.reshape(b, heads, nq, wq)
    in_specs = [pl.BlockSpec((None, tq, hd), lambda bi, h, i: (bi, i, h)),
                pl.BlockSpec((None, l, hd), lambda bi, h, i: (bi, 0, h)),
                pl.BlockSpec((None, l, hd), lambda bi, h, i: (bi, 0, h)),
                pl.BlockSpec((None, None, None, 1, wq), lambda bi, h, i: (bi, h, i, 0, 0)),
                pl.BlockSpec((None, None, nq, wq), lambda bi, h, i: (bi, h, 0, 0))]
    args = [q, k, v, f_blk.reshape(b, heads, nq, 1, wq), f_blk]
    n_past = 0
    vmem = 2 * (2 * tq * hd * 2 + 2 * l * hd * 2) + 6 * tq * max(tq, tkp) * 4
    if past is not None:
        kp, vp, fp = past
        p_len = kp.shape[1]
        n_past = p_len // tkp
        in_specs += [pl.BlockSpec((None, p_len, hd), lambda bi, h, i: (bi, 0, h)),
                     pl.BlockSpec((None, p_len, hd), lambda bi, h, i: (bi, 0, h)),
                     pl.BlockSpec((None, None, n_past, tkp), lambda bi, h, i: (bi, h, 0, 0))]
        args += [kp, vp, fp.reshape(b, heads, n_past, tkp)]
        vmem += 2 * 2 * p_len * hd * 4
    return pl.pallas_call(
        functools.partial(_fox_body, tq=tq, tkp=tkp, n_past=n_past),
        grid=(b, heads, nq), in_specs=in_specs,
        out_specs=pl.BlockSpec((None, tq, hd), lambda bi, h, i: (bi, i, h)),
        out_shape=jax.ShapeDtypeStruct((b, l, hw), BF16),
        compiler_params=_params(("parallel", "parallel", "arbitrary"), vmem), name="fox_attention")(*args)


def _gdn_body(qkv_ref, z_ref, gates_ref, cprev_ref, cw_ref, s0_ref, og_ref, o_ref, st_ref, ybuf_ref,
              *, c, heads, dk):
    n = pl.program_id(1)
    pad = SUBLANE

    @pl.when(n == 0)
    def _():
        ybuf_ref[0:pad, :] = cprev_ref[...]
        st_ref[...] = s0_ref[...]

    u_in = qkv_ref[...]
    ybuf_ref[pad:pad + c, :] = u_in
    conv = u_in * cw_ref[CONV_W - 1:CONV_W, :]
    for k in range(1, CONV_W):
        conv = conv + ybuf_ref[pad - k:pad - k + c, :] * cw_ref[CONV_W - 1 - k:CONV_W - k, :]
    ybuf_ref[0:pad, :] = ybuf_ref[c:c + pad, :]
    act = conv * _sigmoid(conv)

    row = lax.broadcasted_iota(jnp.int32, (c, c), 0)
    col = lax.broadcasted_iota(jnp.int32, (c, c), 1)
    incl = row >= col
    strict = row > col
    eye = (row == col).astype(F32)
    gs = gates_ref[...]
    cs = jnp.dot(incl.astype(F32), gs, preferred_element_type=F32, precision=HI)
    lane = lax.broadcasted_iota(jnp.int32, (c, LANE), 1)
    z = z_ref[...]
    hw = heads * dk
    n_double = int(math.log2(c)) - 1

    for h in range(heads):
        sl = slice(h * dk, (h + 1) * dk)
        qh = act[:, sl]
        kh = act[:, hw + h * dk:hw + (h + 1) * dk]
        vh = act[:, 2 * hw + h * dk:2 * hw + (h + 1) * dk]
        qh = qh * lax.rsqrt(jnp.sum(qh * qh, axis=-1, keepdims=True) + EPS) * (dk ** -0.5)
        kh = kh * lax.rsqrt(jnp.sum(kh * kh, axis=-1, keepdims=True) + EPS)
        beta = gs[:, h:h + 1]
        gc = cs[:, heads + h:heads + h + 1]
        sel = (lane == heads + h).astype(F32)
        gc_row = _dot_nt(sel, cs * sel, precision=HI)
        decay = jnp.exp(jnp.where(incl, gc - gc_row, NEG))
        kb = kh * beta
        khb = kh.astype(BF16)
        lmat = jnp.where(strict, _dot_nt(kb.astype(BF16), khb) * decay, 0.0)
        tinv = eye - lmat
        pw = lmat
        for _ in range(n_double):
            pw = jnp.dot(pw, pw, preferred_element_type=F32, precision=HI)
            tinv = tinv + jnp.dot(tinv, pw, preferred_element_type=F32, precision=HI)
        egc = jnp.exp(gc)
        u = jnp.dot(tinv, vh * beta, preferred_element_type=F32, precision=HI)
        w = jnp.dot(tinv, kb * egc, preferred_element_type=F32, precision=HI)
        attn = jnp.where(incl, _dot_nt(qh.astype(BF16), khb) * decay, 0.0)
        s_old = st_ref[h]
        sb = s_old.astype(BF16)
        v_new = u - _dot(w.astype(BF16), sb)
        vnb = v_new.astype(BF16)
        o = _dot((qh * egc).astype(BF16), sb) + _dot(attn.astype(BF16), vnb)
        g_last = gc[c - 1:c, :]
        st_ref[h] = s_old * jnp.exp(g_last) + _dot_tn((kh * jnp.exp(g_last - gc)).astype(BF16), vnb)
        zh = z[:, sl]
        o_ref[:, sl] = (_rms(o, og_ref[...]) * (zh * _sigmoid(zh))).astype(o_ref.dtype)


def gated_delta_net(qkv, z, gates, conv_prev, conv_w, s0, onorm_g, heads, dk):
    b, l, cd = qkv.shape
    c = min(CHUNK, l)
    hw = heads * dk
    cprev = jnp.pad(conv_prev, ((0, 0), (SUBLANE - (CONV_W - 1), 0), (0, 0)))
    cw = jnp.pad(conv_w, ((0, SUBLANE - CONV_W), (0, 0)))
    vmem = 2 * (c * cd * 4 + c * hw * 4 + c * LANE * 4 + 2 * SUBLANE * cd * 4 + 2 * heads * dk * dk * 4
                + c * hw * 2) + (c + SUBLANE) * cd * 4 * 4
    return pl.pallas_call(
        functools.partial(_gdn_body, c=c, heads=heads, dk=dk),
        grid=(b, l // c),
        in_specs=[pl.BlockSpec((None, c, cd), lambda i, n: (i, n, 0)),
                  pl.BlockSpec((None, c, hw), lambda i, n: (i, n, 0)),
                  pl.BlockSpec((None, c, LANE), lambda i, n: (i, n, 0)),
                  pl.BlockSpec((None, SUBLANE, cd), lambda i, n: (i, 0, 0)),
                  pl.BlockSpec((SUBLANE, cd), lambda i, n: (0, 0)),
                  pl.BlockSpec((None, heads, dk, dk), lambda i, n: (i, 0, 0, 0)),
                  pl.BlockSpec((1, dk), lambda i, n: (0, 0))],
        out_specs=[pl.BlockSpec((None, c, hw), lambda i, n: (i, n, 0)),
                   pl.BlockSpec((None, heads, dk, dk), lambda i, n: (i, 0, 0, 0))],
        out_shape=[jax.ShapeDtypeStruct((b, l, hw), BF16),
                   jax.ShapeDtypeStruct((b, heads, dk, dk), F32)],
        scratch_shapes=[pltpu.VMEM((c + SUBLANE, cd), F32)],
        compiler_params=_params(("parallel", "arbitrary"), vmem), name="gated_delta_net")(
            qkv, z, gates, cprev, cw, s0, onorm_g.reshape(1, dk))


def _swa_body(*refs, n_seg, cq, hkv, grp, hd, chunked):
    sink_ref, q_ref = refs[0], refs[1]
    k_refs, v_refs, o_ref = refs[2:2 + n_seg], refs[2 + n_seg:2 + 2 * n_seg], refs[2 + 2 * n_seg]
    kk = jnp.concatenate([r[...] for r in k_refs], axis=0).astype(BF16)
    vv = jnp.concatenate([r[...] for r in v_refs], axis=0).astype(BF16)
    kb_len = kk.shape[0]
    qb = q_ref[...].astype(BF16)
    row = lax.broadcasted_iota(jnp.int32, (cq, kb_len), 0)
    col = lax.broadcasted_iota(jnp.int32, (cq, kb_len), 1)
    dist = jnp.abs(row - col + WINDOW).astype(F32)
    if chunked:
        n = pl.program_id(1)
        dist = jnp.where(col >= (n_seg - 1 - n) * cq, dist, -NEG)
    hq = hkv * grp
    for j in range(hkv):
        kj = kk[:, j * hd:(j + 1) * hd]
        vj = vv[:, j * hd:(j + 1) * hd]
        q_stack = jnp.concatenate([qb[:, (j * grp + g) * hd:(j * grp + g + 1) * hd] for g in range(grp)], axis=0)
        s = _dot_nt(q_stack, kj) * (hd ** -0.5)
        outs = []
        for g in range(grp):
            h = j * grp + g
            slope = 2.0 ** (-8.0 * (h + 1) / hq)
            sg = s[g * cq:(g + 1) * cq, :] - slope * dist
            sink = sink_ref[h]
            m = jnp.maximum(jnp.max(sg, axis=-1, keepdims=True), sink)
            p = jnp.exp(sg - m)
            l = jnp.sum(p, axis=-1, keepdims=True) + jnp.exp(sink - m)
            outs.append((_dot(p.astype(BF16), vj) / l).astype(o_ref.dtype))
        o_ref[:, j * grp * hd:(j + 1) * grp * hd] = jnp.concatenate(outs, axis=-1)


def swa_attention(q, segs, sinks, hkv, grp, hd, chunked):
    b, l, qw = q.shape
    kw = hkv * hd
    if chunked:
        cq, n_seg = CHUNK, WINDOW // CHUNK + 1
        k, v = segs
        def seg_spec(s):
            return pl.BlockSpec((None, cq, kw), lambda i, n: (i, jnp.maximum(n - (n_seg - 1) + s, 0), 0))
        k_specs = [seg_spec(s) for s in range(n_seg)]
        k_args, v_args = [k] * n_seg, [v] * n_seg
    else:
        cq, n_seg = l, len(segs)
        k_specs = [pl.BlockSpec((None, kk.shape[1], kw), lambda i, n: (i, 0, 0)) for kk, _ in segs]
        k_args, v_args = [kk for kk, _ in segs], [vv for _, vv in segs]
    vmem = 2 * (cq * qw * 4 + cq * qw * 2 + 2 * sum(a.shape[1] if not chunked else cq for a in k_args) * kw * 4)
    vmem += 16 * grp * cq * 256 * 4
    return pl.pallas_call(
        functools.partial(_swa_body, n_seg=n_seg, cq=cq, hkv=hkv, grp=grp, hd=hd, chunked=chunked),
        grid=(b, l // cq),
        in_specs=[pl.BlockSpec(memory_space=pltpu.SMEM),
                  pl.BlockSpec((None, cq, qw), lambda i, n: (i, n, 0))] + k_specs + k_specs,
        out_specs=pl.BlockSpec((None, cq, qw), lambda i, n: (i, n, 0)),
        out_shape=jax.ShapeDtypeStruct((b, l, qw), BF16),
        compiler_params=_params(("parallel", "parallel"), vmem), name="swa_attention")(
            sinks, q, *k_args, *v_args)


def _ab_layer(x, conv_prev, s0, past, w_in, conv_w, a_log, dt_bias, onorm_g, f_bias, w_out, g_mix, dims):
    b, l, d = x.shape
    h_a, dk, h_b, hd_b = dims
    a_qk, a_v, b_w = h_a * dk, h_a * dk, h_b * hd_b
    conv_dim = 2 * a_qk + a_v
    o1 = conv_dim + a_v
    o2 = o1 + 2 * h_a
    small = jnp.concatenate([w_in[:, o1:o2], w_in[:, o2 + 3 * b_w:],
                             jnp.zeros((d, LANE - 2 * h_a - h_b), F32)], axis=1)
    w_pack = jnp.concatenate([w_in[:, :o1], w_in[:, o2:o2 + 3 * b_w], small], axis=1).astype(BF16)
    zeros4 = jnp.zeros((h_a,), F32)
    gp = jnp.zeros((SUBLANE, LANE), F32)
    gp = gp.at[0, :3 * h_a].set(jnp.concatenate([zeros4, dt_bias, f_bias]))
    gp = gp.at[1, h_a:2 * h_a].set(a_log)
    f32o, b16 = (F32, 1.0), (BF16, 1.0)
    groups = ((conv_dim, (f32o,)), (a_v, (f32o,)), (b_w, ((BF16, hd_b ** -0.5),)),
              (b_w, (f32o, b16)), (b_w, (f32o, b16)), (LANE, (f32o,)))
    qkv, z, qb, kb, kb16, vb, vb16, gates = norm_matmul(x.reshape(b * l, d), g_mix, w_pack, groups, gate_params=gp)
    r3 = lambda a: a.reshape(b, l, a.shape[-1])
    qkv3, gates3 = r3(qkv), r3(gates)
    oa, s_new = gated_delta_net(qkv3, r3(z), gates3, conv_prev, conv_w, s0, onorm_g, h_a, dk)
    conv_new = qkv3[:, l - (CONV_W - 1):, :]
    logf = gates3[:, :, 2 * h_a:2 * h_a + h_b]
    lf_rows = jnp.pad(logf.transpose(0, 2, 1).reshape(b * h_b, l), ((0, 0), (0, -l % LANE)))
    f_new = cumsum_rows(lf_rows)[:, :l].reshape(b, h_b, l)
    fox_past = None
    if past is not None:
        pk, pv, plf = past
        p_len = pk.shape[1]
        f_past = cumsum_rows(plf.transpose(0, 2, 1).reshape(b * h_b, p_len), minus_total=True)
        fox_past = (pk.reshape(b, p_len, b_w), pv.reshape(b, p_len, b_w), f_past.reshape(b, h_b, p_len))
    ob = fox_attention(r3(qb), r3(kb16), r3(vb16), f_new, h_b, past=fox_past)
    w_out16 = w_out.astype(BF16)
    x_new = matmul_residual(x.reshape(b * l, d), [oa.reshape(b * l, a_v), ob.reshape(b * l, b_w)],
                            [w_out16[:a_v], w_out16[a_v:]])
    return (x_new.reshape(b, l, d), s_new, conv_new, kb.reshape(b, l, h_b, hd_b), vb.reshape(b, l, h_b, hd_b), logf)


def _c_layer(x, past, w_in, sinks, w_out, g_mix, dims):
    b, l, d = x.shape
    hq, hkv, hd = dims
    grp = hq // hkv
    f32o = (F32, 1.0)
    q, k, v = norm_matmul(x.reshape(b * l, d), g_mix, w_in.astype(BF16),
                          ((hq * hd, (f32o,)), (hkv * hd, (f32o,)), (hkv * hd, (f32o,))))
    q3, k3, v3 = q.reshape(b, l, hq * hd), k.reshape(b, l, hkv * hd), v.reshape(b, l, hkv * hd)
    if past is None:
        o = swa_attention(q3, (k3, v3), sinks, hkv, grp, hd, chunked=True)
        new_k, new_v = k3[:, l - WINDOW:], v3[:, l - WINDOW:]
    else:
        pk, pv = past
        w_len = pk.shape[1]
        o = swa_attention(q3, ((pk.reshape(b, w_len, hkv * hd), pv.reshape(b, w_len, hkv * hd)), (k3, v3)),
                          sinks, hkv, grp, hd, chunked=False)
        new_k, new_v = k3, v3
    x_new = matmul_residual(x.reshape(b * l, d), [o.reshape(b * l, hq * hd)], [w_out.astype(BF16)])
    return x_new.reshape(b, l, d), new_k.reshape(b, -1, hkv, hd), new_v.reshape(b, -1, hkv, hd)


def kernel(x_prompt, mem_prompt, x_sample, state_gdn, state_gdn_conv, cache_fox_k, cache_fox_v, cache_fox_logf, cache_swa_k, cache_swa_v, cache_mem_k, cache_mem_v, g_mix, w_in_ab, conv_w_a, a_log_a, dt_bias_a, onorm_g_a, f_bias_b, w_out_ab, w_in_c, sinks_c, w_out_c, g_mem, w_xkv, g_cross, w_xq, w_xo, g_mlp, w_up, w_down, g_final):
    depth = g_mix.shape[0]
    bp, lp, d = x_prompt.shape
    bs, ls, _ = x_sample.shape
    h_a, dk = state_gdn.shape[2], state_gdn.shape[3]
    h_b, hd_b = cache_fox_k.shape[3], cache_fox_k.shape[4]
    hkv, hd_c = cache_swa_k.shape[3], cache_swa_k.shape[4]
    hq = sinks_c.shape[1]
    m_len, h_x, hd_x = cache_mem_k.shape[2], cache_mem_k.shape[3], cache_mem_k.shape[4]
    xw = h_x * hd_x
    conv_dim = state_gdn_conv.shape[-1]

    xp, xs = x_prompt, x_sample
    p_out = {k: [] for k in ("s", "c", "fk", "fv", "fl", "wk", "wv", "mk", "mv")}
    s_out = {k: [] for k in ("s", "c", "fk", "fv", "fl", "wk", "wv")}
    for layer in range(depth):
        i = layer // 2
        if layer % 2 == 0:
            wts = (w_in_ab[i], conv_w_a[i], a_log_a[i], dt_bias_a[i], onorm_g_a[i], f_bias_b[i], w_out_ab[i],
                   g_mix[layer], (h_a, dk, h_b, hd_b))
            xp, st, cv, fk, fv, fl = _ab_layer(xp, jnp.zeros((bp, CONV_W - 1, conv_dim), F32),
                                               jnp.zeros((bp, h_a, dk, dk), F32), None, *wts)
            for key, val in zip(("s", "c", "fk", "fv", "fl"), (st, cv, fk, fv, fl)):
                p_out[key].append(val)
            xs, st, cv, fk, fv, fl = _ab_layer(xs, state_gdn_conv[i], state_gdn[i],
                                               (cache_fox_k[i], cache_fox_v[i], cache_fox_logf[i]), *wts)
            for key, val in zip(("s", "c", "fk", "fv", "fl"), (st, cv, fk, fv, fl)):
                s_out[key].append(val)
        else:
            wts = (w_in_c[i], sinks_c[i], w_out_c[i], g_mix[layer], (hq, hkv, hd_c))
            xp, wk, wv = _c_layer(xp, None, *wts)
            p_out["wk"].append(wk); p_out["wv"].append(wv)
            xs, wk, wv = _c_layer(xs, (cache_swa_k[i], cache_swa_v[i]), *wts)
            s_out["wk"].append(wk); s_out["wv"].append(wv)
        f32o = (F32, 1.0)
        mk, mv = norm_matmul(mem_prompt.reshape(bp * m_len, d), g_mem[layer], w_xkv[layer].astype(BF16),
                             ((xw, (f32o,)), (xw, (f32o,))))
        p_out["mk"].append(mk.reshape(bp, m_len, h_x, hd_x)); p_out["mv"].append(mv.reshape(bp, m_len, h_x, hd_x))
        wq16, wo16 = w_xq[layer].astype(BF16), w_xo[layer].astype(BF16)
        xp = cross_attend(xp, g_cross[layer], wq16, mk.reshape(bp, m_len, xw), mv.reshape(bp, m_len, xw), wo16, h_x)
        xs = cross_attend(xs, g_cross[layer], wq16, cache_mem_k[layer].reshape(bs, m_len, xw),
                          cache_mem_v[layer].reshape(bs, m_len, xw), wo16, h_x)
        wu16, wd16 = w_up[layer].astype(BF16), w_down[layer].astype(BF16)
        gf = g_final if layer == depth - 1 else None
        xp = mlp(xp.reshape(bp * lp, d), g_mlp[layer], wu16, wd16, gf).reshape(bp, lp, d)
        xs = mlp(xs.reshape(bs * ls, d), g_mlp[layer], wu16, wd16, gf).reshape(bs, ls, d)
    st = jnp.stack
    return (xp, xs,
            st(p_out["s"]), st(p_out["c"]), st(p_out["fk"]), st(p_out["fv"]), st(p_out["fl"]),
            st(p_out["wk"]), st(p_out["wv"]), st(p_out["mk"]), st(p_out["mv"]),
            st(s_out["s"]), st(s_out["c"]), st(s_out["fk"]), st(s_out["fv"]), st(s_out["fl"]),
            st(s_out["wk"]), st(s_out["wv"]))
```

```python
import functools
import math

import jax
import jax.numpy as jnp
from jax import lax
from jax.experimental import pallas as pl
from jax.experimental.pallas import tpu as pltpu

F32 = jnp.float32
BF16 = jnp.bfloat16
HI = lax.Precision.HIGHEST
EPS = 1e-6
NEG = -1e30

CHUNK = 64
CONV_W = 4
WINDOW = 128
LANE = 128
SUBLANE = 8
VMEM_BYTES = 64 * 1024 * 1024


def _params(semantics, vmem_bytes):
    limit = int(min(max(vmem_bytes * 1.25, 16 * 1024 * 1024), VMEM_BYTES - 8 * 1024 * 1024))
    return pltpu.CompilerParams(dimension_semantics=semantics, vmem_limit_bytes=limit)


def _dot(a, b):
    return jnp.dot(a, b, preferred_element_type=F32)


def _dot_nt(a, b, precision=None):
    return lax.dot_general(a, b, (((1,), (1,)), ((), ())), preferred_element_type=F32, precision=precision)


def _dot_tn(a, b):
    return lax.dot_general(a, b, (((0,), (0,)), ((), ())), preferred_element_type=F32)


def _rms(x, g):
    return x * lax.rsqrt(jnp.mean(x * x, axis=-1, keepdims=True) + EPS) * g


def _sigmoid(x):
    return 1.0 / (1.0 + jnp.exp(-x))


def _softplus(x):
    return jnp.maximum(x, 0.0) + jnp.log(1.0 + jnp.exp(-jnp.abs(x)))


def _token_tile(t, cap):
    tm = min(t, cap)
    assert t % tm == 0
    return tm


def _norm_matmul_body(*refs, groups, gated, col_chunk):
    x_ref, g_ref, w_ref = refs[:3]
    refs = refs[3:]
    if gated:
        gp_ref, refs = refs[0], refs[1:]
    x = x_ref[...]
    yb = _rms(x, g_ref[...]).astype(BF16)
    off, k = 0, 0
    for gi, (n, outs) in enumerate(groups):
        for c in range(0, n, col_chunk):
            cw = min(col_chunk, n - c)
            r = _dot(yb, w_ref[:, off + c:off + c + cw])
            if gated and gi == len(groups) - 1:
                rb = r + gp_ref[0:1, :]
                lane = lax.broadcasted_iota(jnp.int32, rb.shape, 1)
                e = jnp.exp(-jnp.abs(rb))
                l1p = jnp.log(1.0 + e)
                sig = _sigmoid(rb)
                gdec = -jnp.exp(gp_ref[1:2, :]) * (jnp.maximum(rb, 0.0) + l1p)
                logsig = jnp.minimum(rb, 0.0) - l1p
                r = jnp.where(lane < 4, sig, jnp.where(lane < 8, gdec, jnp.where(lane < 12, logsig, 0.0)))
            for j, (dtype, scale) in enumerate(outs):
                o_ref = refs[k + j]
                rr = r if scale == 1.0 else r * scale
                o_ref[:, c:c + cw] = rr.astype(dtype)
        off += n
        k += len(outs)


def norm_matmul(x, g, w, groups, gate_params=None, tile_cap=512):
    t, d = x.shape
    tm = _token_tile(t, tile_cap)
    gated = gate_params is not None
    in_specs = [pl.BlockSpec((tm, d), lambda i: (i, 0)),
                pl.BlockSpec((1, d), lambda i: (0, 0)),
                pl.BlockSpec(w.shape, lambda i: (0, 0))]
    args = [x, g.reshape(1, d), w]
    if gated:
        in_specs.append(pl.BlockSpec(gate_params.shape, lambda i: (0, 0)))
        args.append(gate_params)
    out_shape, out_specs, out_bytes = [], [], 0
    for n, outs in groups:
        for dtype, _ in outs:
            out_shape.append(jax.ShapeDtypeStruct((t, n), dtype))
            out_specs.append(pl.BlockSpec((tm, n), lambda i: (i, 0)))
            out_bytes += tm * n * jnp.dtype(dtype).itemsize
    vmem = 2 * (tm * d * 4 + w.size * 2 + out_bytes) + tm * d * 8
    return pl.pallas_call(
        functools.partial(_norm_matmul_body, groups=groups, gated=gated, col_chunk=512),
        grid=(t // tm,), in_specs=in_specs, out_specs=out_specs, out_shape=out_shape,
        compiler_params=_params(("parallel",), vmem), name="norm_matmul")(*args)


def _matmul_residual_body(*refs, n_in, col_chunk):
    x_ref = refs[0]
    a_refs, w_refs, o_ref = refs[1:1 + n_in], refs[1 + n_in:1 + 2 * n_in], refs[1 + 2 * n_in]
    a_vals = [a[...].astype(BF16) for a in a_refs]
    d = x_ref.shape[-1]
    for c in range(0, d, col_chunk):
        acc = x_ref[:, c:c + col_chunk]
        for a, w in zip(a_vals, w_refs):
            acc = acc + _dot(a, w[:, c:c + col_chunk])
        o_ref[:, c:c + col_chunk] = acc


def matmul_residual(x, a_list, w_list, tile_cap=512):
    t, d = x.shape
    tm = _token_tile(t, tile_cap)
    n_in = len(a_list)
    in_specs = [pl.BlockSpec((tm, d), lambda i: (i, 0))]
    in_specs += [pl.BlockSpec((tm, a.shape[1]), lambda i: (i, 0)) for a in a_list]
    in_specs += [pl.BlockSpec(w.shape, lambda i: (0, 0)) for w in w_list]
    vmem = 2 * (2 * tm * d * 4 + sum(tm * a.shape[1] * a.dtype.itemsize for a in a_list)
                + sum(w.size * 2 for w in w_list))
    return pl.pallas_call(
        functools.partial(_matmul_residual_body, n_in=n_in, col_chunk=512),
        grid=(t // tm,), in_specs=in_specs, out_specs=pl.BlockSpec((tm, d), lambda i: (i, 0)),
        out_shape=jax.ShapeDtypeStruct((t, d), F32),
        compiler_params=_params(("parallel",), vmem), name="matmul_residual")(x, *a_list, *w_list)


def _cross_body(x_ref, g_ref, wq_ref, mk_ref, mv_ref, wo_ref, o_ref, *, heads, hd):
    x = x_ref[...]
    hb = _rms(x, g_ref[...]).astype(BF16)
    q = _dot(hb, wq_ref[...]) * (hd ** -0.5)
    outs = []
    for h in range(heads):
        sl = slice(h * hd, (h + 1) * hd)
        qh = q[:, sl].astype(BF16)
        kh = mk_ref[:, sl].astype(BF16)
        vh = mv_ref[:, sl].astype(BF16)
        s = _dot_nt(qh, kh)
        m = jnp.max(s, axis=-1, keepdims=True)
        p = jnp.exp(s - m)
        l = jnp.sum(p, axis=-1, keepdims=True)
        outs.append((_dot(p.astype(BF16), vh) / l).astype(BF16))
    oc = jnp.concatenate(outs, axis=-1)
    o_ref[...] = x + _dot(oc, wo_ref[...])


def cross_attend(x, g, wq, mk, mv, wo, heads, tile_cap=512):
    b, l, d = x.shape
    m, hw = mk.shape[1], mk.shape[2]
    tm = _token_tile(l, tile_cap)
    vmem = 2 * (2 * tm * d * 4 + 2 * m * hw * 4 + wq.size * 2 + wo.size * 2) + tm * (d * 6 + hw * 8 + m * 8)
    return pl.pallas_call(
        functools.partial(_cross_body, heads=heads, hd=hw // heads),
        grid=(b, l // tm),
        in_specs=[pl.BlockSpec((None, tm, d), lambda i, j: (i, j, 0)),
                  pl.BlockSpec((1, d), lambda i, j: (0, 0)),
                  pl.BlockSpec(wq.shape, lambda i, j: (0, 0)),
                  pl.BlockSpec((None, m, hw), lambda i, j: (i, 0, 0)),
                  pl.BlockSpec((None, m, hw), lambda i, j: (i, 0, 0)),
                  pl.BlockSpec(wo.shape, lambda i, j: (0, 0))],
        out_specs=pl.BlockSpec((None, tm, d), lambda i, j: (i, j, 0)),
        out_shape=jax.ShapeDtypeStruct((b, l, d), F32),
        compiler_params=_params(("parallel", "parallel"), vmem), name="cross_attend")(
            x, g.reshape(1, d), wq, mk, mv, wo)


def _mlp_body(*refs, ff_chunk, final_norm):
    if final_norm:
        x_ref, g_ref, wu_ref, wd_ref, gf_ref, o_ref = refs
    else:
        x_ref, g_ref, wu_ref, wd_ref, o_ref = refs
    x = x_ref[...]
    hb = _rms(x, g_ref[...]).astype(BF16)
    acc = x
    for c in range(0, wu_ref.shape[1], ff_chunk):
        u = jnp.maximum(_dot(hb, wu_ref[:, c:c + ff_chunk]), 0.0)
        acc = acc + _dot((u * u).astype(BF16), wd_ref[c:c + ff_chunk, :])
    if final_norm:
        acc = _rms(acc, gf_ref[...])
    o_ref[...] = acc


def mlp(x, g, w_up, w_down, g_final=None, tile_cap=512):
    t, d = x.shape
    tm = _token_tile(t, tile_cap)
    final_norm = g_final is not None
    in_specs = [pl.BlockSpec((tm, d), lambda i: (i, 0)),
                pl.BlockSpec((1, d), lambda i: (0, 0)),
                pl.BlockSpec(w_up.shape, lambda i: (0, 0)),
                pl.BlockSpec(w_down.shape, lambda i: (0, 0))]
    args = [x, g.reshape(1, d), w_up, w_down]
    if final_norm:
        in_specs.append(pl.BlockSpec((1, d), lambda i: (0, 0)))
        args.append(g_final.reshape(1, d))
    vmem = 2 * (2 * tm * d * 4 + w_up.size * 2 + w_down.size * 2) + tm * (d * 10 + 512 * 8)
    return pl.pallas_call(
        functools.partial(_mlp_body, ff_chunk=512, final_norm=final_norm),
        grid=(t // tm,), in_specs=in_specs, out_specs=pl.BlockSpec((tm, d), lambda i: (i, 0)),
        out_shape=jax.ShapeDtypeStruct((t, d), F32),
        compiler_params=_params(("parallel",), vmem), name="mlp")(*args)


def _cumsum_body(x_ref, o_ref, *, minus_total):
    r, l = x_ref.shape
    row = lax.broadcasted_iota(jnp.int32, (LANE, LANE), 0)
    col = lax.broadcasted_iota(jnp.int32, (LANE, LANE), 1)
    upper = (row <= col).astype(F32)
    carry = jnp.zeros((r, 1), F32)
    for c in range(0, l, LANE):
        y = jnp.dot(x_ref[:, c:c + LANE], upper, preferred_element_type=F32, precision=HI) + carry
        o_ref[:, c:c + LANE] = y
        carry = y[:, LANE - 1:LANE]
    if minus_total:
        o_ref[...] = o_ref[...] - carry


def cumsum_rows(x, minus_total=False):
    r, l = x.shape
    assert l % LANE == 0 and r % SUBLANE == 0
    return pl.pallas_call(
        functools.partial(_cumsum_body, minus_total=minus_total),
        grid=(1,), in_specs=[pl.BlockSpec((r, l), lambda i: (0, 0))],
        out_specs=pl.BlockSpec((r, l), lambda i: (0, 0)),
        out_shape=jax.ShapeDtypeStruct((r, l), F32),
        compiler_params=_params(("arbitrary",), 4 * r * l * 4), name="cumsum_rows")(x)


def _fox_body(*refs, tq, tkp, n_past, nsub):
    if n_past:
        q_ref, k_ref, v_ref, fk_ref, kp_ref, vp_ref, fp_ref, o_ref = refs
    else:
        q_ref, k_ref, v_ref, fk_ref, o_ref = refs
    i = pl.program_id(2)
    sub = tq // nsub
    hd = q_ref.shape[-1]
    qs = [q_ref[c * sub:(c + 1) * sub, :] for c in range(nsub)]

    def update(carry, ss, vbs):
        ms, ls, accs = carry[0::3], carry[1::3], carry[2::3]
        m_new = [jnp.maximum(m, jnp.max(s, axis=-1, keepdims=True)) for m, s in zip(ms, ss)]
        alpha = [jnp.exp(m - mn) for m, mn in zip(ms, m_new)]
        ps = [jnp.exp(s - mn) for s, mn in zip(ss, m_new)]
        ls = [a * l + jnp.sum(p, axis=-1, keepdims=True) for a, l, p in zip(alpha, ls, ps)]
        pv = [_dot(p.astype(BF16), vb) for p, vb in zip(ps, vbs)]
        accs = [a * acc + x for a, acc, x in zip(alpha, accs, pv)]
        out = []
        for c in range(nsub):
            out += [m_new[c], ls[c], accs[c]]
        return tuple(out)

    carry = (jnp.full((sub, 1), NEG, F32), jnp.zeros((sub, 1), F32), jnp.zeros((sub, hd), F32)) * nsub

    if n_past:
        def past_step(j, carry):
            start = pl.multiple_of(j * tkp, tkp)
            kb = kp_ref[pl.ds(start, tkp), :].astype(BF16)
            vb = vp_ref[pl.ds(start, tkp), :].astype(BF16)
            fk = fp_ref[pl.ds(j, 1), :]
            return update(carry, [_dot_nt(q, kb) - fk for q in qs], [vb] * nsub)
        carry = lax.fori_loop(0, n_past, past_step, carry)

    def self_step(j, carry):
        start = pl.multiple_of(j * tq, tq)
        kb = k_ref[pl.ds(start, tq), :]
        vb = v_ref[pl.ds(start, tq), :]
        fk = fk_ref[pl.ds(j, 1), :][:, :tq]
        return update(carry, [_dot_nt(q, kb) - fk for q in qs], [vb] * nsub)
    carry = lax.fori_loop(0, i, self_step, carry)

    start = pl.multiple_of(i * tq, tq)
    fk = fk_ref[pl.ds(i, 1), :]
    ss, vbs = [], []
    for c in range(nsub):
        w = (c + 1) * sub
        s = _dot_nt(qs[c], k_ref[pl.ds(start, w), :]) - fk[:, :w]
        row = lax.broadcasted_iota(jnp.int32, (sub, w), 0) + c * sub
        col = lax.broadcasted_iota(jnp.int32, (sub, w), 1)
        ss.append(jnp.where(col <= row, s, NEG))
        vbs.append(v_ref[pl.ds(start, w), :])
    carry = update(carry, ss, vbs)
    for c in range(nsub):
        o_ref[c * sub:(c + 1) * sub, :] = (carry[3 * c + 2] / carry[3 * c + 1]).astype(o_ref.dtype)


def fox_attention(q, k, v, f_new, heads, past=None, tq_cap=512, sub_cap=256, tkp=256):
    b, l, hw = q.shape
    hd = hw // heads
    tq = min(l, tq_cap)
    nq = l // tq
    nsub = max(1, tq // sub_cap)
    wq = -(-tq // LANE) * LANE
    if wq != tq:
        assert nq == 1
        f_new = jnp.pad(f_new, ((0, 0), (0, 0), (0, wq - tq)))
    in_specs = [pl.BlockSpec((None, tq, hd), lambda bi, h, i: (bi, i, h)),
                pl.BlockSpec((None, l, hd), lambda bi, h, i: (bi, 0, h)),
                pl.BlockSpec((None, l, hd), lambda bi, h, i: (bi, 0, h)),
                pl.BlockSpec((None, None, nq, wq), lambda bi, h, i: (bi, h, 0, 0))]
    args = [q, k, v, f_new.reshape(b, heads, nq, wq)]
    n_past = 0
    vmem = 2 * (2 * tq * hd * 2 + 2 * l * hd * 2) + 8 * tq * max(tq, tkp) * 4
    if past is not None:
        kp, vp, fp = past
        p_len = kp.shape[1]
        n_past = p_len // tkp
        in_specs += [pl.BlockSpec((None, p_len, hd), lambda bi, h, i: (bi, 0, h)),
                     pl.BlockSpec((None, p_len, hd), lambda bi, h, i: (bi, 0, h)),
                     pl.BlockSpec((None, None, n_past, tkp), lambda bi, h, i: (bi, h, 0, 0))]
        args += [kp, vp, fp.reshape(b, heads, n_past, tkp)]
        vmem += 2 * 2 * p_len * hd * 4
    return pl.pallas_call(
        functools.partial(_fox_body, tq=tq, tkp=tkp, n_past=n_past, nsub=nsub),
        grid=(b, heads, nq), in_specs=in_specs,
        out_specs=pl.BlockSpec((None, tq, hd), lambda bi, h, i: (bi, i, h)),
        out_shape=jax.ShapeDtypeStruct((b, l, hw), BF16),
        compiler_params=_params(("parallel", "parallel", "arbitrary"), vmem), name="fox_attention")(*args)


def _gdn_body(qkv_ref, z_ref, gates_ref, cprev_ref, cw_ref, s0_ref, og_ref, o_ref, st_ref, ybuf_ref,
              *, c, nc, heads, dk):
    n = pl.program_id(1)
    pad = SUBLANE
    r = c * nc
    hw = heads * dk

    @pl.when(n == 0)
    def _():
        ybuf_ref[0:pad, :] = cprev_ref[...]
        st_ref[...] = s0_ref[...]

    ybuf_ref[pad:pad + r, :] = qkv_ref[...]

    row = lax.broadcasted_iota(jnp.int32, (c, c), 0)
    col = lax.broadcasted_iota(jnp.int32, (c, c), 1)
    incl = row >= col
    strict = row > col
    tril = incl.astype(BF16)
    n_double = int(math.log2(c)) - 1
    og = og_ref[...]

    chains = []
    for ci in range(nc):
        r0 = ci * c
        conv = qkv_ref[r0:r0 + c, :] * cw_ref[CONV_W - 1:CONV_W, :]
        for k in range(1, CONV_W):
            conv = conv + ybuf_ref[pad + r0 - k:pad + r0 - k + c, :] * cw_ref[CONV_W - 1 - k:CONV_W - k, :]
        act = conv * _sigmoid(conv)
        gs = gates_ref[r0:r0 + c, :]
        g1 = gs.astype(BF16)
        res = gs - g1.astype(F32)
        g2 = res.astype(BF16)
        g3 = (res - g2.astype(F32)).astype(BF16)
        cs3 = _dot(tril, jnp.concatenate([g1, g2, g3], axis=-1))
        cs = cs3[:, :LANE] + cs3[:, LANE:2 * LANE] + cs3[:, 2 * LANE:]
        cs_t = jnp.concatenate([cs, jnp.zeros((LANE - c, LANE), F32)], axis=0).T
        for h in range(heads):
            qh = act[:, h * dk:(h + 1) * dk]
            kh = act[:, hw + h * dk:hw + (h + 1) * dk]
            vh = act[:, 2 * hw + h * dk:2 * hw + (h + 1) * dk]
            qh = qh * lax.rsqrt(jnp.sum(qh * qh, axis=-1, keepdims=True) + EPS) * (dk ** -0.5)
            kh = kh * lax.rsqrt(jnp.sum(kh * kh, axis=-1, keepdims=True) + EPS)
            beta = gs[:, h:h + 1]
            gc = cs[:, heads + h:heads + h + 1]
            gc_row = cs_t[heads + h:heads + h + 1, :c]
            kb = kh * beta
            egc = jnp.exp(gc)
            g_last = gc[c - 1:c, :]
            chains.append(dict(
                decay=jnp.exp(jnp.where(incl, gc - gc_row, NEG)),
                kq=jnp.concatenate([kb, qh], axis=0).astype(BF16), khb=kh.astype(BF16),
                rhs=jnp.concatenate([vh * beta, kb * egc], axis=-1), qg=qh * egc,
                kd=(kh * jnp.exp(g_last - gc)).astype(BF16), ds=jnp.exp(g_last)))

    for ch in chains:
        ch["a1"] = _dot_nt(ch["kq"], ch["khb"])
    for ch in chains:
        lmat = jnp.where(strict, ch["a1"][:c] * ch["decay"], 0.0)
        ch["attn"] = jnp.where(incl, ch["a1"][c:] * ch["decay"], 0.0).astype(BF16)
        ch["toff"] = -lmat
        ch["pw"] = lmat
    for _ in range(n_double):
        for ch in chains:
            pwb = ch["pw"].astype(BF16)
            ch["pw"] = _dot(pwb, pwb)
        for ch in chains:
            ch["tp"] = _dot(ch["toff"].astype(BF16), ch["pw"].astype(BF16))
        for ch in chains:
            ch["toff"] = ch["toff"] + ch["pw"] + ch["tp"]
    for ch in chains:
        ch["uw"] = _dot(ch["toff"].astype(BF16), ch["rhs"].astype(BF16))
    for ch in chains:
        uw = ch["rhs"] + ch["uw"]
        ch["u"] = uw[:, :dk]
        ch["wq"] = jnp.concatenate([uw[:, dk:], ch["qg"]], axis=0).astype(BF16)

    states = [st_ref[h] for h in range(heads)]
    for ci in range(nc):
        group = chains[ci * heads:(ci + 1) * heads]
        t1 = [_dot(ch["wq"], s.astype(BF16)) for ch, s in zip(group, states)]
        vnb = [(ch["u"] - t[:c]).astype(BF16) for ch, t in zip(group, t1)]
        o2 = [_dot(ch["attn"], v) for ch, v in zip(group, vnb)]
        sd = [_dot_tn(ch["kd"], v) for ch, v in zip(group, vnb)]
        states = [s * ch["ds"] + d for ch, s, d in zip(group, states, sd)]
        for h in range(heads):
            o = t1[h][c:] + o2[h]
            zh = z_ref[ci * c:(ci + 1) * c, h * dk:(h + 1) * dk]
            o_ref[ci * c:(ci + 1) * c, h * dk:(h + 1) * dk] = (_rms(o, og) * (zh * _sigmoid(zh))).astype(o_ref.dtype)
    for h in range(heads):
        st_ref[h] = states[h]

    ybuf_ref[0:pad, :] = ybuf_ref[r:r + pad, :]


def gated_delta_net(qkv, z, gates, conv_prev, conv_w, s0, onorm_g, heads, dk, chunks_per_step=4):
    b, l, cd = qkv.shape
    c = min(CHUNK, l)
    nc = min(chunks_per_step, l // c)
    r = c * nc
    hw = heads * dk
    cprev = jnp.pad(conv_prev, ((0, 0), (SUBLANE - (CONV_W - 1), 0), (0, 0)))
    cw = jnp.pad(conv_w, ((0, SUBLANE - CONV_W), (0, 0)))
    vmem = 2 * (r * cd * 4 + r * hw * 4 + r * LANE * 4 + 2 * SUBLANE * cd * 4 + 2 * heads * dk * dk * 4
                + r * hw * 2) + (r + SUBLANE) * cd * 4 * 4
    return pl.pallas_call(
        functools.partial(_gdn_body, c=c, nc=nc, heads=heads, dk=dk),
        grid=(b, l // r),
        in_specs=[pl.BlockSpec((None, r, cd), lambda i, n: (i, n, 0)),
                  pl.BlockSpec((None, r, hw), lambda i, n: (i, n, 0)),
                  pl.BlockSpec((None, r, LANE), lambda i, n: (i, n, 0)),
                  pl.BlockSpec((None, SUBLANE, cd), lambda i, n: (i, 0, 0)),
                  pl.BlockSpec((SUBLANE, cd), lambda i, n: (0, 0)),
                  pl.BlockSpec((None, heads, dk, dk), lambda i, n: (i, 0, 0, 0)),
                  pl.BlockSpec((1, dk), lambda i, n: (0, 0))],
        out_specs=[pl.BlockSpec((None, r, hw), lambda i, n: (i, n, 0)),
                   pl.BlockSpec((None, heads, dk, dk), lambda i, n: (i, 0, 0, 0))],
        out_shape=[jax.ShapeDtypeStruct((b, l, hw), BF16),
                   jax.ShapeDtypeStruct((b, heads, dk, dk), F32)],
        scratch_shapes=[pltpu.VMEM((r + SUBLANE, cd), F32)],
        compiler_params=_params(("parallel", "arbitrary"), vmem), name="gated_delta_net")(
            qkv, z, gates, cprev, cw, s0, onorm_g.reshape(1, dk))


def _swa_body(*refs, n_seg, cq, hkv, grp, hd, chunked):
    sink_ref, q_ref = refs[0], refs[1]
    k_refs, v_refs, o_ref = refs[2:2 + n_seg], refs[2 + n_seg:2 + 2 * n_seg], refs[2 + 2 * n_seg]
    kk = jnp.concatenate([r[...] for r in k_refs], axis=0).astype(BF16)
    vv = jnp.concatenate([r[...] for r in v_refs], axis=0).astype(BF16)
    kb_len = kk.shape[0]
    qb = q_ref[...].astype(BF16)
    row = lax.broadcasted_iota(jnp.int32, (cq, kb_len), 0)
    col = lax.broadcasted_iota(jnp.int32, (cq, kb_len), 1)
    dist = jnp.abs(row - col + WINDOW).astype(F32)
    if chunked:
        n = pl.program_id(1)
        dist = jnp.where(col >= (n_seg - 1 - n) * cq, dist, -NEG)
    hq = hkv * grp
    for j in range(hkv):
        kj = kk[:, j * hd:(j + 1) * hd]
        vj = vv[:, j * hd:(j + 1) * hd]
        q_stack = jnp.concatenate([qb[:, (j * grp + g) * hd:(j * grp + g + 1) * hd] for g in range(grp)], axis=0)
        s = _dot_nt(q_stack, kj) * (hd ** -0.5)
        outs = []
        for g in range(grp):
            h = j * grp + g
            slope = 2.0 ** (-8.0 * (h + 1) / hq)
            sg = s[g * cq:(g + 1) * cq, :] - slope * dist
            sink = sink_ref[h]
            m = jnp.maximum(jnp.max(sg, axis=-1, keepdims=True), sink)
            p = jnp.exp(sg - m)
            l = jnp.sum(p, axis=-1, keepdims=True) + jnp.exp(sink - m)
            outs.append((_dot(p.astype(BF16), vj) / l).astype(o_ref.dtype))
        o_ref[:, j * grp * hd:(j + 1) * grp * hd] = jnp.concatenate(outs, axis=-1)


def swa_attention(q, segs, sinks, hkv, grp, hd, chunked):
    b, l, qw = q.shape
    kw = hkv * hd
    if chunked:
        cq, n_seg = CHUNK, WINDOW // CHUNK + 1
        k, v = segs
        def seg_spec(s):
            return pl.BlockSpec((None, cq, kw), lambda i, n: (i, jnp.maximum(n - (n_seg - 1) + s, 0), 0))
        k_specs = [seg_spec(s) for s in range(n_seg)]
        k_args, v_args = [k] * n_seg, [v] * n_seg
    else:
        cq, n_seg = l, len(segs)
        k_specs = [pl.BlockSpec((None, kk.shape[1], kw), lambda i, n: (i, 0, 0)) for kk, _ in segs]
        k_args, v_args = [kk for kk, _ in segs], [vv for _, vv in segs]
    vmem = 2 * (cq * qw * 4 + cq * qw * 2 + 2 * sum(a.shape[1] if not chunked else cq for a in k_args) * kw * 4)
    vmem += 16 * grp * cq * 256 * 4
    return pl.pallas_call(
        functools.partial(_swa_body, n_seg=n_seg, cq=cq, hkv=hkv, grp=grp, hd=hd, chunked=chunked),
        grid=(b, l // cq),
        in_specs=[pl.BlockSpec(memory_space=pltpu.SMEM),
                  pl.BlockSpec((None, cq, qw), lambda i, n: (i, n, 0))] + k_specs + k_specs,
        out_specs=pl.BlockSpec((None, cq, qw), lambda i, n: (i, n, 0)),
        out_shape=jax.ShapeDtypeStruct((b, l, qw), BF16),
        compiler_params=_params(("parallel", "parallel"), vmem), name="swa_attention")(
            sinks, q, *k_args, *v_args)


def _ab_layer(x, conv_prev, s0, past, w_in, conv_w, a_log, dt_bias, onorm_g, f_bias, w_out, g_mix, dims):
    b, l, d = x.shape
    h_a, dk, h_b, hd_b = dims
    a_qk, a_v, b_w = h_a * dk, h_a * dk, h_b * hd_b
    conv_dim = 2 * a_qk + a_v
    o1 = conv_dim + a_v
    o2 = o1 + 2 * h_a
    small = jnp.concatenate([w_in[:, o1:o2], w_in[:, o2 + 3 * b_w:],
                             jnp.zeros((d, LANE - 2 * h_a - h_b), F32)], axis=1)
    w_pack = jnp.concatenate([w_in[:, :o1], w_in[:, o2:o2 + 3 * b_w], small], axis=1).astype(BF16)
    zeros4 = jnp.zeros((h_a,), F32)
    gp = jnp.zeros((SUBLANE, LANE), F32)
    gp = gp.at[0, :3 * h_a].set(jnp.concatenate([zeros4, dt_bias, f_bias]))
    gp = gp.at[1, h_a:2 * h_a].set(a_log)
    f32o, b16 = (F32, 1.0), (BF16, 1.0)
    groups = ((conv_dim, (f32o,)), (a_v, (f32o,)), (b_w, ((BF16, hd_b ** -0.5),)),
              (b_w, (f32o, b16)), (b_w, (f32o, b16)), (LANE, (f32o,)))
    qkv, z, qb, kb, kb16, vb, vb16, gates = norm_matmul(x.reshape(b * l, d), g_mix, w_pack, groups, gate_params=gp)
    r3 = lambda a: a.reshape(b, l, a.shape[-1])
    qkv3, gates3 = r3(qkv), r3(gates)
    oa, s_new = gated_delta_net(qkv3, r3(z), gates3, conv_prev, conv_w, s0, onorm_g, h_a, dk)
    conv_new = qkv3[:, l - (CONV_W - 1):, :]
    logf = gates3[:, :, 2 * h_a:2 * h_a + h_b]
    lf_rows = jnp.pad(logf.transpose(0, 2, 1).reshape(b * h_b, l), ((0, 0), (0, -l % LANE)))
    f_new = cumsum_rows(lf_rows)[:, :l].reshape(b, h_b, l)
    fox_past = None
    if past is not None:
        pk, pv, plf = past
        p_len = pk.shape[1]
        f_past = cumsum_rows(plf.transpose(0, 2, 1).reshape(b * h_b, p_len), minus_total=True)
        fox_past = (pk.reshape(b, p_len, b_w), pv.reshape(b, p_len, b_w), f_past.reshape(b, h_b, p_len))
    ob = fox_attention(r3(qb), r3(kb16), r3(vb16), f_new, h_b, past=fox_past)
    w_out16 = w_out.astype(BF16)
    x_new = matmul_residual(x.reshape(b * l, d), [oa.reshape(b * l, a_v), ob.reshape(b * l, b_w)],
                            [w_out16[:a_v], w_out16[a_v:]])
    return (x_new.reshape(b, l, d), s_new, conv_new, kb.reshape(b, l, h_b, hd_b), vb.reshape(b, l, h_b, hd_b), logf)


def _c_layer(x, past, w_in, sinks, w_out, g_mix, dims):
    b, l, d = x.shape
    hq, hkv, hd = dims
    grp = hq // hkv
    f32o = (F32, 1.0)
    q, k, v = norm_matmul(x.reshape(b * l, d), g_mix, w_in.astype(BF16),
                          ((hq * hd, (f32o,)), (hkv * hd, (f32o,)), (hkv * hd, (f32o,))))
    q3, k3, v3 = q.reshape(b, l, hq * hd), k.reshape(b, l, hkv * hd), v.reshape(b, l, hkv * hd)
    if past is None:
        o = swa_attention(q3, (k3, v3), sinks, hkv, grp, hd, chunked=True)
        new_k, new_v = k3[:, l - WINDOW:], v3[:, l - WINDOW:]
    else:
        pk, pv = past
        w_len = pk.shape[1]
        o = swa_attention(q3, ((pk.reshape(b, w_len, hkv * hd), pv.reshape(b, w_len, hkv * hd)), (k3, v3)),
                          sinks, hkv, grp, hd, chunked=False)
        new_k, new_v = k3, v3
    x_new = matmul_residual(x.reshape(b * l, d), [o.reshape(b * l, hq * hd)], [w_out.astype(BF16)])
    return x_new.reshape(b, l, d), new_k.reshape(b, -1, hkv, hd), new_v.reshape(b, -1, hkv, hd)


def kernel(x_prompt, mem_prompt, x_sample, state_gdn, state_gdn_conv, cache_fox_k, cache_fox_v, cache_fox_logf, cache_swa_k, cache_swa_v, cache_mem_k, cache_mem_v, g_mix, w_in_ab, conv_w_a, a_log_a, dt_bias_a, onorm_g_a, f_bias_b, w_out_ab, w_in_c, sinks_c, w_out_c, g_mem, w_xkv, g_cross, w_xq, w_xo, g_mlp, w_up, w_down, g_final):
    depth = g_mix.shape[0]
    bp, lp, d = x_prompt.shape
    bs, ls, _ = x_sample.shape
    h_a, dk = state_gdn.shape[2], state_gdn.shape[3]
    h_b, hd_b = cache_fox_k.shape[3], cache_fox_k.shape[4]
    hkv, hd_c = cache_swa_k.shape[3], cache_swa_k.shape[4]
    hq = sinks_c.shape[1]
    m_len, h_x, hd_x = cache_mem_k.shape[2], cache_mem_k.shape[3], cache_mem_k.shape[4]
    xw = h_x * hd_x
    conv_dim = state_gdn_conv.shape[-1]

    xp, xs = x_prompt, x_sample
    p_out = {k: [] for k in ("s", "c", "fk", "fv", "fl", "wk", "wv", "mk", "mv")}
    s_out = {k: [] for k in ("s", "c", "fk", "fv", "fl", "wk", "wv")}
    for layer in range(depth):
        i = layer // 2
        if layer % 2 == 0:
            wts = (w_in_ab[i], conv_w_a[i], a_log_a[i], dt_bias_a[i], onorm_g_a[i], f_bias_b[i], w_out_ab[i],
                   g_mix[layer], (h_a, dk, h_b, hd_b))
            xp, st, cv, fk, fv, fl = _ab_layer(xp, jnp.zeros((bp, CONV_W - 1, conv_dim), F32),
                                               jnp.zeros((bp, h_a, dk, dk), F32), None, *wts)
            for key, val in zip(("s", "c", "fk", "fv", "fl"), (st, cv, fk, fv, fl)):
                p_out[key].append(val)
            xs, st, cv, fk, fv, fl = _ab_layer(xs, state_gdn_conv[i], state_gdn[i],
                                               (cache_fox_k[i], cache_fox_v[i], cache_fox_logf[i]), *wts)
            for key, val in zip(("s", "c", "fk", "fv", "fl"), (st, cv, fk, fv, fl)):
                s_out[key].append(val)
        else:
            wts = (w_in_c[i], sinks_c[i], w_out_c[i], g_mix[layer], (hq, hkv, hd_c))
            xp, wk, wv = _c_layer(xp, None, *wts)
            p_out["wk"].append(wk); p_out["wv"].append(wv)
            xs, wk, wv = _c_layer(xs, (cache_swa_k[i], cache_swa_v[i]), *wts)
            s_out["wk"].append(wk); s_out["wv"].append(wv)
        f32o = (F32, 1.0)
        mk, mv = norm_matmul(mem_prompt.reshape(bp * m_len, d), g_mem[layer], w_xkv[layer].astype(BF16),
                             ((xw, (f32o,)), (xw, (f32o,))))
        p_out["mk"].append(mk.reshape(bp, m_len, h_x, hd_x)); p_out["mv"].append(mv.reshape(bp, m_len, h_x, hd_x))
        wq16, wo16 = w_xq[layer].astype(BF16), w_xo[layer].astype(BF16)
        xp = cross_attend(xp, g_cross[layer], wq16, mk.reshape(bp, m_len, xw), mv.reshape(bp, m_len, xw), wo16, h_x)
        xs = cross_attend(xs, g_cross[layer], wq16, cache_mem_k[layer].reshape(bs, m_len, xw),
                          cache_mem_v[layer].reshape(bs, m_len, xw), wo16, h_x)
        wu16, wd16 = w_up[layer].astype(BF16), w_down[layer].astype(BF16)
        gf = g_final if layer == depth - 1 else None
        xp = mlp(xp.reshape(bp * lp, d), g_mlp[layer], wu16, wd16, gf).reshape(bp, lp, d)
        xs = mlp(xs.reshape(bs * ls, d), g_mlp[layer], wu16, wd16, gf).reshape(bs, ls, d)
    st = jnp.stack
    return (xp, xs,
            st(p_out["s"]), st(p_out["c"]), st(p_out["fk"]), st(p_out["fv"]), st(p_out["fl"]),
            st(p_out["wk"]), st(p_out["wv"]), st(p_out["mk"]), st(p_out["mv"]),
            st(s_out["s"]), st(s_out["c"]), st(s_out["fk"]), st(s_out["fv"]), st(s_out["fl"]),
            st(s_out["wk"]), st(s_out["wv"]))
```

```python
import functools
import math

import jax
import jax.numpy as jnp
from jax import lax
from jax.experimental import pallas as pl
from jax.experimental.pallas import tpu as pltpu

F32 = jnp.float32
BF16 = jnp.bfloat16
HI = lax.Precision.HIGHEST
EPS = 1e-6
NEG = -1e30

CHUNK = 64
CONV_W = 4
WINDOW = 128
LANE = 128
SUBLANE = 8
VMEM_BYTES = 64 * 1024 * 1024


def _params(semantics, vmem_bytes):
    limit = int(min(max(vmem_bytes * 1.25, 16 * 1024 * 1024), VMEM_BYTES - 8 * 1024 * 1024))
    return pltpu.CompilerParams(dimension_semantics=semantics, vmem_limit_bytes=limit)


def _dot(a, b):
    return jnp.dot(a, b, preferred_element_type=F32)


def _dot_nt(a, b, precision=None):
    return lax.dot_general(a, b, (((1,), (1,)), ((), ())), preferred_element_type=F32, precision=precision)


def _dot_tn(a, b):
    return lax.dot_general(a, b, (((0,), (0,)), ((), ())), preferred_element_type=F32)


def _rms(x, g):
    return x * lax.rsqrt(jnp.mean(x * x, axis=-1, keepdims=True) + EPS) * g


def _sigmoid(x):
    return 1.0 / (1.0 + jnp.exp(-x))


def _softplus(x):
    return jnp.maximum(x, 0.0) + jnp.log(1.0 + jnp.exp(-jnp.abs(x)))


def _token_tile(t, cap):
    tm = min(t, cap)
    assert t % tm == 0
    return tm


def _norm_matmul_body(*refs, groups, gated, col_chunk):
    x_ref, g_ref, w_ref = refs[:3]
    refs = refs[3:]
    if gated:
        gp_ref, refs = refs[0], refs[1:]
    x = x_ref[...]
    yb = _rms(x, g_ref[...]).astype(BF16)
    off, k = 0, 0
    for gi, (n, outs) in enumerate(groups):
        for c in range(0, n, col_chunk):
            cw = min(col_chunk, n - c)
            r = _dot(yb, w_ref[:, off + c:off + c + cw])
            if gated and gi == len(groups) - 1:
                rb = r + gp_ref[0:1, :]
                lane = lax.broadcasted_iota(jnp.int32, rb.shape, 1)
                e = jnp.exp(-jnp.abs(rb))
                l1p = jnp.log(1.0 + e)
                sig = _sigmoid(rb)
                gdec = -jnp.exp(gp_ref[1:2, :]) * (jnp.maximum(rb, 0.0) + l1p)
                logsig = jnp.minimum(rb, 0.0) - l1p
                r = jnp.where(lane < 4, sig, jnp.where(lane < 8, gdec, jnp.where(lane < 12, logsig, 0.0)))
            for j, (dtype, scale) in enumerate(outs):
                o_ref = refs[k + j]
                rr = r if scale == 1.0 else r * scale
                o_ref[:, c:c + cw] = rr.astype(dtype)
        off += n
        k += len(outs)


def norm_matmul(x, g, w, groups, gate_params=None, tile_cap=512):
    t, d = x.shape
    tm = _token_tile(t, tile_cap)
    gated = gate_params is not None
    in_specs = [pl.BlockSpec((tm, d), lambda i: (i, 0)),
                pl.BlockSpec((1, d), lambda i: (0, 0)),
                pl.BlockSpec(w.shape, lambda i: (0, 0))]
    args = [x, g.reshape(1, d), w]
    if gated:
        in_specs.append(pl.BlockSpec(gate_params.shape, lambda i: (0, 0)))
        args.append(gate_params)
    out_shape, out_specs, out_bytes = [], [], 0
    for n, outs in groups:
        for dtype, _ in outs:
            out_shape.append(jax.ShapeDtypeStruct((t, n), dtype))
            out_specs.append(pl.BlockSpec((tm, n), lambda i: (i, 0)))
            out_bytes += tm * n * jnp.dtype(dtype).itemsize
    vmem = 2 * (tm * d * 4 + w.size * 2 + out_bytes) + tm * d * 8
    return pl.pallas_call(
        functools.partial(_norm_matmul_body, groups=groups, gated=gated, col_chunk=512),
        grid=(t // tm,), in_specs=in_specs, out_specs=out_specs, out_shape=out_shape,
        compiler_params=_params(("parallel",), vmem), name="norm_matmul")(*args)


def _post_body(*refs, n_in, heads, hd, ff_chunk, final_norm, col_chunk):
    x_ref = refs[0]
    a_refs, w_refs = refs[1:1 + n_in], refs[1 + n_in:1 + 2 * n_in]
    gc_ref, wq_ref, mk_ref, mv_ref, wo_ref, gm_ref, wu_ref, wd_ref = refs[1 + 2 * n_in:9 + 2 * n_in]
    gf_ref = refs[9 + 2 * n_in] if final_norm else None
    o_ref = refs[-1]
    d = x_ref.shape[-1]
    a_vals = [a[...].astype(BF16) for a in a_refs]
    x1 = jnp.concatenate(
        [x_ref[:, c:c + col_chunk] + sum(_dot(a, w[:, c:c + col_chunk]) for a, w in zip(a_vals, w_refs))
         for c in range(0, d, col_chunk)], axis=-1)
    hb = _rms(x1, gc_ref[...]).astype(BF16)
    q = _dot(hb, wq_ref[...]) * (hd ** -0.5)
    outs = []
    for h in range(heads):
        sl = slice(h * hd, (h + 1) * hd)
        s = _dot_nt(q[:, sl].astype(BF16), mk_ref[:, sl].astype(BF16))
        m = jnp.max(s, axis=-1, keepdims=True)
        p = jnp.exp(s - m)
        l = jnp.sum(p, axis=-1, keepdims=True)
        outs.append((_dot(p.astype(BF16), mv_ref[:, sl].astype(BF16)) / l).astype(BF16))
    x2 = x1 + _dot(jnp.concatenate(outs, axis=-1), wo_ref[...])
    hb = _rms(x2, gm_ref[...]).astype(BF16)
    acc = x2
    for c in range(0, wu_ref.shape[1], ff_chunk):
        u = jnp.maximum(_dot(hb, wu_ref[:, c:c + ff_chunk]), 0.0)
        acc = acc + _dot((u * u).astype(BF16), wd_ref[c:c + ff_chunk, :])
    if final_norm:
        acc = _rms(acc, gf_ref[...])
    o_ref[...] = acc


def post_mixer(x, a_list, w_list, g_cross, wq, mk, mv, wo, g_mlp, w_up, w_down, heads, g_final=None, tile_cap=512):
    b, l, d = x.shape
    m, hw = mk.shape[1], mk.shape[2]
    tm = _token_tile(l, tile_cap)
    n_in = len(a_list)
    final_norm = g_final is not None
    once = pl.Buffered(1)

    def const(shape):
        return pl.BlockSpec(shape, lambda i, j: (0,) * len(shape), pipeline_mode=once)

    in_specs = [pl.BlockSpec((None, tm, d), lambda i, j: (i, j, 0))]
    in_specs += [pl.BlockSpec((None, tm, a.shape[2]), lambda i, j: (i, j, 0)) for a in a_list]
    in_specs += [const(w.shape) for w in w_list]
    in_specs += [const((1, d)), const(wq.shape),
                 pl.BlockSpec((None, m, hw), lambda i, j: (i, 0, 0)),
                 pl.BlockSpec((None, m, hw), lambda i, j: (i, 0, 0)),
                 const(wo.shape), const((1, d)), const(w_up.shape), const(w_down.shape)]
    args = [x, *a_list, *w_list, g_cross.reshape(1, d), wq, mk, mv, wo, g_mlp.reshape(1, d), w_up, w_down]
    if final_norm:
        in_specs.append(const((1, d)))
        args.append(g_final.reshape(1, d))
    weights = sum(w.size for w in w_list) + wq.size + wo.size + w_up.size + w_down.size
    vmem = (weights * 2 + 2 * (2 * tm * d * 4 + 2 * m * hw * 4 + sum(tm * a.shape[2] * a.dtype.itemsize for a in a_list))
            + tm * (d * 16 + hw * 8 + m * 8 + 512 * 8))
    return pl.pallas_call(
        functools.partial(_post_body, n_in=n_in, heads=heads, hd=hw // heads, ff_chunk=512,
                          final_norm=final_norm, col_chunk=512),
        grid=(b, l // tm), in_specs=in_specs,
        out_specs=pl.BlockSpec((None, tm, d), lambda i, j: (i, j, 0)),
        out_shape=jax.ShapeDtypeStruct((b, l, d), F32),
        compiler_params=_params(("parallel", "parallel"), vmem), name="post_mixer")(*args)


def _cumsum_body(x_ref, o_ref, *, minus_total):
    r, l = x_ref.shape
    row = lax.broadcasted_iota(jnp.int32, (LANE, LANE), 0)
    col = lax.broadcasted_iota(jnp.int32, (LANE, LANE), 1)
    upper = (row <= col).astype(F32)
    carry = jnp.zeros((r, 1), F32)
    for c in range(0, l, LANE):
        y = jnp.dot(x_ref[:, c:c + LANE], upper, preferred_element_type=F32, precision=HI) + carry
        o_ref[:, c:c + LANE] = y
        carry = y[:, LANE - 1:LANE]
    if minus_total:
        o_ref[...] = o_ref[...] - carry


def cumsum_rows(x, minus_total=False):
    r, l = x.shape
    assert l % LANE == 0 and r % SUBLANE == 0
    return pl.pallas_call(
        functools.partial(_cumsum_body, minus_total=minus_total),
        grid=(1,), in_specs=[pl.BlockSpec((r, l), lambda i: (0, 0))],
        out_specs=pl.BlockSpec((r, l), lambda i: (0, 0)),
        out_shape=jax.ShapeDtypeStruct((r, l), F32),
        compiler_params=_params(("arbitrary",), 4 * r * l * 4), name="cumsum_rows")(x)


def _fox_body(*refs, tq, tkp, n_past, nsub):
    if n_past:
        q_ref, k_ref, v_ref, fk_ref, kp_ref, vp_ref, fp_ref, o_ref = refs
    else:
        q_ref, k_ref, v_ref, fk_ref, o_ref = refs
    i = pl.program_id(2)
    sub = tq // nsub
    hd = q_ref.shape[-1]
    qs = [q_ref[c * sub:(c + 1) * sub, :] for c in range(nsub)]

    def update(carry, ss, vbs):
        ms, ls, accs = carry[0::3], carry[1::3], carry[2::3]
        m_new = [jnp.maximum(m, jnp.max(s, axis=-1, keepdims=True)) for m, s in zip(ms, ss)]
        alpha = [jnp.exp(m - mn) for m, mn in zip(ms, m_new)]
        ps = [jnp.exp(s - mn) for s, mn in zip(ss, m_new)]
        ls = [a * l + jnp.sum(p, axis=-1, keepdims=True) for a, l, p in zip(alpha, ls, ps)]
        pv = [_dot(p.astype(BF16), vb) for p, vb in zip(ps, vbs)]
        accs = [a * acc + x for a, acc, x in zip(alpha, accs, pv)]
        out = []
        for c in range(nsub):
            out += [m_new[c], ls[c], accs[c]]
        return tuple(out)

    carry = (jnp.full((sub, 1), NEG, F32), jnp.zeros((sub, 1), F32), jnp.zeros((sub, hd), F32)) * nsub

    if n_past:
        def past_step(j, carry):
            start = pl.multiple_of(j * tkp, tkp)
            kb = kp_ref[pl.ds(start, tkp), :].astype(BF16)
            vb = vp_ref[pl.ds(start, tkp), :].astype(BF16)
            fk = fp_ref[pl.ds(j, 1), :]
            return update(carry, [_dot_nt(q, kb) - fk for q in qs], [vb] * nsub)
        carry = lax.fori_loop(0, n_past, past_step, carry)

    def self_step(j, carry):
        start = pl.multiple_of(j * tq, tq)
        kb = k_ref[pl.ds(start, tq), :]
        vb = v_ref[pl.ds(start, tq), :]
        fk = fk_ref[pl.ds(j, 1), :][:, :tq]
        return update(carry, [_dot_nt(q, kb) - fk for q in qs], [vb] * nsub)
    carry = lax.fori_loop(0, i, self_step, carry)

    start = pl.multiple_of(i * tq, tq)
    fk = fk_ref[pl.ds(i, 1), :]
    ss, vbs = [], []
    for c in range(nsub):
        w = (c + 1) * sub
        s = _dot_nt(qs[c], k_ref[pl.ds(start, w), :]) - fk[:, :w]
        row = lax.broadcasted_iota(jnp.int32, (sub, w), 0) + c * sub
        col = lax.broadcasted_iota(jnp.int32, (sub, w), 1)
        ss.append(jnp.where(col <= row, s, NEG))
        vbs.append(v_ref[pl.ds(start, w), :])
    carry = update(carry, ss, vbs)
    for c in range(nsub):
        o_ref[c * sub:(c + 1) * sub, :] = (carry[3 * c + 2] / carry[3 * c + 1]).astype(o_ref.dtype)


def fox_attention(q, k, v, f_new, heads, past=None, tq_cap=512, sub_cap=256, tkp=256):
    b, l, hw = q.shape
    hd = hw // heads
    tq = min(l, tq_cap)
    nq = l // tq
    nsub = max(1, tq // sub_cap)
    wq = -(-tq // LANE) * LANE
    if wq != tq:
        assert nq == 1
        f_new = jnp.pad(f_new, ((0, 0), (0, 0), (0, wq - tq)))
    in_specs = [pl.BlockSpec((None, tq, hd), lambda bi, h, i: (bi, i, h)),
                pl.BlockSpec((None, l, hd), lambda bi, h, i: (bi, 0, h)),
                pl.BlockSpec((None, l, hd), lambda bi, h, i: (bi, 0, h)),
                pl.BlockSpec((None, None, nq, wq), lambda bi, h, i: (bi, h, 0, 0))]
    args = [q, k, v, f_new.reshape(b, heads, nq, wq)]
    n_past = 0
    vmem = 2 * (2 * tq * hd * 2 + 2 * l * hd * 2) + 8 * tq * max(tq, tkp) * 4
    if past is not None:
        kp, vp, fp = past
        p_len = kp.shape[1]
        n_past = p_len // tkp
        in_specs += [pl.BlockSpec((None, p_len, hd), lambda bi, h, i: (bi, 0, h)),
                     pl.BlockSpec((None, p_len, hd), lambda bi, h, i: (bi, 0, h)),
                     pl.BlockSpec((None, None, n_past, tkp), lambda bi, h, i: (bi, h, 0, 0))]
        args += [kp, vp, fp.reshape(b, heads, n_past, tkp)]
        vmem += 2 * 2 * p_len * hd * 4
    return pl.pallas_call(
        functools.partial(_fox_body, tq=tq, tkp=tkp, n_past=n_past, nsub=nsub),
        grid=(b, heads, nq), in_specs=in_specs,
        out_specs=pl.BlockSpec((None, tq, hd), lambda bi, h, i: (bi, i, h)),
        out_shape=jax.ShapeDtypeStruct((b, l, hw), BF16),
        compiler_params=_params(("parallel", "parallel", "arbitrary"), vmem), name="fox_attention")(*args)


def _gdn_body(qkv_ref, z_ref, gates_ref, cprev_ref, cw_ref, s0_ref, og_ref, o_ref, st_ref, ybuf_ref,
              *, c, nc, heads, dk):
    n = pl.program_id(1)
    pad = SUBLANE
    r = c * nc
    hw = heads * dk

    @pl.when(n == 0)
    def _():
        ybuf_ref[0:pad, :] = cprev_ref[...]
        st_ref[...] = s0_ref[...]

    ybuf_ref[pad:pad + r, :] = qkv_ref[...]

    row = lax.broadcasted_iota(jnp.int32, (c, c), 0)
    col = lax.broadcasted_iota(jnp.int32, (c, c), 1)
    incl = row >= col
    strict = row > col
    tril = incl.astype(BF16)
    n_double = int(math.log2(c)) - 1
    og = og_ref[...]

    chains = []
    for ci in range(nc):
        r0 = ci * c
        conv = qkv_ref[r0:r0 + c, :] * cw_ref[CONV_W - 1:CONV_W, :]
        for k in range(1, CONV_W):
            conv = conv + ybuf_ref[pad + r0 - k:pad + r0 - k + c, :] * cw_ref[CONV_W - 1 - k:CONV_W - k, :]
        act = conv * _sigmoid(conv)
        gs = gates_ref[r0:r0 + c, :]
        g1 = gs.astype(BF16)
        res = gs - g1.astype(F32)
        g2 = res.astype(BF16)
        g3 = (res - g2.astype(F32)).astype(BF16)
        cs3 = _dot(tril, jnp.concatenate([g1, g2, g3], axis=-1))
        cs = cs3[:, :LANE] + cs3[:, LANE:2 * LANE] + cs3[:, 2 * LANE:]
        cs_t = jnp.concatenate([cs, jnp.zeros((LANE - c, LANE), F32)], axis=0).T
        for h in range(heads):
            qh = act[:, h * dk:(h + 1) * dk]
            kh = act[:, hw + h * dk:hw + (h + 1) * dk]
            vh = act[:, 2 * hw + h * dk:2 * hw + (h + 1) * dk]
            qh = qh * lax.rsqrt(jnp.sum(qh * qh, axis=-1, keepdims=True) + EPS) * (dk ** -0.5)
            kh = kh * lax.rsqrt(jnp.sum(kh * kh, axis=-1, keepdims=True) + EPS)
            beta = gs[:, h:h + 1]
            gc = cs[:, heads + h:heads + h + 1]
            gc_row = cs_t[heads + h:heads + h + 1, :c]
            kb = kh * beta
            egc = jnp.exp(gc)
            g_last = gc[c - 1:c, :]
            chains.append(dict(
                decay=jnp.exp(jnp.where(incl, gc - gc_row, NEG)),
                kq=jnp.concatenate([kb, qh], axis=0).astype(BF16), khb=kh.astype(BF16),
                rhs=jnp.concatenate([vh * beta, kb * egc], axis=-1), qg=qh * egc,
                kd=(kh * jnp.exp(g_last - gc)).astype(BF16), ds=jnp.exp(g_last)))

    for ch in chains:
        ch["a1"] = _dot_nt(ch["kq"], ch["khb"])
    for ch in chains:
        lmat = jnp.where(strict, ch["a1"][:c] * ch["decay"], 0.0)
        ch["attn"] = jnp.where(incl, ch["a1"][c:] * ch["decay"], 0.0).astype(BF16)
        ch["toff"] = -lmat
        ch["pw"] = lmat
    for _ in range(n_double):
        for ch in chains:
            pwb = ch["pw"].astype(BF16)
            ch["pw"] = _dot(pwb, pwb)
        for ch in chains:
            ch["tp"] = _dot(ch["toff"].astype(BF16), ch["pw"].astype(BF16))
        for ch in chains:
            ch["toff"] = ch["toff"] + ch["pw"] + ch["tp"]
    for ch in chains:
        ch["uw"] = _dot(ch["toff"].astype(BF16), ch["rhs"].astype(BF16))
    for ch in chains:
        uw = ch["rhs"] + ch["uw"]
        ch["u"] = uw[:, :dk]
        ch["wq"] = jnp.concatenate([uw[:, dk:], ch["qg"]], axis=0).astype(BF16)

    states = [st_ref[h] for h in range(heads)]
    for ci in range(nc):
        group = chains[ci * heads:(ci + 1) * heads]
        t1 = [_dot(ch["wq"], s.astype(BF16)) for ch, s in zip(group, states)]
        vnb = [(ch["u"] - t[:c]).astype(BF16) for ch, t in zip(group, t1)]
        o2 = [_dot(ch["attn"], v) for ch, v in zip(group, vnb)]
        sd = [_dot_tn(ch["kd"], v) for ch, v in zip(group, vnb)]
        states = [s * ch["ds"] + d for ch, s, d in zip(group, states, sd)]
        for h in range(heads):
            o = t1[h][c:] + o2[h]
            zh = z_ref[ci * c:(ci + 1) * c, h * dk:(h + 1) * dk]
            o_ref[ci * c:(ci + 1) * c, h * dk:(h + 1) * dk] = (_rms(o, og) * (zh * _sigmoid(zh))).astype(o_ref.dtype)
    for h in range(heads):
        st_ref[h] = states[h]

    ybuf_ref[0:pad, :] = ybuf_ref[r:r + pad, :]


def gated_delta_net(qkv, z, gates, conv_prev, conv_w, s0, onorm_g, heads, dk, chunks_per_step=4):
    b, l, cd = qkv.shape
    c = min(CHUNK, l)
    nc = min(chunks_per_step, l // c)
    r = c * nc
    hw = heads * dk
    cprev = jnp.pad(conv_prev, ((0, 0), (SUBLANE - (CONV_W - 1), 0), (0, 0)))
    cw = jnp.pad(conv_w, ((0, SUBLANE - CONV_W), (0, 0)))
    vmem = 2 * (r * cd * 4 + r * hw * 4 + r * LANE * 4 + 2 * SUBLANE * cd * 4 + 2 * heads * dk * dk * 4
                + r * hw * 2) + (r + SUBLANE) * cd * 4 * 4
    return pl.pallas_call(
        functools.partial(_gdn_body, c=c, nc=nc, heads=heads, dk=dk),
        grid=(b, l // r),
        in_specs=[pl.BlockSpec((None, r, cd), lambda i, n: (i, n, 0)),
                  pl.BlockSpec((None, r, hw), lambda i, n: (i, n, 0)),
                  pl.BlockSpec((None, r, LANE), lambda i, n: (i, n, 0)),
                  pl.BlockSpec((None, SUBLANE, cd), lambda i, n: (i, 0, 0)),
                  pl.BlockSpec((SUBLANE, cd), lambda i, n: (0, 0)),
                  pl.BlockSpec((None, heads, dk, dk), lambda i, n: (i, 0, 0, 0)),
                  pl.BlockSpec((1, dk), lambda i, n: (0, 0))],
        out_specs=[pl.BlockSpec((None, r, hw), lambda i, n: (i, n, 0)),
                   pl.BlockSpec((None, heads, dk, dk), lambda i, n: (i, 0, 0, 0))],
        out_shape=[jax.ShapeDtypeStruct((b, l, hw), BF16),
                   jax.ShapeDtypeStruct((b, heads, dk, dk), F32)],
        scratch_shapes=[pltpu.VMEM((r + SUBLANE, cd), F32)],
        compiler_params=_params(("parallel", "arbitrary"), vmem), name="gated_delta_net")(
            qkv, z, gates, cprev, cw, s0, onorm_g.reshape(1, dk))


def _swa_body(*refs, n_seg, cq, nchunk, win, hkv, grp, hd, chunked):
    sink_ref, q_ref = refs[0], refs[1]
    k_refs, v_refs, o_ref = refs[2:2 + n_seg], refs[2 + n_seg:2 + 2 * n_seg], refs[2 + 2 * n_seg]
    kk = jnp.concatenate([r[...] for r in k_refs], axis=0).astype(BF16)
    vv = jnp.concatenate([r[...] for r in v_refs], axis=0).astype(BF16)
    row = lax.broadcasted_iota(jnp.int32, (cq, win), 0)
    col = lax.broadcasted_iota(jnp.int32, (cq, win), 1)
    dist = jnp.abs(row - col + WINDOW).astype(F32)
    hq = hkv * grp
    k_heads = [kk[:, j * hd:(j + 1) * hd] for j in range(hkv)]
    v_heads = [vv[:, j * hd:(j + 1) * hd] for j in range(hkv)]
    chains = []
    for c in range(nchunk):
        dist_c = dist
        if chunked and c * cq < WINDOW:
            first_valid = WINDOW - (pl.program_id(1) * nchunk + c) * cq
            dist_c = jnp.where(col >= first_valid, dist, -NEG)
        for j in range(hkv):
            q_stack = jnp.concatenate(
                [q_ref[c * cq:(c + 1) * cq, (j * grp + g) * hd:(j * grp + g + 1) * hd] for g in range(grp)], axis=0)
            chains.append(dict(c=c, j=j, dist=dist_c, q=q_stack,
                               k=k_heads[j][c * cq:c * cq + win, :], v=v_heads[j][c * cq:c * cq + win, :]))

    def scores(ch):
        ch["s"] = _dot_nt(ch["q"], ch["k"])

    def softmax_pv(ch):
        ps, ch["l"] = [], []
        for g in range(grp):
            h = ch["j"] * grp + g
            slope = 2.0 ** (-8.0 * (h + 1) / hq)
            sg = ch["s"][g * cq:(g + 1) * cq, :] - slope * ch["dist"]
            sink = sink_ref[h]
            m = jnp.maximum(jnp.max(sg, axis=-1, keepdims=True), sink)
            p = jnp.exp(sg - m)
            ch["l"].append(jnp.sum(p, axis=-1, keepdims=True) + jnp.exp(sink - m))
            ps.append(p.astype(BF16))
        ch["pv"] = _dot(jnp.concatenate(ps, axis=0), ch["v"])

    def emit(ch):
        c, j = ch["c"], ch["j"]
        outs = [(ch["pv"][g * cq:(g + 1) * cq, :] / ch["l"][g]).astype(o_ref.dtype) for g in range(grp)]
        o_ref[c * cq:(c + 1) * cq, j * grp * hd:(j + 1) * grp * hd] = jnp.concatenate(outs, axis=-1)

    scores(chains[0])
    for i, ch in enumerate(chains):
        if i + 1 < len(chains):
            scores(chains[i + 1])
        softmax_pv(ch)
        if i > 0:
            emit(chains[i - 1])
    emit(chains[-1])


def swa_attention(q, segs, sinks, hkv, grp, hd, chunked, chunks_per_step=4):
    b, l, qw = q.shape
    kw = hkv * hd
    if chunked:
        cq, nchunk, n_seg, win = CHUNK, chunks_per_step, 2, WINDOW + CHUNK
        rows = cq * nchunk
        assert l % rows == 0 and rows % WINDOW == 0
        k, v = segs
        k_specs = [pl.BlockSpec((None, WINDOW, kw), lambda i, n: (i, jnp.maximum(n * (rows // WINDOW) - 1, 0), 0)),
                   pl.BlockSpec((None, rows, kw), lambda i, n: (i, n, 0))]
        k_args, v_args = [k] * n_seg, [v] * n_seg
        key_rows = WINDOW + rows
    else:
        cq, nchunk, n_seg = l, 1, len(segs)
        rows = l
        k_specs = [pl.BlockSpec((None, kk.shape[1], kw), lambda i, n: (i, 0, 0)) for kk, _ in segs]
        k_args, v_args = [kk for kk, _ in segs], [vv for _, vv in segs]
        key_rows = win = sum(kk.shape[1] for kk in k_args)
    vmem = 2 * (2 * rows * qw * 2 + 2 * key_rows * kw * 4) + 3 * nchunk * hkv * grp * cq * 256 * 4
    return pl.pallas_call(
        functools.partial(_swa_body, n_seg=n_seg, cq=cq, nchunk=nchunk, win=win, hkv=hkv, grp=grp, hd=hd,
                          chunked=chunked),
        grid=(b, l // rows),
        in_specs=[pl.BlockSpec(memory_space=pltpu.SMEM),
                  pl.BlockSpec((None, rows, qw), lambda i, n: (i, n, 0))] + k_specs + k_specs,
        out_specs=pl.BlockSpec((None, rows, qw), lambda i, n: (i, n, 0)),
        out_shape=jax.ShapeDtypeStruct((b, l, qw), BF16),
        compiler_params=_params(("parallel", "parallel"), vmem), name="swa_attention")(
            sinks, q, *k_args, *v_args)


def _ab_layer(x, conv_prev, s0, past, w_in, conv_w, a_log, dt_bias, onorm_g, f_bias, w_out, g_mix, dims):
    b, l, d = x.shape
    h_a, dk, h_b, hd_b = dims
    a_qk, a_v, b_w = h_a * dk, h_a * dk, h_b * hd_b
    conv_dim = 2 * a_qk + a_v
    o1 = conv_dim + a_v
    o2 = o1 + 2 * h_a
    small = jnp.concatenate([w_in[:, o1:o2], w_in[:, o2 + 3 * b_w:],
                             jnp.zeros((d, LANE - 2 * h_a - h_b), F32)], axis=1)
    w_pack = jnp.concatenate([w_in[:, :o1], w_in[:, o2:o2 + 3 * b_w], small], axis=1).astype(BF16)
    zeros4 = jnp.zeros((h_a,), F32)
    gp = jnp.zeros((SUBLANE, LANE), F32)
    gp = gp.at[0, :3 * h_a].set(jnp.concatenate([zeros4, dt_bias, f_bias]))
    gp = gp.at[1, h_a:2 * h_a].set(a_log)
    f32o, b16 = (F32, 1.0), (BF16, 1.0)
    groups = ((conv_dim, (f32o,)), (a_v, (f32o,)), (b_w, ((BF16, hd_b ** -0.5),)),
              (b_w, (f32o, b16)), (b_w, (f32o, b16)), (LANE, (f32o,)))
    qkv, z, qb, kb, kb16, vb, vb16, gates = norm_matmul(x.reshape(b * l, d), g_mix, w_pack, groups, gate_params=gp)
    r3 = lambda a: a.reshape(b, l, a.shape[-1])
    qkv3, gates3 = r3(qkv), r3(gates)
    oa, s_new = gated_delta_net(qkv3, r3(z), gates3, conv_prev, conv_w, s0, onorm_g, h_a, dk)
    conv_new = qkv3[:, l - (CONV_W - 1):, :]
    logf = gates3[:, :, 2 * h_a:2 * h_a + h_b]
    lf_rows = jnp.pad(logf.transpose(0, 2, 1).reshape(b * h_b, l), ((0, 0), (0, -l % LANE)))
    f_new = cumsum_rows(lf_rows)[:, :l].reshape(b, h_b, l)
    fox_past = None
    if past is not None:
        pk, pv, plf = past
        p_len = pk.shape[1]
        f_past = cumsum_rows(plf.transpose(0, 2, 1).reshape(b * h_b, p_len), minus_total=True)
        fox_past = (pk.reshape(b, p_len, b_w), pv.reshape(b, p_len, b_w), f_past.reshape(b, h_b, p_len))
    ob = fox_attention(r3(qb), r3(kb16), r3(vb16), f_new, h_b, past=fox_past)
    w_out16 = w_out.astype(BF16)
    mixed = ([oa, ob], [w_out16[:a_v], w_out16[a_v:]])
    return (mixed, s_new, conv_new, kb.reshape(b, l, h_b, hd_b), vb.reshape(b, l, h_b, hd_b), logf)


def _c_layer(x, past, w_in, sinks, w_out, g_mix, dims):
    b, l, d = x.shape
    hq, hkv, hd = dims
    grp = hq // hkv
    f32o = (F32, 1.0)
    q, k, v = norm_matmul(x.reshape(b * l, d), g_mix, w_in.astype(BF16),
                          ((hq * hd, ((BF16, hd ** -0.5),)), (hkv * hd, (f32o,)), (hkv * hd, (f32o,))))
    q3, k3, v3 = q.reshape(b, l, hq * hd), k.reshape(b, l, hkv * hd), v.reshape(b, l, hkv * hd)
    if past is None:
        o = swa_attention(q3, (k3, v3), sinks, hkv, grp, hd, chunked=True)
        new_k, new_v = k3[:, l - WINDOW:], v3[:, l - WINDOW:]
    else:
        pk, pv = past
        w_len = pk.shape[1]
        o = swa_attention(q3, ((pk.reshape(b, w_len, hkv * hd), pv.reshape(b, w_len, hkv * hd)), (k3, v3)),
                          sinks, hkv, grp, hd, chunked=False)
        new_k, new_v = k3, v3
    return ([o], [w_out.astype(BF16)]), new_k.reshape(b, -1, hkv, hd), new_v.reshape(b, -1, hkv, hd)


def kernel(x_prompt, mem_prompt, x_sample, state_gdn, state_gdn_conv, cache_fox_k, cache_fox_v, cache_fox_logf, cache_swa_k, cache_swa_v, cache_mem_k, cache_mem_v, g_mix, w_in_ab, conv_w_a, a_log_a, dt_bias_a, onorm_g_a, f_bias_b, w_out_ab, w_in_c, sinks_c, w_out_c, g_mem, w_xkv, g_cross, w_xq, w_xo, g_mlp, w_up, w_down, g_final):
    depth = g_mix.shape[0]
    bp, lp, d = x_prompt.shape
    bs, ls, _ = x_sample.shape
    h_a, dk = state_gdn.shape[2], state_gdn.shape[3]
    h_b, hd_b = cache_fox_k.shape[3], cache_fox_k.shape[4]
    hkv, hd_c = cache_swa_k.shape[3], cache_swa_k.shape[4]
    hq = sinks_c.shape[1]
    m_len, h_x, hd_x = cache_mem_k.shape[2], cache_mem_k.shape[3], cache_mem_k.shape[4]
    xw = h_x * hd_x
    conv_dim = state_gdn_conv.shape[-1]

    xp, xs = x_prompt, x_sample
    p_out = {k: [] for k in ("s", "c", "fk", "fv", "fl", "wk", "wv", "mk", "mv")}
    s_out = {k: [] for k in ("s", "c", "fk", "fv", "fl", "wk", "wv")}
    for layer in range(depth):
        i = layer // 2
        if layer % 2 == 0:
            wts = (w_in_ab[i], conv_w_a[i], a_log_a[i], dt_bias_a[i], onorm_g_a[i], f_bias_b[i], w_out_ab[i],
                   g_mix[layer], (h_a, dk, h_b, hd_b))
            mix_p, st, cv, fk, fv, fl = _ab_layer(xp, jnp.zeros((bp, CONV_W - 1, conv_dim), F32),
                                                  jnp.zeros((bp, h_a, dk, dk), F32), None, *wts)
            for key, val in zip(("s", "c", "fk", "fv", "fl"), (st, cv, fk, fv, fl)):
                p_out[key].append(val)
            mix_s, st, cv, fk, fv, fl = _ab_layer(xs, state_gdn_conv[i], state_gdn[i],
                                                  (cache_fox_k[i], cache_fox_v[i], cache_fox_logf[i]), *wts)
            for key, val in zip(("s", "c", "fk", "fv", "fl"), (st, cv, fk, fv, fl)):
                s_out[key].append(val)
        else:
            wts = (w_in_c[i], sinks_c[i], w_out_c[i], g_mix[layer], (hq, hkv, hd_c))
            mix_p, wk, wv = _c_layer(xp, None, *wts)
            p_out["wk"].append(wk); p_out["wv"].append(wv)
            mix_s, wk, wv = _c_layer(xs, (cache_swa_k[i], cache_swa_v[i]), *wts)
            s_out["wk"].append(wk); s_out["wv"].append(wv)
        f32o = (F32, 1.0)
        mk, mv = norm_matmul(mem_prompt.reshape(bp * m_len, d), g_mem[layer], w_xkv[layer].astype(BF16),
                             ((xw, (f32o,)), (xw, (f32o,))))
        p_out["mk"].append(mk.reshape(bp, m_len, h_x, hd_x)); p_out["mv"].append(mv.reshape(bp, m_len, h_x, hd_x))
        post_w = (g_cross[layer], w_xq[layer].astype(BF16))
        mlp_w = (w_xo[layer].astype(BF16), g_mlp[layer], w_up[layer].astype(BF16), w_down[layer].astype(BF16), h_x,
                 g_final if layer == depth - 1 else None)
        xp = post_mixer(xp, *mix_p, *post_w, mk.reshape(bp, m_len, xw), mv.reshape(bp, m_len, xw), *mlp_w)
        xs = post_mixer(xs, *mix_s, *post_w, cache_mem_k[layer].reshape(bs, m_len, xw),
                        cache_mem_v[layer].reshape(bs, m_len, xw), *mlp_w)
    st = jnp.stack
    return (xp, xs,
            st(p_out["s"]), st(p_out["c"]), st(p_out["fk"]), st(p_out["fv"]), st(p_out["fl"]),
            st(p_out["wk"]), st(p_out["wv"]), st(p_out["mk"]), st(p_out["mv"]),
            st(s_out["s"]), st(s_out["c"]), st(s_out["fk"]), st(s_out["fv"]), st(s_out["fl"]),
            st(s_out["wk"]), st(s_out["wv"]))
```

```python
import functools
import math

import jax
import jax.numpy as jnp
from jax import lax
from jax.experimental import pallas as pl
from jax.experimental.pallas import tpu as pltpu

F32 = jnp.float32
BF16 = jnp.bfloat16
HI = lax.Precision.HIGHEST
EPS = 1e-6
NEG = -1e30

CHUNK = 64
CONV_W = 4
WINDOW = 128
LANE = 128
SUBLANE = 8
VMEM_BYTES = 64 * 1024 * 1024


def _params(semantics, vmem_bytes):
    limit = int(min(max(vmem_bytes * 1.25, 16 * 1024 * 1024), VMEM_BYTES - 8 * 1024 * 1024))
    return pltpu.CompilerParams(dimension_semantics=semantics, vmem_limit_bytes=limit)


def _dot(a, b):
    return jnp.dot(a, b, preferred_element_type=F32)


def _dot_nt(a, b, precision=None):
    return lax.dot_general(a, b, (((1,), (1,)), ((), ())), preferred_element_type=F32, precision=precision)


def _dot_tn(a, b):
    return lax.dot_general(a, b, (((0,), (0,)), ((), ())), preferred_element_type=F32)


def _rms(x, g):
    return x * lax.rsqrt(jnp.mean(x * x, axis=-1, keepdims=True) + EPS) * g


def _sigmoid(x):
    return 1.0 / (1.0 + jnp.exp(-x))


def _softplus(x):
    return jnp.maximum(x, 0.0) + jnp.log(1.0 + jnp.exp(-jnp.abs(x)))


def _token_tile(t, cap):
    tm = min(t, cap)
    assert t % tm == 0
    return tm


def _norm_matmul_body(*refs, groups, gated, col_chunk):
    x_ref, g_ref, w_ref = refs[:3]
    refs = refs[3:]
    if gated:
        gp_ref, refs = refs[0], refs[1:]
    x = x_ref[...]
    yb = _rms(x, g_ref[...]).astype(BF16)
    off, k = 0, 0
    for gi, (n, outs) in enumerate(groups):
        for c in range(0, n, col_chunk):
            cw = min(col_chunk, n - c)
            r = _dot(yb, w_ref[:, off + c:off + c + cw])
            if gated and gi == len(groups) - 1:
                rb = r + gp_ref[0:1, :]
                lane = lax.broadcasted_iota(jnp.int32, rb.shape, 1)
                e = jnp.exp(-jnp.abs(rb))
                l1p = jnp.log(1.0 + e)
                sig = _sigmoid(rb)
                gdec = -jnp.exp(gp_ref[1:2, :]) * (jnp.maximum(rb, 0.0) + l1p)
                logsig = jnp.minimum(rb, 0.0) - l1p
                r = jnp.where(lane < 4, sig, jnp.where(lane < 8, gdec, jnp.where(lane < 12, logsig, 0.0)))
            for j, (dtype, scale, heads) in enumerate(outs):
                o_ref = refs[k + j]
                rr = (r if scale == 1.0 else r * scale).astype(dtype)
                if heads == 1:
                    o_ref[:, c:c + cw] = rr
                else:
                    hw = n // heads
                    tm = x_ref.shape[0]
                    for h in range(c // hw, (c + cw) // hw):
                        o_ref[pl.ds(h, tm, stride=heads), :] = rr[:, h * hw - c:(h + 1) * hw - c]
        off += n
        k += len(outs)


def norm_matmul(x, g, w, groups, gate_params=None, tile_cap=512):
    t, d = x.shape
    tm = _token_tile(t, tile_cap)
    gated = gate_params is not None
    in_specs = [pl.BlockSpec((tm, d), lambda i: (i, 0)),
                pl.BlockSpec((1, d), lambda i: (0, 0)),
                pl.BlockSpec(w.shape, lambda i: (0, 0))]
    args = [x, g.reshape(1, d), w]
    if gated:
        in_specs.append(pl.BlockSpec(gate_params.shape, lambda i: (0, 0)))
        args.append(gate_params)
    out_shape, out_specs, out_bytes = [], [], 0
    for n, outs in groups:
        for dtype, _, heads in outs:
            out_shape.append(jax.ShapeDtypeStruct((t * heads, n // heads), dtype))
            out_specs.append(pl.BlockSpec((tm * heads, n // heads), lambda i: (i, 0)))
            out_bytes += tm * n * jnp.dtype(dtype).itemsize
    vmem = 2 * (tm * d * 4 + w.size * 2 + out_bytes) + tm * d * 8
    return pl.pallas_call(
        functools.partial(_norm_matmul_body, groups=groups, gated=gated, col_chunk=512),
        grid=(t // tm,), in_specs=in_specs, out_specs=out_specs, out_shape=out_shape,
        compiler_params=_params(("parallel",), vmem), name="norm_matmul")(*args)


def _post_body(*refs, n_in, heads, hd, ff_chunk, final_norm, col_chunk):
    x_ref = refs[0]
    a_refs, w_refs = refs[1:1 + n_in], refs[1 + n_in:1 + 2 * n_in]
    gc_ref, wq_ref, mk_ref, mv_ref, wo_ref, gm_ref, wu_ref, wd_ref = refs[1 + 2 * n_in:9 + 2 * n_in]
    gf_ref = refs[9 + 2 * n_in] if final_norm else None
    o_ref = refs[-1]
    bt, tm, d = x_ref.shape
    rows = bt * tm
    mem_len = mk_ref.shape[1] // heads
    x0 = x_ref[...].reshape(rows, d)
    a_vals = [a[...].reshape(rows, a.shape[-1]).astype(BF16) for a in a_refs]
    x1 = jnp.concatenate(
        [x0[:, c:c + col_chunk] + sum(_dot(a, w[:, c:c + col_chunk]) for a, w in zip(a_vals, w_refs))
         for c in range(0, d, col_chunk)], axis=-1)
    hb = _rms(x1, gc_ref[...]).astype(BF16)
    q = (_dot(hb, wq_ref[...]) * (hd ** -0.5)).astype(BF16)
    att = []
    for bi in range(bt):
        outs = []
        for h in range(heads):
            mem_rows = pl.ds(h, mem_len, stride=heads)
            s = _dot_nt(q[bi * tm:(bi + 1) * tm, h * hd:(h + 1) * hd], mk_ref[bi, mem_rows, :].astype(BF16))
            m = jnp.max(s, axis=-1, keepdims=True)
            p = jnp.exp(s - m)
            l = jnp.sum(p, axis=-1, keepdims=True)
            outs.append((_dot(p.astype(BF16), mv_ref[bi, mem_rows, :].astype(BF16)) / l).astype(BF16))
        att.append(jnp.concatenate(outs, axis=-1))
    x2 = x1 + _dot(jnp.concatenate(att, axis=0), wo_ref[...])
    hb = _rms(x2, gm_ref[...]).astype(BF16)
    acc = x2
    for c in range(0, wu_ref.shape[1], ff_chunk):
        u = jnp.maximum(_dot(hb, wu_ref[:, c:c + ff_chunk]), 0.0)
        acc = acc + _dot((u * u).astype(BF16), wd_ref[c:c + ff_chunk, :])
    if final_norm:
        acc = _rms(acc, gf_ref[...])
    o_ref[...] = acc.reshape(bt, tm, d)


def post_mixer(x, a_list, w_list, g_cross, wq, mk, mv, wo, g_mlp, w_up, w_down, heads, g_final=None, tile_cap=512):
    b, l, d = x.shape
    mh, hd = mk.shape[1], mk.shape[2]
    m, hw = mh // heads, hd * heads
    tm = _token_tile(l, tile_cap)
    bt = math.gcd(b, max(1, tile_cap // tm))
    rows = bt * tm
    n_in = len(a_list)
    final_norm = g_final is not None
    once = pl.Buffered(1)

    def const(shape):
        return pl.BlockSpec(shape, lambda i, j: (0,) * len(shape), pipeline_mode=once)

    in_specs = [pl.BlockSpec((bt, tm, d), lambda i, j: (i, j, 0))]
    in_specs += [pl.BlockSpec((bt, tm, a.shape[2]), lambda i, j: (i, j, 0)) for a in a_list]
    in_specs += [const(w.shape) for w in w_list]
    in_specs += [const((1, d)), const(wq.shape),
                 pl.BlockSpec((bt, mh, hd), lambda i, j: (i, 0, 0)),
                 pl.BlockSpec((bt, mh, hd), lambda i, j: (i, 0, 0)),
                 const(wo.shape), const((1, d)), const(w_up.shape), const(w_down.shape)]
    args = [x, *a_list, *w_list, g_cross.reshape(1, d), wq, mk, mv, wo, g_mlp.reshape(1, d), w_up, w_down]
    if final_norm:
        in_specs.append(const((1, d)))
        args.append(g_final.reshape(1, d))
    weights = sum(w.size for w in w_list) + wq.size + wo.size + w_up.size + w_down.size
    vmem = (weights * 2 + 2 * (2 * rows * d * 4 + 2 * bt * m * hw * 4
                               + sum(rows * a.shape[2] * a.dtype.itemsize for a in a_list))
            + rows * (d * 16 + hw * 8 + m * 8 + 512 * 8))
    return pl.pallas_call(
        functools.partial(_post_body, n_in=n_in, heads=heads, hd=hw // heads, ff_chunk=512,
                          final_norm=final_norm, col_chunk=512),
        grid=(b // bt, l // tm), in_specs=in_specs,
        out_specs=pl.BlockSpec((bt, tm, d), lambda i, j: (i, j, 0)),
        out_shape=jax.ShapeDtypeStruct((b, l, d), F32),
        compiler_params=_params(("parallel", "parallel"), vmem), name="post_mixer")(*args)


def _cumsum_body(x_ref, o_ref, *, minus_total):
    r, l = x_ref.shape
    row = lax.broadcasted_iota(jnp.int32, (LANE, LANE), 0)
    col = lax.broadcasted_iota(jnp.int32, (LANE, LANE), 1)
    upper = (row <= col).astype(BF16)
    carry = jnp.zeros((r, 1), F32)
    for c in range(0, l, LANE):
        x = x_ref[:, c:c + LANE]
        x1 = x.astype(BF16)
        res = x - x1.astype(F32)
        x2 = res.astype(BF16)
        x3 = (res - x2.astype(F32)).astype(BF16)
        y3 = _dot(jnp.concatenate([x1, x2, x3], axis=0), upper)
        y = y3[:r] + y3[r:2 * r] + y3[2 * r:] + carry
        o_ref[:, c:c + LANE] = y
        carry = y[:, LANE - 1:LANE]
    if minus_total:
        o_ref[...] = o_ref[...] - carry


def cumsum_rows(x, minus_total=False):
    r, l = x.shape
    assert l % LANE == 0 and r % SUBLANE == 0
    return pl.pallas_call(
        functools.partial(_cumsum_body, minus_total=minus_total),
        grid=(1,), in_specs=[pl.BlockSpec((r, l), lambda i: (0, 0))],
        out_specs=pl.BlockSpec((r, l), lambda i: (0, 0)),
        out_shape=jax.ShapeDtypeStruct((r, l), F32),
        compiler_params=_params(("arbitrary",), 4 * r * l * 4), name="cumsum_rows")(x)


def _fox_body(*refs, tq, tkp, n_past, nsub, heads):
    if n_past:
        q_ref, k_ref, v_ref, fk_ref, kp_ref, vp_ref, fp_ref, o_ref = refs
    else:
        q_ref, k_ref, v_ref, fk_ref, o_ref = refs
    i = pl.program_id(2)
    sub = tq // nsub
    hd = q_ref.shape[-1]
    qs = [q_ref[c * sub:(c + 1) * sub, :] for c in range(nsub)]

    def update(carry, ss, vbs):
        ms, ls, accs = carry[0::3], carry[1::3], carry[2::3]
        m_new = [jnp.maximum(m, jnp.max(s, axis=-1, keepdims=True)) for m, s in zip(ms, ss)]
        alpha = [jnp.exp(m - mn) for m, mn in zip(ms, m_new)]
        ps = [jnp.exp(s - mn) for s, mn in zip(ss, m_new)]
        ls = [a * l + jnp.sum(p, axis=-1, keepdims=True) for a, l, p in zip(alpha, ls, ps)]
        pv = [_dot(p.astype(BF16), vb) for p, vb in zip(ps, vbs)]
        accs = [a * acc + x for a, acc, x in zip(alpha, accs, pv)]
        out = []
        for c in range(nsub):
            out += [m_new[c], ls[c], accs[c]]
        return tuple(out)

    carry = (jnp.full((sub, 1), NEG, F32), jnp.zeros((sub, 1), F32), jnp.zeros((sub, hd), F32)) * nsub

    if n_past:
        def past_step(j, carry):
            rows = pl.ds(j * tkp * heads + pl.program_id(1), tkp, stride=heads)
            kb = kp_ref[rows, :].astype(BF16)
            vb = vp_ref[rows, :].astype(BF16)
            fk = fp_ref[pl.ds(j, 1), :]
            return update(carry, [_dot_nt(q, kb) - fk for q in qs], [vb] * nsub)
        carry = lax.fori_loop(0, n_past, past_step, carry)

    def self_step(j, carry):
        start = pl.multiple_of(j * tq, tq)
        kb = k_ref[pl.ds(start, tq), :]
        vb = v_ref[pl.ds(start, tq), :]
        fk = fk_ref[pl.ds(j, 1), :][:, :tq]
        return update(carry, [_dot_nt(q, kb) - fk for q in qs], [vb] * nsub)
    carry = lax.fori_loop(0, i, self_step, carry)

    start = pl.multiple_of(i * tq, tq)
    fk = fk_ref[pl.ds(i, 1), :]
    ss, vbs = [], []
    for c in range(nsub):
        w = (c + 1) * sub
        s = _dot_nt(qs[c], k_ref[pl.ds(start, w), :]) - fk[:, :w]
        row = lax.broadcasted_iota(jnp.int32, (sub, w), 0) + c * sub
        col = lax.broadcasted_iota(jnp.int32, (sub, w), 1)
        ss.append(jnp.where(col <= row, s, NEG))
        vbs.append(v_ref[pl.ds(start, w), :])
    carry = update(carry, ss, vbs)
    for c in range(nsub):
        o_ref[c * sub:(c + 1) * sub, :] = (carry[3 * c + 2] / carry[3 * c + 1]).astype(o_ref.dtype)


def fox_attention(q, k, v, f_new, heads, past=None, tq_cap=512, sub_cap=256, score_tile=64 * 1024):
    b, l, hw = q.shape
    hd = hw // heads
    tq = min(l, tq_cap)
    nq = l // tq
    nsub = max(1, tq // sub_cap)
    p_len = 0 if past is None else past[0].shape[1] // heads
    tkp = min(p_len, max(sub_cap, score_tile // tq))
    wq = -(-tq // LANE) * LANE
    if wq != tq:
        assert nq == 1
        f_new = jnp.pad(f_new, ((0, 0), (0, 0), (0, wq - tq)))
    in_specs = [pl.BlockSpec((None, tq, hd), lambda bi, h, i: (bi, i, h)),
                pl.BlockSpec((None, l, hd), lambda bi, h, i: (bi, 0, h)),
                pl.BlockSpec((None, l, hd), lambda bi, h, i: (bi, 0, h)),
                pl.BlockSpec((None, None, nq, wq), lambda bi, h, i: (bi, h, 0, 0))]
    args = [q, k, v, f_new.reshape(b, heads, nq, wq)]
    n_past = 0
    vmem = 2 * (2 * tq * hd * 2 + 2 * l * hd * 2) + 8 * tq * max(tq, tkp) * 4
    if past is not None:
        kp, vp, fp = past
        n_past = p_len // tkp
        in_specs += [pl.BlockSpec((None, p_len * heads, hd), lambda bi, h, i: (bi, 0, 0)),
                     pl.BlockSpec((None, p_len * heads, hd), lambda bi, h, i: (bi, 0, 0)),
                     pl.BlockSpec((None, None, n_past, tkp), lambda bi, h, i: (bi, h, 0, 0))]
        args += [kp, vp, fp.reshape(b, heads, n_past, tkp)]
        vmem += 2 * 2 * p_len * heads * hd * 4
    return pl.pallas_call(
        functools.partial(_fox_body, tq=tq, tkp=tkp, n_past=n_past, nsub=nsub, heads=heads),
        grid=(b, heads, nq), in_specs=in_specs,
        out_specs=pl.BlockSpec((None, tq, hd), lambda bi, h, i: (bi, i, h)),
        out_shape=jax.ShapeDtypeStruct((b, l, hw), BF16),
        compiler_params=_params(("parallel", "parallel", "arbitrary"), vmem), name="fox_attention")(*args)


def _gdn_body(qkv_ref, z_ref, gates_ref, cprev_ref, cw_ref, s0_ref, og_ref, o_ref, st_ref, ybuf_ref,
              *, c, nc, heads, dk):
    n = pl.program_id(1)
    pad = SUBLANE
    r = c * nc
    hw = heads * dk

    @pl.when(n == 0)
    def _():
        ybuf_ref[0:pad, :] = cprev_ref[...]
        st_ref[...] = s0_ref[...]

    ybuf_ref[pad:pad + r, :] = qkv_ref[...]

    row = lax.broadcasted_iota(jnp.int32, (c, c), 0)
    col = lax.broadcasted_iota(jnp.int32, (c, c), 1)
    incl = row >= col
    strict = row > col
    tril = incl.astype(BF16)
    n_double = int(math.log2(c)) - 1
    og = og_ref[...]

    chains = []
    for ci in range(nc):
        r0 = ci * c
        conv = qkv_ref[r0:r0 + c, :] * cw_ref[CONV_W - 1:CONV_W, :]
        for k in range(1, CONV_W):
            conv = conv + ybuf_ref[pad + r0 - k:pad + r0 - k + c, :] * cw_ref[CONV_W - 1 - k:CONV_W - k, :]
        act = conv * _sigmoid(conv)
        gs = gates_ref[r0:r0 + c, :]
        g1 = gs.astype(BF16)
        res = gs - g1.astype(F32)
        g2 = res.astype(BF16)
        g3 = (res - g2.astype(F32)).astype(BF16)
        cs3 = _dot(tril, jnp.concatenate([g1, g2, g3], axis=-1))
        cs = cs3[:, :LANE] + cs3[:, LANE:2 * LANE] + cs3[:, 2 * LANE:]
        cs_t = jnp.concatenate([cs, jnp.zeros((LANE - c, LANE), F32)], axis=0).T
        for h in range(heads):
            qh = act[:, h * dk:(h + 1) * dk]
            kh = act[:, hw + h * dk:hw + (h + 1) * dk]
            vh = act[:, 2 * hw + h * dk:2 * hw + (h + 1) * dk]
            qh = qh * lax.rsqrt(jnp.sum(qh * qh, axis=-1, keepdims=True) + EPS) * (dk ** -0.5)
            kh = kh * lax.rsqrt(jnp.sum(kh * kh, axis=-1, keepdims=True) + EPS)
            beta = gs[:, h:h + 1]
            gc = cs[:, heads + h:heads + h + 1]
            gc_row = cs_t[heads + h:heads + h + 1, :c]
            kb = kh * beta
            egc = jnp.exp(gc)
            g_last = gc[c - 1:c, :]
            chains.append(dict(
                decay=jnp.exp(jnp.where(incl, gc - gc_row, NEG)),
                kq=jnp.concatenate([kb, qh], axis=0).astype(BF16), khb=kh.astype(BF16),
                rhs=jnp.concatenate([vh * beta, kb * egc], axis=-1), qg=qh * egc,
                kd=(kh * jnp.exp(g_last - gc)).astype(BF16), ds=jnp.exp(g_last)))

    for ch in chains:
        ch["a1"] = _dot_nt(ch["kq"], ch["khb"])
    for ch in chains:
        lmat = jnp.where(strict, ch["a1"][:c] * ch["decay"], 0.0)
        ch["attn"] = jnp.where(incl, ch["a1"][c:] * ch["decay"], 0.0).astype(BF16)
        ch["toff"] = -lmat
        ch["pw"] = lmat
    for _ in range(n_double):
        for ch in chains:
            pwb = ch["pw"].astype(BF16)
            ch["pw"] = _dot(pwb, pwb)
        for ch in chains:
            ch["tp"] = _dot(ch["toff"].astype(BF16), ch["pw"].astype(BF16))
        for ch in chains:
            ch["toff"] = ch["toff"] + ch["pw"] + ch["tp"]
    for ch in chains:
        ch["uw"] = _dot(ch["toff"].astype(BF16), ch["rhs"].astype(BF16))
    for ch in chains:
        uw = ch["rhs"] + ch["uw"]
        ch["u"] = uw[:, :dk]
        ch["wq"] = jnp.concatenate([uw[:, dk:], ch["qg"]], axis=0).astype(BF16)

    states = [st_ref[h] for h in range(heads)]
    for ci in range(nc):
        group = chains[ci * heads:(ci + 1) * heads]
        t1 = [_dot(ch["wq"], s.astype(BF16)) for ch, s in zip(group, states)]
        vnb = [(ch["u"] - t[:c]).astype(BF16) for ch, t in zip(group, t1)]
        o2 = [_dot(ch["attn"], v) for ch, v in zip(group, vnb)]
        sd = [_dot_tn(ch["kd"], v) for ch, v in zip(group, vnb)]
        states = [s * ch["ds"] + d for ch, s, d in zip(group, states, sd)]
        for h in range(heads):
            o = t1[h][c:] + o2[h]
            zh = z_ref[ci * c:(ci + 1) * c, h * dk:(h + 1) * dk]
            o_ref[ci * c:(ci + 1) * c, h * dk:(h + 1) * dk] = (_rms(o, og) * (zh * _sigmoid(zh))).astype(o_ref.dtype)
    for h in range(heads):
        st_ref[h] = states[h]

    ybuf_ref[0:pad, :] = ybuf_ref[r:r + pad, :]


def gated_delta_net(qkv, z, gates, conv_prev, conv_w, s0, onorm_g, heads, dk, chunks_per_step=4):
    b, l, cd = qkv.shape
    c = min(CHUNK, l)
    nc = min(chunks_per_step, l // c)
    r = c * nc
    hw = heads * dk
    cprev = jnp.pad(conv_prev, ((0, 0), (SUBLANE - (CONV_W - 1), 0), (0, 0)))
    cw = jnp.pad(conv_w, ((0, SUBLANE - CONV_W), (0, 0)))
    vmem = 2 * (r * cd * 4 + r * hw * 4 + r * LANE * 4 + 2 * SUBLANE * cd * 4 + 2 * heads * dk * dk * 4
                + r * hw * 2) + (r + SUBLANE) * cd * 4 * 4
    return pl.pallas_call(
        functools.partial(_gdn_body, c=c, nc=nc, heads=heads, dk=dk),
        grid=(b, l // r),
        in_specs=[pl.BlockSpec((None, r, cd), lambda i, n: (i, n, 0)),
                  pl.BlockSpec((None, r, hw), lambda i, n: (i, n, 0)),
                  pl.BlockSpec((None, r, LANE), lambda i, n: (i, n, 0)),
                  pl.BlockSpec((None, SUBLANE, cd), lambda i, n: (i, 0, 0)),
                  pl.BlockSpec((SUBLANE, cd), lambda i, n: (0, 0)),
                  pl.BlockSpec((None, heads, dk, dk), lambda i, n: (i, 0, 0, 0)),
                  pl.BlockSpec((1, dk), lambda i, n: (0, 0))],
        out_specs=[pl.BlockSpec((None, r, hw), lambda i, n: (i, n, 0)),
                   pl.BlockSpec((None, heads, dk, dk), lambda i, n: (i, 0, 0, 0))],
        out_shape=[jax.ShapeDtypeStruct((b, l, hw), BF16),
                   jax.ShapeDtypeStruct((b, heads, dk, dk), F32)],
        scratch_shapes=[pltpu.VMEM((r + SUBLANE, cd), F32)],
        compiler_params=_params(("parallel", "arbitrary"), vmem), name="gated_delta_net")(
            qkv, z, gates, cprev, cw, s0, onorm_g.reshape(1, dk))


def _swa_body(*refs, n_seg, cq, nchunk, win, hkv, grp, hd, chunked):
    sink_ref, q_ref = refs[0], refs[1]
    k_refs, v_refs, o_ref = refs[2:2 + n_seg], refs[2 + n_seg:2 + 2 * n_seg], refs[2 + 2 * n_seg]
    kk = jnp.concatenate([r[...] for r in k_refs], axis=0).astype(BF16)
    vv = jnp.concatenate([r[...] for r in v_refs], axis=0).astype(BF16)
    row = lax.broadcasted_iota(jnp.int32, (cq, win), 0)
    col = lax.broadcasted_iota(jnp.int32, (cq, win), 1)
    dist = jnp.abs(row - col + WINDOW).astype(F32)
    hq = hkv * grp
    k_heads = [kk[:, j * hd:(j + 1) * hd] for j in range(hkv)]
    v_heads = [vv[:, j * hd:(j + 1) * hd] for j in range(hkv)]
    chains = []
    for c in range(nchunk):
        dist_c = dist
        if chunked and c * cq < WINDOW:
            first_valid = WINDOW - (pl.program_id(1) * nchunk + c) * cq
            dist_c = jnp.where(col >= first_valid, dist, -NEG)
        for j in range(hkv):
            q_stack = jnp.concatenate(
                [q_ref[c * cq:(c + 1) * cq, (j * grp + g) * hd:(j * grp + g + 1) * hd] for g in range(grp)], axis=0)
            chains.append(dict(c=c, j=j, dist=dist_c, q=q_stack,
                               k=k_heads[j][c * cq:c * cq + win, :], v=v_heads[j][c * cq:c * cq + win, :]))

    def scores(ch):
        ch["s"] = _dot_nt(ch["q"], ch["k"])

    def softmax_pv(ch):
        ps, ch["l"] = [], []
        for g in range(grp):
            h = ch["j"] * grp + g
            slope = 2.0 ** (-8.0 * (h + 1) / hq)
            sg = ch["s"][g * cq:(g + 1) * cq, :] - slope * ch["dist"]
            sink = sink_ref[h]
            m = jnp.maximum(jnp.max(sg, axis=-1, keepdims=True), sink)
            p = jnp.exp(sg - m)
            ch["l"].append(jnp.sum(p, axis=-1, keepdims=True) + jnp.exp(sink - m))
            ps.append(p.astype(BF16))
        ch["pv"] = _dot(jnp.concatenate(ps, axis=0), ch["v"])

    def emit(ch):
        c, j = ch["c"], ch["j"]
        outs = [(ch["pv"][g * cq:(g + 1) * cq, :] / ch["l"][g]).astype(o_ref.dtype) for g in range(grp)]
        o_ref[c * cq:(c + 1) * cq, j * grp * hd:(j + 1) * grp * hd] = jnp.concatenate(outs, axis=-1)

    scores(chains[0])
    for i, ch in enumerate(chains):
        if i + 1 < len(chains):
            scores(chains[i + 1])
        softmax_pv(ch)
        if i > 0:
            emit(chains[i - 1])
    emit(chains[-1])


def swa_attention(q, segs, sinks, hkv, grp, hd, chunked, chunks_per_step=4):
    b, l, qw = q.shape
    kw = hkv * hd
    if chunked:
        cq, nchunk, n_seg, win = CHUNK, chunks_per_step, 2, WINDOW + CHUNK
        rows = cq * nchunk
        assert l % rows == 0 and rows % WINDOW == 0
        k, v = segs
        k_specs = [pl.BlockSpec((None, WINDOW, kw), lambda i, n: (i, jnp.maximum(n * (rows // WINDOW) - 1, 0), 0)),
                   pl.BlockSpec((None, rows, kw), lambda i, n: (i, n, 0))]
        k_args, v_args = [k] * n_seg, [v] * n_seg
        key_rows = WINDOW + rows
    else:
        cq, nchunk, n_seg = l, 1, len(segs)
        rows = l
        k_specs = [pl.BlockSpec((None, kk.shape[1], kw), lambda i, n: (i, 0, 0)) for kk, _ in segs]
        k_args, v_args = [kk for kk, _ in segs], [vv for _, vv in segs]
        key_rows = win = sum(kk.shape[1] for kk in k_args)
    vmem = 2 * (2 * rows * qw * 2 + 2 * key_rows * kw * 4) + 3 * nchunk * hkv * grp * cq * 256 * 4
    return pl.pallas_call(
        functools.partial(_swa_body, n_seg=n_seg, cq=cq, nchunk=nchunk, win=win, hkv=hkv, grp=grp, hd=hd,
                          chunked=chunked),
        grid=(b, l // rows),
        in_specs=[pl.BlockSpec(memory_space=pltpu.SMEM),
                  pl.BlockSpec((None, rows, qw), lambda i, n: (i, n, 0))] + k_specs + k_specs,
        out_specs=pl.BlockSpec((None, rows, qw), lambda i, n: (i, n, 0)),
        out_shape=jax.ShapeDtypeStruct((b, l, qw), BF16),
        compiler_params=_params(("parallel", "parallel"), vmem), name="swa_attention")(
            sinks, q, *k_args, *v_args)


def _ab_layer(x, conv_prev, s0, past, w_in, conv_w, a_log, dt_bias, onorm_g, f_bias, w_out, g_mix, dims):
    b, l, d = x.shape
    h_a, dk, h_b, hd_b = dims
    a_qk, a_v, b_w = h_a * dk, h_a * dk, h_b * hd_b
    conv_dim = 2 * a_qk + a_v
    o1 = conv_dim + a_v
    o2 = o1 + 2 * h_a
    small = jnp.concatenate([w_in[:, o1:o2], w_in[:, o2 + 3 * b_w:],
                             jnp.zeros((d, LANE - 2 * h_a - h_b), F32)], axis=1)
    w_pack = jnp.concatenate([w_in[:, :o1], w_in[:, o2:o2 + 3 * b_w], small], axis=1).astype(BF16)
    zeros4 = jnp.zeros((h_a,), F32)
    gp = jnp.zeros((SUBLANE, LANE), F32)
    gp = gp.at[0, :3 * h_a].set(jnp.concatenate([zeros4, dt_bias, f_bias]))
    gp = gp.at[1, h_a:2 * h_a].set(a_log)
    f32o, b16, cache = (F32, 1.0, 1), (BF16, 1.0, 1), (F32, 1.0, h_b)
    groups = ((conv_dim, (f32o,)), (a_v, (f32o,)), (b_w, ((BF16, hd_b ** -0.5, 1),)),
              (b_w, (cache, b16)), (b_w, (cache, b16)), (LANE, (f32o,)))
    qkv, z, qb, kb, kb16, vb, vb16, gates = norm_matmul(x.reshape(b * l, d), g_mix, w_pack, groups, gate_params=gp)
    r3 = lambda a: a.reshape(b, l, a.shape[-1])
    qkv3, gates3 = r3(qkv), r3(gates)
    oa, s_new = gated_delta_net(qkv3, r3(z), gates3, conv_prev, conv_w, s0, onorm_g, h_a, dk)
    conv_new = qkv3[:, l - (CONV_W - 1):, :]
    logf = gates3[:, :, 2 * h_a:2 * h_a + h_b]
    lf_rows = jnp.pad(logf.transpose(0, 2, 1).reshape(b * h_b, l), ((0, 0), (0, -l % LANE)))
    f_new = cumsum_rows(lf_rows)[:, :l].reshape(b, h_b, l)
    fox_past = None
    if past is not None:
        pk, pv, plf = past
        p_len = pk.shape[1]
        f_past = cumsum_rows(plf.transpose(0, 2, 1).reshape(b * h_b, p_len), minus_total=True)
        fox_past = (pk.reshape(b, p_len * h_b, hd_b), pv.reshape(b, p_len * h_b, hd_b), f_past.reshape(b, h_b, p_len))
    ob = fox_attention(r3(qb), r3(kb16), r3(vb16), f_new, h_b, past=fox_past)
    w_out16 = w_out.astype(BF16)
    mixed = ([oa, ob], [w_out16[:a_v], w_out16[a_v:]])
    return (mixed, s_new, conv_new, kb.reshape(b, l, h_b, hd_b), vb.reshape(b, l, h_b, hd_b), logf)


def _c_layer(x, past, w_in, sinks, w_out, g_mix, dims):
    b, l, d = x.shape
    hq, hkv, hd = dims
    grp = hq // hkv
    f32o = (F32, 1.0, 1)
    q, k, v = norm_matmul(x.reshape(b * l, d), g_mix, w_in.astype(BF16),
                          ((hq * hd, ((BF16, hd ** -0.5, 1),)), (hkv * hd, (f32o,)), (hkv * hd, (f32o,))))
    q3, k3, v3 = q.reshape(b, l, hq * hd), k.reshape(b, l, hkv * hd), v.reshape(b, l, hkv * hd)
    if past is None:
        o = swa_attention(q3, (k3, v3), sinks, hkv, grp, hd, chunked=True)
        new_k, new_v = k3[:, l - WINDOW:], v3[:, l - WINDOW:]
    else:
        pk, pv = past
        w_len = pk.shape[1]
        o = swa_attention(q3, ((pk.reshape(b, w_len, hkv * hd), pv.reshape(b, w_len, hkv * hd)), (k3, v3)),
                          sinks, hkv, grp, hd, chunked=False)
        new_k, new_v = k3, v3
    return ([o], [w_out.astype(BF16)]), new_k.reshape(b, -1, hkv, hd), new_v.reshape(b, -1, hkv, hd)


def kernel(x_prompt, mem_prompt, x_sample, state_gdn, state_gdn_conv, cache_fox_k, cache_fox_v, cache_fox_logf, cache_swa_k, cache_swa_v, cache_mem_k, cache_mem_v, g_mix, w_in_ab, conv_w_a, a_log_a, dt_bias_a, onorm_g_a, f_bias_b, w_out_ab, w_in_c, sinks_c, w_out_c, g_mem, w_xkv, g_cross, w_xq, w_xo, g_mlp, w_up, w_down, g_final):
    depth = g_mix.shape[0]
    bp, lp, d = x_prompt.shape
    bs, ls, _ = x_sample.shape
    h_a, dk = state_gdn.shape[2], state_gdn.shape[3]
    h_b, hd_b = cache_fox_k.shape[3], cache_fox_k.shape[4]
    hkv, hd_c = cache_swa_k.shape[3], cache_swa_k.shape[4]
    hq = sinks_c.shape[1]
    m_len, h_x, hd_x = cache_mem_k.shape[2], cache_mem_k.shape[3], cache_mem_k.shape[4]
    xw = h_x * hd_x
    conv_dim = state_gdn_conv.shape[-1]

    xp, xs = x_prompt, x_sample
    p_out = {k: [] for k in ("s", "c", "fk", "fv", "fl", "wk", "wv", "mk", "mv")}
    s_out = {k: [] for k in ("s", "c", "fk", "fv", "fl", "wk", "wv")}
    for layer in range(depth):
        i = layer // 2
        if layer % 2 == 0:
            wts = (w_in_ab[i], conv_w_a[i], a_log_a[i], dt_bias_a[i], onorm_g_a[i], f_bias_b[i], w_out_ab[i],
                   g_mix[layer], (h_a, dk, h_b, hd_b))
            mix_p, st, cv, fk, fv, fl = _ab_layer(xp, jnp.zeros((bp, CONV_W - 1, conv_dim), F32),
                                                  jnp.zeros((bp, h_a, dk, dk), F32), None, *wts)
            for key, val in zip(("s", "c", "fk", "fv", "fl"), (st, cv, fk, fv, fl)):
                p_out[key].append(val)
            mix_s, st, cv, fk, fv, fl = _ab_layer(xs, state_gdn_conv[i], state_gdn[i],
                                                  (cache_fox_k[i], cache_fox_v[i], cache_fox_logf[i]), *wts)
            for key, val in zip(("s", "c", "fk", "fv", "fl"), (st, cv, fk, fv, fl)):
                s_out[key].append(val)
        else:
            wts = (w_in_c[i], sinks_c[i], w_out_c[i], g_mix[layer], (hq, hkv, hd_c))
            mix_p, wk, wv = _c_layer(xp, None, *wts)
            p_out["wk"].append(wk); p_out["wv"].append(wv)
            mix_s, wk, wv = _c_layer(xs, (cache_swa_k[i], cache_swa_v[i]), *wts)
            s_out["wk"].append(wk); s_out["wv"].append(wv)
        mem_outs = ((F32, 1.0, h_x),)
        mk, mv = norm_matmul(mem_prompt.reshape(bp * m_len, d), g_mem[layer], w_xkv[layer].astype(BF16),
                             ((xw, mem_outs), (xw, mem_outs)))
        p_out["mk"].append(mk.reshape(bp, m_len, h_x, hd_x)); p_out["mv"].append(mv.reshape(bp, m_len, h_x, hd_x))
        post_w = (g_cross[layer], w_xq[layer].astype(BF16))
        mlp_w = (w_xo[layer].astype(BF16), g_mlp[layer], w_up[layer].astype(BF16), w_down[layer].astype(BF16), h_x,
                 g_final if layer == depth - 1 else None)
        xp = post_mixer(xp, *mix_p, *post_w, mk.reshape(bp, m_len * h_x, hd_x), mv.reshape(bp, m_len * h_x, hd_x),
                        *mlp_w)
        xs = post_mixer(xs, *mix_s, *post_w, cache_mem_k[layer].reshape(bs, m_len * h_x, hd_x),
                        cache_mem_v[layer].reshape(bs, m_len * h_x, hd_x), *mlp_w)
    st = jnp.stack
    return (xp, xs,
            st(p_out["s"]), st(p_out["c"]), st(p_out["fk"]), st(p_out["fv"]), st(p_out["fl"]),
            st(p_out["wk"]), st(p_out["wv"]), st(p_out["mk"]), st(p_out["mv"]),
            st(s_out["s"]), st(s_out["c"]), st(s_out["fk"]), st(s_out["fv"]), st(s_out["fl"]),
            st(s_out["wk"]), st(s_out["wv"]))
```

```python
import functools
import math

import jax
import jax.numpy as jnp
from jax import lax
from jax.experimental import pallas as pl
from jax.experimental.pallas import tpu as pltpu

F32 = jnp.float32
BF16 = jnp.bfloat16
LOG2E = math.log2(math.e)
EPS = 1e-6
NEG = -1e30

CHUNK = 64
CONV_W = 4
WINDOW = 128
LANE = 128
SUBLANE = 8
VMEM_BYTES = 64 * 1024 * 1024


def _params(semantics, vmem_bytes):
    limit = int(min(max(vmem_bytes * 1.25, 16 * 1024 * 1024), VMEM_BYTES - 8 * 1024 * 1024))
    return pltpu.CompilerParams(dimension_semantics=semantics, vmem_limit_bytes=limit)


def _dot(a, b):
    return jnp.dot(a, b, preferred_element_type=F32)


def _dot_nt(a, b):
    return lax.dot_general(a, b, (((1,), (1,)), ((), ())), preferred_element_type=F32)


def _dot_tn(a, b):
    return lax.dot_general(a, b, (((0,), (0,)), ((), ())), preferred_element_type=F32)


def _rms(x, g):
    return x * lax.rsqrt(jnp.mean(x * x, axis=-1, keepdims=True) + EPS) * g


def _sigmoid(x):
    return 1.0 / (1.0 + jnp.exp(-x))


def _softplus(x):
    return jnp.maximum(x, 0.0) + jnp.log(1.0 + jnp.exp(-jnp.abs(x)))


def _token_tile(t, cap):
    tm = min(t, cap)
    assert t % tm == 0
    return tm


def _norm_matmul_body(*refs, groups, gated, col_chunk):
    x_ref, g_ref, w_ref = refs[:3]
    refs = refs[3:]
    if gated:
        gp_ref, refs = refs[0], refs[1:]
    x = x_ref[...]
    yb = _rms(x, g_ref[...]).astype(BF16)
    off, k = 0, 0
    for gi, (n, outs) in enumerate(groups):
        for c in range(0, n, col_chunk):
            cw = min(col_chunk, n - c)
            r = _dot(yb, w_ref[:, off + c:off + c + cw])
            if gated and gi == len(groups) - 1:
                rb = r + gp_ref[0:1, :]
                lane = lax.broadcasted_iota(jnp.int32, rb.shape, 1)
                e = jnp.exp(-jnp.abs(rb))
                l1p = jnp.log(1.0 + e)
                sig = _sigmoid(rb)
                gdec = -jnp.exp(gp_ref[1:2, :]) * (jnp.maximum(rb, 0.0) + l1p)
                logsig = jnp.minimum(rb, 0.0) - l1p
                r = jnp.where(lane < 4, sig, jnp.where(lane < 8, gdec, jnp.where(lane < 12, logsig, 0.0)))
            for j, (dtype, scale, heads) in enumerate(outs):
                o_ref = refs[k + j]
                rr = (r if scale == 1.0 else r * scale).astype(dtype)
                if heads == 1:
                    o_ref[:, c:c + cw] = rr
                else:
                    hw = n // heads
                    tm = x_ref.shape[0]
                    for h in range(c // hw, (c + cw) // hw):
                        o_ref[pl.ds(h, tm, stride=heads), :] = rr[:, h * hw - c:(h + 1) * hw - c]
        off += n
        k += len(outs)


def norm_matmul(x, g, w, groups, gate_params=None, tile_cap=512):
    t, d = x.shape
    tm = _token_tile(t, tile_cap)
    gated = gate_params is not None
    in_specs = [pl.BlockSpec((tm, d), lambda i: (i, 0)),
                pl.BlockSpec((1, d), lambda i: (0, 0)),
                pl.BlockSpec(w.shape, lambda i: (0, 0))]
    args = [x, g.reshape(1, d), w]
    if gated:
        in_specs.append(pl.BlockSpec(gate_params.shape, lambda i: (0, 0)))
        args.append(gate_params)
    out_shape, out_specs, out_bytes = [], [], 0
    for n, outs in groups:
        for dtype, _, heads in outs:
            out_shape.append(jax.ShapeDtypeStruct((t * heads, n // heads), dtype))
            out_specs.append(pl.BlockSpec((tm * heads, n // heads), lambda i: (i, 0)))
            out_bytes += tm * n * jnp.dtype(dtype).itemsize
    vmem = 2 * (tm * d * 4 + w.size * 2 + out_bytes) + tm * d * 8
    return pl.pallas_call(
        functools.partial(_norm_matmul_body, groups=groups, gated=gated, col_chunk=512),
        grid=(t // tm,), in_specs=in_specs, out_specs=out_specs, out_shape=out_shape,
        compiler_params=_params(("parallel",), vmem), name="norm_matmul")(*args)


def _post_body(*refs, n_in, heads, hd, ff_chunk, final_norm, col_chunk):
    x_ref = refs[0]
    a_refs, w_refs = refs[1:1 + n_in], refs[1 + n_in:1 + 2 * n_in]
    gc_ref, wq_ref, mk_ref, mv_ref, wo_ref, gm_ref, wu_ref, wd_ref = refs[1 + 2 * n_in:9 + 2 * n_in]
    gf_ref = refs[9 + 2 * n_in] if final_norm else None
    o_ref = refs[-1]
    bt, tm, d = x_ref.shape
    rows = bt * tm
    mem_len = mk_ref.shape[1] // heads
    x0 = x_ref[...].reshape(rows, d)
    a_vals = [a[...].reshape(rows, a.shape[-1]).astype(BF16) for a in a_refs]
    x1 = jnp.concatenate(
        [x0[:, c:c + col_chunk] + sum(_dot(a, w[:, c:c + col_chunk]) for a, w in zip(a_vals, w_refs))
         for c in range(0, d, col_chunk)], axis=-1)
    hb = _rms(x1, gc_ref[...]).astype(BF16)
    q = (_dot(hb, wq_ref[...]) * (hd ** -0.5)).astype(BF16)
    att = []
    for bi in range(bt):
        outs = []
        for h in range(heads):
            mem_rows = pl.ds(h, mem_len, stride=heads)
            s = _dot_nt(q[bi * tm:(bi + 1) * tm, h * hd:(h + 1) * hd], mk_ref[bi, mem_rows, :].astype(BF16))
            m = jnp.max(s, axis=-1, keepdims=True)
            p = jnp.exp(s - m)
            l = jnp.sum(p, axis=-1, keepdims=True)
            outs.append((_dot(p.astype(BF16), mv_ref[bi, mem_rows, :].astype(BF16)) / l).astype(BF16))
        att.append(jnp.concatenate(outs, axis=-1))
    x2 = x1 + _dot(jnp.concatenate(att, axis=0), wo_ref[...])
    hb = _rms(x2, gm_ref[...]).astype(BF16)
    acc = x2
    for c in range(0, wu_ref.shape[1], ff_chunk):
        u = jnp.maximum(_dot(hb, wu_ref[:, c:c + ff_chunk]), 0.0)
        acc = acc + _dot((u * u).astype(BF16), wd_ref[c:c + ff_chunk, :])
    if final_norm:
        acc = _rms(acc, gf_ref[...])
    o_ref[...] = acc.reshape(bt, tm, d)


def post_mixer(x, a_list, w_list, g_cross, wq, mk, mv, wo, g_mlp, w_up, w_down, heads, g_final=None, tile_cap=512):
    b, l, d = x.shape
    mh, hd = mk.shape[1], mk.shape[2]
    m, hw = mh // heads, hd * heads
    tm = _token_tile(l, tile_cap)
    bt = math.gcd(b, max(1, tile_cap // tm))
    rows = bt * tm
    n_in = len(a_list)
    final_norm = g_final is not None
    once = pl.Buffered(1)

    def const(shape):
        return pl.BlockSpec(shape, lambda i, j: (0,) * len(shape), pipeline_mode=once)

    in_specs = [pl.BlockSpec((bt, tm, d), lambda i, j: (i, j, 0))]
    in_specs += [pl.BlockSpec((bt, tm, a.shape[2]), lambda i, j: (i, j, 0)) for a in a_list]
    in_specs += [const(w.shape) for w in w_list]
    in_specs += [const((1, d)), const(wq.shape),
                 pl.BlockSpec((bt, mh, hd), lambda i, j: (i, 0, 0)),
                 pl.BlockSpec((bt, mh, hd), lambda i, j: (i, 0, 0)),
                 const(wo.shape), const((1, d)), const(w_up.shape), const(w_down.shape)]
    args = [x, *a_list, *w_list, g_cross.reshape(1, d), wq, mk, mv, wo, g_mlp.reshape(1, d), w_up, w_down]
    if final_norm:
        in_specs.append(const((1, d)))
        args.append(g_final.reshape(1, d))
    weights = sum(w.size for w in w_list) + wq.size + wo.size + w_up.size + w_down.size
    vmem = (weights * 2 + 2 * (2 * rows * d * 4 + 2 * bt * m * hw * 4
                               + sum(rows * a.shape[2] * a.dtype.itemsize for a in a_list))
            + rows * (d * 16 + hw * 8 + m * 8 + 512 * 8))
    return pl.pallas_call(
        functools.partial(_post_body, n_in=n_in, heads=heads, hd=hw // heads, ff_chunk=512,
                          final_norm=final_norm, col_chunk=512),
        grid=(b // bt, l // tm), in_specs=in_specs,
        out_specs=pl.BlockSpec((bt, tm, d), lambda i, j: (i, j, 0)),
        out_shape=jax.ShapeDtypeStruct((b, l, d), F32),
        compiler_params=_params(("parallel", "parallel"), vmem), name="post_mixer")(*args)


def _cumsum_body(x_ref, o_ref, *, minus_total, scale):
    r, l = x_ref.shape
    row = lax.broadcasted_iota(jnp.int32, (LANE, LANE), 0)
    col = lax.broadcasted_iota(jnp.int32, (LANE, LANE), 1)
    upper = (row <= col).astype(BF16)
    carry = jnp.zeros((r, 1), F32)
    for c in range(0, l, LANE):
        x = x_ref[:, c:c + LANE]
        x1 = x.astype(BF16)
        res = x - x1.astype(F32)
        x2 = res.astype(BF16)
        x3 = (res - x2.astype(F32)).astype(BF16)
        y3 = _dot(jnp.concatenate([x1, x2, x3], axis=0), upper)
        y = y3[:r] + y3[r:2 * r] + y3[2 * r:] + carry
        o_ref[:, c:c + LANE] = y
        carry = y[:, LANE - 1:LANE]
    total = carry if minus_total else 0.0
    o_ref[...] = (o_ref[...] - total) * scale


def cumsum_rows(x, scale, minus_total=False):
    r, l = x.shape
    assert l % LANE == 0 and r % SUBLANE == 0
    return pl.pallas_call(
        functools.partial(_cumsum_body, minus_total=minus_total, scale=scale),
        grid=(1,), in_specs=[pl.BlockSpec((r, l), lambda i: (0, 0))],
        out_specs=pl.BlockSpec((r, l), lambda i: (0, 0)),
        out_shape=jax.ShapeDtypeStruct((r, l), F32),
        compiler_params=_params(("arbitrary",), 4 * r * l * 4), name="cumsum_rows")(x)


def _fox_body(*refs, tq, tkp, n_past, nsub, heads):
    if n_past:
        q_ref, k_ref, v_ref, fk_ref, kp_ref, vp_ref, fp_ref, o_ref = refs
    else:
        q_ref, k_ref, v_ref, fk_ref, o_ref = refs
    i = pl.program_id(2)
    sub = tq // nsub
    hd = q_ref.shape[-1]
    qs = [q_ref[c * sub:(c + 1) * sub, :] for c in range(nsub)]

    def update(carry, ss, vbs):
        ms, ls, accs = carry[0::3], carry[1::3], carry[2::3]
        m_new = [jnp.maximum(m, jnp.max(s, axis=-1, keepdims=True)) for m, s in zip(ms, ss)]
        alpha = [jnp.exp2(m - mn) for m, mn in zip(ms, m_new)]
        ps = [jnp.exp2(s - mn) for s, mn in zip(ss, m_new)]
        ls = [a * l + jnp.sum(p, axis=-1, keepdims=True) for a, l, p in zip(alpha, ls, ps)]
        pv = [_dot(p.astype(BF16), vb) for p, vb in zip(ps, vbs)]
        accs = [a * acc + x for a, acc, x in zip(alpha, accs, pv)]
        out = []
        for c in range(nsub):
            out += [m_new[c], ls[c], accs[c]]
        return tuple(out)

    carry = (jnp.full((sub, 1), NEG, F32), jnp.zeros((sub, 1), F32), jnp.zeros((sub, hd), F32)) * nsub

    if n_past:
        def past_step(j, carry):
            rows = pl.ds(j * tkp * heads + pl.program_id(1), tkp, stride=heads)
            kb = kp_ref[rows, :].astype(BF16)
            vb = vp_ref[rows, :].astype(BF16)
            fk = fp_ref[pl.ds(j, 1), :]
            return update(carry, [_dot_nt(q, kb) - fk for q in qs], [vb] * nsub)
        carry = lax.fori_loop(0, n_past, past_step, carry)

    def self_step(j, carry):
        start = pl.multiple_of(j * sub, sub)
        kb = k_ref[pl.ds(start, sub), :]
        vb = v_ref[pl.ds(start, sub), :]
        fk = fk_ref[pl.ds(j, 1), :][:, :sub]
        return update(carry, [_dot_nt(q, kb) - fk for q in qs], [vb] * nsub)
    carry = lax.fori_loop(0, i * nsub, self_step, carry)

    start = pl.multiple_of(i * tq, tq)
    fks = [fk_ref[pl.ds(i * nsub + c, 1), :] for c in range(nsub)]
    ss, vbs = [], []
    for c in range(nsub):
        w = (c + 1) * sub
        fk = fks[0][:, :sub] if c == 0 else jnp.concatenate(fks[:c + 1], axis=1)
        s = _dot_nt(qs[c], k_ref[pl.ds(start, w), :]) - fk
        row = lax.broadcasted_iota(jnp.int32, (sub, w), 0) + c * sub
        col = lax.broadcasted_iota(jnp.int32, (sub, w), 1)
        ss.append(jnp.where(col <= row, s, NEG))
        vbs.append(v_ref[pl.ds(start, w), :])
    carry = update(carry, ss, vbs)
    for c in range(nsub):
        o_ref[c * sub:(c + 1) * sub, :] = (carry[3 * c + 2] / carry[3 * c + 1]).astype(o_ref.dtype)


def fox_attention(q, k, v, f_new, heads, past=None, tq_cap=1024, sub_cap=256, score_tile=64 * 1024):
    b, l, hw = q.shape
    hd = hw // heads
    tq = min(l, tq_cap)
    nq = l // tq
    nsub = max(1, tq // sub_cap)
    p_len = 0 if past is None else past[0].shape[1] // heads
    tkp = min(p_len, max(sub_cap, score_tile // tq))
    sub = tq // nsub
    nk = l // sub
    wk = -(-sub // LANE) * LANE
    if wk != sub:
        assert nk == 1
        f_new = jnp.pad(f_new, ((0, 0), (0, 0), (0, wk - sub)))
    in_specs = [pl.BlockSpec((None, tq, hd), lambda bi, h, i: (bi, i, h)),
                pl.BlockSpec((None, l, hd), lambda bi, h, i: (bi, 0, h)),
                pl.BlockSpec((None, l, hd), lambda bi, h, i: (bi, 0, h)),
                pl.BlockSpec((None, None, nk, wk), lambda bi, h, i: (bi, h, 0, 0))]
    args = [q, k, v, f_new.reshape(b, heads, nk, wk)]
    n_past = 0
    vmem = 2 * (2 * tq * hd * 2 + 2 * l * hd * 2) + 8 * tq * max(tq, tkp) * 4
    if past is not None:
        kp, vp, fp = past
        n_past = p_len // tkp
        in_specs += [pl.BlockSpec((None, p_len * heads, hd), lambda bi, h, i: (bi, 0, 0)),
                     pl.BlockSpec((None, p_len * heads, hd), lambda bi, h, i: (bi, 0, 0)),
                     pl.BlockSpec((None, None, n_past, tkp), lambda bi, h, i: (bi, h, 0, 0))]
        args += [kp, vp, fp.reshape(b, heads, n_past, tkp)]
        vmem += 2 * 2 * p_len * heads * hd * 4
    return pl.pallas_call(
        functools.partial(_fox_body, tq=tq, tkp=tkp, n_past=n_past, nsub=nsub, heads=heads),
        grid=(b, heads, nq), in_specs=in_specs,
        out_specs=pl.BlockSpec((None, tq, hd), lambda bi, h, i: (bi, i, h)),
        out_shape=jax.ShapeDtypeStruct((b, l, hw), BF16),
        compiler_params=_params(("parallel", "parallel", "arbitrary"), vmem), name="fox_attention")(*args)


def _gdn_body(qkv_ref, z_ref, gates_ref, cprev_ref, cw_ref, s0_ref, og_ref, o_ref, st_ref, ybuf_ref,
              *, c, nc, heads, dk):
    n = pl.program_id(1)
    pad = SUBLANE
    r = c * nc
    hw = heads * dk

    @pl.when(n == 0)
    def _():
        ybuf_ref[0:pad, :] = cprev_ref[...]
        st_ref[...] = s0_ref[...]

    ybuf_ref[pad:pad + r, :] = qkv_ref[...]

    row = lax.broadcasted_iota(jnp.int32, (c, c), 0)
    col = lax.broadcasted_iota(jnp.int32, (c, c), 1)
    incl = row >= col
    strict = row > col
    tril = incl.astype(BF16)
    n_double = int(math.log2(c)) - 1
    og = og_ref[...]

    chains = []
    for ci in range(nc):
        r0 = ci * c
        conv = qkv_ref[r0:r0 + c, :] * cw_ref[CONV_W - 1:CONV_W, :]
        for k in range(1, CONV_W):
            conv = conv + ybuf_ref[pad + r0 - k:pad + r0 - k + c, :] * cw_ref[CONV_W - 1 - k:CONV_W - k, :]
        act = conv * _sigmoid(conv)
        gs = gates_ref[r0:r0 + c, :]
        g1 = gs.astype(BF16)
        res = gs - g1.astype(F32)
        g2 = res.astype(BF16)
        g3 = (res - g2.astype(F32)).astype(BF16)
        cs3 = _dot(tril, jnp.concatenate([g1, g2, g3], axis=-1))
        cs = cs3[:, :LANE] + cs3[:, LANE:2 * LANE] + cs3[:, 2 * LANE:]
        cs_t = jnp.concatenate([cs, jnp.zeros((LANE - c, LANE), F32)], axis=0).T
        for h in range(heads):
            qh = act[:, h * dk:(h + 1) * dk]
            kh = act[:, hw + h * dk:hw + (h + 1) * dk]
            vh = act[:, 2 * hw + h * dk:2 * hw + (h + 1) * dk]
            qh = qh * lax.rsqrt(jnp.sum(qh * qh, axis=-1, keepdims=True) + EPS) * (dk ** -0.5)
            kh = kh * lax.rsqrt(jnp.sum(kh * kh, axis=-1, keepdims=True) + EPS)
            beta = gs[:, h:h + 1]
            gc = cs[:, heads + h:heads + h + 1]
            gc_row = cs_t[heads + h:heads + h + 1, :c]
            kb = kh * beta
            egc = jnp.exp(gc)
            g_last = gc[c - 1:c, :]
            chains.append(dict(
                decay=jnp.exp(jnp.where(incl, gc - gc_row, NEG)),
                kq=jnp.concatenate([kb, qh], axis=0).astype(BF16), khb=kh.astype(BF16),
                rhs=jnp.concatenate([vh * beta, kb * egc], axis=-1), qg=qh * egc,
                kd=(kh * jnp.exp(g_last - gc)).astype(BF16), ds=jnp.exp(g_last)))

    for ch in chains:
        ch["a1"] = _dot_nt(ch["kq"], ch["khb"])
    for ch in chains:
        lmat = jnp.where(strict, ch["a1"][:c] * ch["decay"], 0.0)
        ch["attn"] = jnp.where(incl, ch["a1"][c:] * ch["decay"], 0.0).astype(BF16)
        ch["toff"] = -lmat
        ch["pw"] = lmat
    for _ in range(n_double):
        for ch in chains:
            pwb = ch["pw"].astype(BF16)
            ch["pw"] = _dot(pwb, pwb)
        for ch in chains:
            ch["tp"] = _dot(ch["toff"].astype(BF16), ch["pw"].astype(BF16))
        for ch in chains:
            ch["toff"] = ch["toff"] + ch["pw"] + ch["tp"]
    for ch in chains:
        ch["uw"] = _dot(ch["toff"].astype(BF16), ch["rhs"].astype(BF16))
    for ch in chains:
        uw = ch["rhs"] + ch["uw"]
        ch["u"] = uw[:, :dk]
        ch["wq"] = jnp.concatenate([uw[:, dk:], ch["qg"]], axis=0).astype(BF16)

    states = [st_ref[h] for h in range(heads)]
    for ci in range(nc):
        group = chains[ci * heads:(ci + 1) * heads]
        t1 = [_dot(ch["wq"], s.astype(BF16)) for ch, s in zip(group, states)]
        vnb = [(ch["u"] - t[:c]).astype(BF16) for ch, t in zip(group, t1)]
        o2 = [_dot(ch["attn"], v) for ch, v in zip(group, vnb)]
        sd = [_dot_tn(ch["kd"], v) for ch, v in zip(group, vnb)]
        states = [s * ch["ds"] + d for ch, s, d in zip(group, states, sd)]
        for h in range(heads):
            o = t1[h][c:] + o2[h]
            zh = z_ref[ci * c:(ci + 1) * c, h * dk:(h + 1) * dk]
            o_ref[ci * c:(ci + 1) * c, h * dk:(h + 1) * dk] = (_rms(o, og) * (zh * _sigmoid(zh))).astype(o_ref.dtype)
    for h in range(heads):
        st_ref[h] = states[h]

    ybuf_ref[0:pad, :] = ybuf_ref[r:r + pad, :]


def gated_delta_net(qkv, z, gates, conv_prev, conv_w, s0, onorm_g, heads, dk, chunks_per_step=4):
    b, l, cd = qkv.shape
    c = min(CHUNK, l)
    nc = min(chunks_per_step, l // c)
    r = c * nc
    hw = heads * dk
    cprev = jnp.pad(conv_prev, ((0, 0), (SUBLANE - (CONV_W - 1), 0), (0, 0)))
    cw = jnp.pad(conv_w, ((0, SUBLANE - CONV_W), (0, 0)))
    vmem = 2 * (r * cd * 4 + r * hw * 4 + r * LANE * 4 + 2 * SUBLANE * cd * 4 + 2 * heads * dk * dk * 4
                + r * hw * 2) + (r + SUBLANE) * cd * 4 * 4
    return pl.pallas_call(
        functools.partial(_gdn_body, c=c, nc=nc, heads=heads, dk=dk),
        grid=(b, l // r),
        in_specs=[pl.BlockSpec((None, r, cd), lambda i, n: (i, n, 0)),
                  pl.BlockSpec((None, r, hw), lambda i, n: (i, n, 0)),
                  pl.BlockSpec((None, r, LANE), lambda i, n: (i, n, 0)),
                  pl.BlockSpec((None, SUBLANE, cd), lambda i, n: (i, 0, 0)),
                  pl.BlockSpec((SUBLANE, cd), lambda i, n: (0, 0)),
                  pl.BlockSpec((None, heads, dk, dk), lambda i, n: (i, 0, 0, 0)),
                  pl.BlockSpec((1, dk), lambda i, n: (0, 0))],
        out_specs=[pl.BlockSpec((None, r, hw), lambda i, n: (i, n, 0)),
                   pl.BlockSpec((None, heads, dk, dk), lambda i, n: (i, 0, 0, 0))],
        out_shape=[jax.ShapeDtypeStruct((b, l, hw), BF16),
                   jax.ShapeDtypeStruct((b, heads, dk, dk), F32)],
        scratch_shapes=[pltpu.VMEM((r + SUBLANE, cd), F32)],
        compiler_params=_params(("parallel", "arbitrary"), vmem), name="gated_delta_net")(
            qkv, z, gates, cprev, cw, s0, onorm_g.reshape(1, dk))


def _swa_body(*refs, n_seg, cq, nchunk, win, hkv, grp, hd, chunked):
    sink_ref, q_ref = refs[0], refs[1]
    k_refs, v_refs, o_ref = refs[2:2 + n_seg], refs[2 + n_seg:2 + 2 * n_seg], refs[2 + 2 * n_seg]
    kk = jnp.concatenate([r[...] for r in k_refs], axis=0).astype(BF16)
    vv = jnp.concatenate([r[...] for r in v_refs], axis=0).astype(BF16)
    row = lax.broadcasted_iota(jnp.int32, (cq, win), 0)
    col = lax.broadcasted_iota(jnp.int32, (cq, win), 1)
    dist = jnp.abs(row - col + WINDOW).astype(F32)
    hq = hkv * grp
    k_heads = [kk[:, j * hd:(j + 1) * hd] for j in range(hkv)]
    v_heads = [vv[:, j * hd:(j + 1) * hd] for j in range(hkv)]
    chains = []
    for c in range(nchunk):
        dist_c = dist
        if chunked and c * cq < WINDOW:
            first_valid = WINDOW - (pl.program_id(1) * nchunk + c) * cq
            dist_c = jnp.where(col >= first_valid, dist, -NEG)
        for j in range(hkv):
            q_stack = jnp.concatenate(
                [q_ref[c * cq:(c + 1) * cq, (j * grp + g) * hd:(j * grp + g + 1) * hd] for g in range(grp)], axis=0)
            chains.append(dict(c=c, j=j, dist=dist_c, q=q_stack,
                               k=k_heads[j][c * cq:c * cq + win, :], v=v_heads[j][c * cq:c * cq + win, :]))

    def scores(group):
        for ch in group:
            ch["s"] = _dot_nt(ch["q"], ch["k"])

    def softmax_pv(group):
        heads_of = [(ch, g, ch["j"] * grp + g) for ch in group for g in range(grp)]
        slopes = [LOG2E * 2.0 ** (-8.0 * (h + 1) / hq) for _, _, h in heads_of]
        sinks = [LOG2E * sink_ref[h] for _, _, h in heads_of]
        sg = [ch["s"][g * cq:(g + 1) * cq, :] - sl * ch["dist"] for (ch, g, _), sl in zip(heads_of, slopes)]
        m = [jnp.maximum(jnp.max(x, axis=-1, keepdims=True), sk) for x, sk in zip(sg, sinks)]
        p = [jnp.exp2(x - mm) for x, mm in zip(sg, m)]
        l = [jnp.sum(x, axis=-1, keepdims=True) + jnp.exp2(sk - mm) for x, mm, sk in zip(p, m, sinks)]
        pb = [x.astype(BF16) for x in p]
        for ci, ch in enumerate(group):
            ch["l"] = l[ci * grp:(ci + 1) * grp]
            ch["pv"] = _dot(jnp.concatenate(pb[ci * grp:(ci + 1) * grp], axis=0), ch["v"])

    def emit(group):
        for ch in group:
            c, j = ch["c"], ch["j"]
            outs = [(ch["pv"][g * cq:(g + 1) * cq, :] / ch["l"][g]).astype(o_ref.dtype) for g in range(grp)]
            o_ref[c * cq:(c + 1) * cq, j * grp * hd:(j + 1) * grp * hd] = jnp.concatenate(outs, axis=-1)

    gsz = 2
    groups = [chains[i:i + gsz] for i in range(0, len(chains), gsz)]
    scores(groups[0])
    for i, group in enumerate(groups):
        if i + 1 < len(groups):
            scores(groups[i + 1])
        softmax_pv(group)
        if i > 0:
            emit(groups[i - 1])
    emit(groups[-1])


def swa_attention(q, segs, sinks, hkv, grp, hd, chunked, chunks_per_step=4):
    b, l, qw = q.shape
    kw = hkv * hd
    if chunked:
        cq, nchunk, n_seg, win = CHUNK, chunks_per_step, 2, WINDOW + CHUNK
        rows = cq * nchunk
        assert l % rows == 0 and rows % WINDOW == 0
        k, v = segs
        k_specs = [pl.BlockSpec((None, WINDOW, kw), lambda i, n: (i, jnp.maximum(n * (rows // WINDOW) - 1, 0), 0)),
                   pl.BlockSpec((None, rows, kw), lambda i, n: (i, n, 0))]
        k_args, v_args = [k] * n_seg, [v] * n_seg
        key_rows = WINDOW + rows
    else:
        cq, nchunk, n_seg = l, 1, len(segs)
        rows = l
        k_specs = [pl.BlockSpec((None, kk.shape[1], kw), lambda i, n: (i, 0, 0)) for kk, _ in segs]
        k_args, v_args = [kk for kk, _ in segs], [vv for _, vv in segs]
        key_rows = win = sum(kk.shape[1] for kk in k_args)
    vmem = 2 * (2 * rows * qw * 2 + 2 * key_rows * kw * 4) + 3 * nchunk * hkv * grp * cq * 256 * 4
    return pl.pallas_call(
        functools.partial(_swa_body, n_seg=n_seg, cq=cq, nchunk=nchunk, win=win, hkv=hkv, grp=grp, hd=hd,
                          chunked=chunked),
        grid=(b, l // rows),
        in_specs=[pl.BlockSpec(memory_space=pltpu.SMEM),
                  pl.BlockSpec((None, rows, qw), lambda i, n: (i, n, 0))] + k_specs + k_specs,
        out_specs=pl.BlockSpec((None, rows, qw), lambda i, n: (i, n, 0)),
        out_shape=jax.ShapeDtypeStruct((b, l, qw), BF16),
        compiler_params=_params(("parallel", "parallel"), vmem), name="swa_attention")(
            sinks, q, *k_args, *v_args)


def _ab_layer(x, conv_prev, s0, past, w_in, conv_w, a_log, dt_bias, onorm_g, f_bias, w_out, g_mix, dims):
    b, l, d = x.shape
    h_a, dk, h_b, hd_b = dims
    a_qk, a_v, b_w = h_a * dk, h_a * dk, h_b * hd_b
    conv_dim = 2 * a_qk + a_v
    o1 = conv_dim + a_v
    o2 = o1 + 2 * h_a
    small = jnp.concatenate([w_in[:, o1:o2], w_in[:, o2 + 3 * b_w:],
                             jnp.zeros((d, LANE - 2 * h_a - h_b), F32)], axis=1)
    w_pack = jnp.concatenate([w_in[:, :o1], w_in[:, o2:o2 + 3 * b_w], small], axis=1).astype(BF16)
    zeros4 = jnp.zeros((h_a,), F32)
    gp = jnp.zeros((SUBLANE, LANE), F32)
    gp = gp.at[0, :3 * h_a].set(jnp.concatenate([zeros4, dt_bias, f_bias]))
    gp = gp.at[1, h_a:2 * h_a].set(a_log)
    f32o, b16, cache = (F32, 1.0, 1), (BF16, 1.0, 1), (F32, 1.0, h_b)
    groups = ((conv_dim, (f32o,)), (a_v, (f32o,)), (b_w, ((BF16, LOG2E * hd_b ** -0.5, 1),)),
              (b_w, (cache, b16)), (b_w, (cache, b16)), (LANE, (f32o,)))
    qkv, z, qb, kb, kb16, vb, vb16, gates = norm_matmul(x.reshape(b * l, d), g_mix, w_pack, groups, gate_params=gp)
    r3 = lambda a: a.reshape(b, l, a.shape[-1])
    qkv3, gates3 = r3(qkv), r3(gates)
    oa, s_new = gated_delta_net(qkv3, r3(z), gates3, conv_prev, conv_w, s0, onorm_g, h_a, dk)
    conv_new = qkv3[:, l - (CONV_W - 1):, :]
    logf = gates3[:, :, 2 * h_a:2 * h_a + h_b]
    lf_rows = jnp.pad(logf.transpose(0, 2, 1).reshape(b * h_b, l), ((0, 0), (0, -l % LANE)))
    f_new = cumsum_rows(lf_rows, LOG2E)[:, :l].reshape(b, h_b, l)
    fox_past = None
    if past is not None:
        pk, pv, plf = past
        p_len = pk.shape[1]
        f_past = cumsum_rows(plf.transpose(0, 2, 1).reshape(b * h_b, p_len), LOG2E, minus_total=True)
        fox_past = (pk.reshape(b, p_len * h_b, hd_b), pv.reshape(b, p_len * h_b, hd_b), f_past.reshape(b, h_b, p_len))
    ob = fox_attention(r3(qb), r3(kb16), r3(vb16), f_new, h_b, past=fox_past)
    w_out16 = w_out.astype(BF16)
    mixed = ([oa, ob], [w_out16[:a_v], w_out16[a_v:]])
    return (mixed, s_new, conv_new, kb.reshape(b, l, h_b, hd_b), vb.reshape(b, l, h_b, hd_b), logf)


def _c_layer(x, past, w_in, sinks, w_out, g_mix, dims):
    b, l, d = x.shape
    hq, hkv, hd = dims
    grp = hq // hkv
    f32o = (F32, 1.0, 1)
    q, k, v = norm_matmul(x.reshape(b * l, d), g_mix, w_in.astype(BF16),
                          ((hq * hd, ((BF16, LOG2E * hd ** -0.5, 1),)), (hkv * hd, (f32o,)), (hkv * hd, (f32o,))))
    q3, k3, v3 = q.reshape(b, l, hq * hd), k.reshape(b, l, hkv * hd), v.reshape(b, l, hkv * hd)
    if past is None:
        o = swa_attention(q3, (k3, v3), sinks, hkv, grp, hd, chunked=True)
        new_k, new_v = k3[:, l - WINDOW:], v3[:, l - WINDOW:]
    else:
        pk, pv = past
        w_len = pk.shape[1]
        o = swa_attention(q3, ((pk.reshape(b, w_len, hkv * hd), pv.reshape(b, w_len, hkv * hd)), (k3, v3)),
                          sinks, hkv, grp, hd, chunked=False)
        new_k, new_v = k3, v3
    return ([o], [w_out.astype(BF16)]), new_k.reshape(b, -1, hkv, hd), new_v.reshape(b, -1, hkv, hd)


def kernel(x_prompt, mem_prompt, x_sample, state_gdn, state_gdn_conv, cache_fox_k, cache_fox_v, cache_fox_logf, cache_swa_k, cache_swa_v, cache_mem_k, cache_mem_v, g_mix, w_in_ab, conv_w_a, a_log_a, dt_bias_a, onorm_g_a, f_bias_b, w_out_ab, w_in_c, sinks_c, w_out_c, g_mem, w_xkv, g_cross, w_xq, w_xo, g_mlp, w_up, w_down, g_final):
    depth = g_mix.shape[0]
    bp, lp, d = x_prompt.shape
    bs, ls, _ = x_sample.shape
    h_a, dk = state_gdn.shape[2], state_gdn.shape[3]
    h_b, hd_b = cache_fox_k.shape[3], cache_fox_k.shape[4]
    hkv, hd_c = cache_swa_k.shape[3], cache_swa_k.shape[4]
    hq = sinks_c.shape[1]
    m_len, h_x, hd_x = cache_mem_k.shape[2], cache_mem_k.shape[3], cache_mem_k.shape[4]
    xw = h_x * hd_x
    conv_dim = state_gdn_conv.shape[-1]

    xp, xs = x_prompt, x_sample
    p_out = {k: [] for k in ("s", "c", "fk", "fv", "fl", "wk", "wv", "mk", "mv")}
    s_out = {k: [] for k in ("s", "c", "fk", "fv", "fl", "wk", "wv")}
    for layer in range(depth):
        i = layer // 2
        if layer % 2 == 0:
            wts = (w_in_ab[i], conv_w_a[i], a_log_a[i], dt_bias_a[i], onorm_g_a[i], f_bias_b[i], w_out_ab[i],
                   g_mix[layer], (h_a, dk, h_b, hd_b))
            mix_p, st, cv, fk, fv, fl = _ab_layer(xp, jnp.zeros((bp, CONV_W - 1, conv_dim), F32),
                                                  jnp.zeros((bp, h_a, dk, dk), F32), None, *wts)
            for key, val in zip(("s", "c", "fk", "fv", "fl"), (st, cv, fk, fv, fl)):
                p_out[key].append(val)
            mix_s, st, cv, fk, fv, fl = _ab_layer(xs, state_gdn_conv[i], state_gdn[i],
                                                  (cache_fox_k[i], cache_fox_v[i], cache_fox_logf[i]), *wts)
            for key, val in zip(("s", "c", "fk", "fv", "fl"), (st, cv, fk, fv, fl)):
                s_out[key].append(val)
        else:
            wts = (w_in_c[i], sinks_c[i], w_out_c[i], g_mix[layer], (hq, hkv, hd_c))
            mix_p, wk, wv = _c_layer(xp, None, *wts)
            p_out["wk"].append(wk); p_out["wv"].append(wv)
            mix_s, wk, wv = _c_layer(xs, (cache_swa_k[i], cache_swa_v[i]), *wts)
            s_out["wk"].append(wk); s_out["wv"].append(wv)
        mem_outs = ((F32, 1.0, h_x),)
        mk, mv = norm_matmul(mem_prompt.reshape(bp * m_len, d), g_mem[layer], w_xkv[layer].astype(BF16),
                             ((xw, mem_outs), (xw, mem_outs)))
        p_out["mk"].append(mk.reshape(bp, m_len, h_x, hd_x)); p_out["mv"].append(mv.reshape(bp, m_len, h_x, hd_x))
        post_w = (g_cross[layer], w_xq[layer].astype(BF16))
        mlp_w = (w_xo[layer].astype(BF16), g_mlp[layer], w_up[layer].astype(BF16), w_down[layer].astype(BF16), h_x,
                 g_final if layer == depth - 1 else None)
        xp = post_mixer(xp, *mix_p, *post_w, mk.reshape(bp, m_len * h_x, hd_x), mv.reshape(bp, m_len * h_x, hd_x),
                        *mlp_w)
        xs = post_mixer(xs, *mix_s, *post_w, cache_mem_k[layer].reshape(bs, m_len * h_x, hd_x),
                        cache_mem_v[layer].reshape(bs, m_len * h_x, hd_x), *mlp_w)
    st = jnp.stack
    return (xp, xs,
            st(p_out["s"]), st(p_out["c"]), st(p_out["fk"]), st(p_out["fv"]), st(p_out["fl"]),
            st(p_out["wk"]), st(p_out["wv"]), st(p_out["mk"]), st(p_out["mv"]),
            st(s_out["s"]), st(s_out["c"]), st(s_out["fk"]), st(s_out["fv"]), st(s_out["fl"]),
            st(s_out["wk"]), st(s_out["wv"]))
```

```python
import functools
import math

import jax
import jax.numpy as jnp
from jax import lax
from jax.experimental import pallas as pl
from jax.experimental.pallas import tpu as pltpu

F32 = jnp.float32
BF16 = jnp.bfloat16
LOG2E = math.log2(math.e)
EPS = 1e-6
NEG = -1e30

CHUNK = 64
CONV_W = 4
WINDOW = 128
LANE = 128
SUBLANE = 8
VMEM_BYTES = 64 * 1024 * 1024


def _params(semantics, vmem_bytes):
    limit = int(min(max(vmem_bytes * 1.25, 16 * 1024 * 1024), VMEM_BYTES - 8 * 1024 * 1024))
    return pltpu.CompilerParams(dimension_semantics=semantics, vmem_limit_bytes=limit)


def _dot(a, b):
    return jnp.dot(a, b, preferred_element_type=F32)


def _dot_nt(a, b):
    return lax.dot_general(a, b, (((1,), (1,)), ((), ())), preferred_element_type=F32)


def _dot_tn(a, b):
    return lax.dot_general(a, b, (((0,), (0,)), ((), ())), preferred_element_type=F32)


def _rms(x, g):
    return x * lax.rsqrt(jnp.mean(x * x, axis=-1, keepdims=True) + EPS) * g


def _sigmoid(x):
    return 1.0 / (1.0 + jnp.exp(-x))


def _softplus(x):
    return jnp.maximum(x, 0.0) + jnp.log(1.0 + jnp.exp(-jnp.abs(x)))


def _token_tile(t, cap):
    tm = min(t, cap)
    assert t % tm == 0
    return tm


def _conv_silu_norm(r, c, cw, conv, cprev_ref, cw_ref, tail_ref, ybuf_ref):
    segs, tiles_per_seq, hd, n_norm, n_scaled, scale = conv
    ts = r.shape[0] // segs
    pad = SUBLANE
    cols = slice(c, c + cw)
    first = pl.program_id(0) % tiles_per_seq == 0
    out = []
    for s in range(segs):
        base = s * (ts + pad)

        @pl.when(first)
        def _():
            ybuf_ref[base:base + pad, cols] = cprev_ref[s, :, cols]

        rs = r[s * ts:(s + 1) * ts, :]
        ybuf_ref[base + pad:base + pad + ts, cols] = rs
        acc = rs * cw_ref[CONV_W - 1:CONV_W, cols]
        for k in range(1, CONV_W):
            acc = acc + ybuf_ref[base + pad - k:base + pad - k + ts, cols] * cw_ref[CONV_W - 1 - k:CONV_W - k, cols]
        last = ybuf_ref[base + ts:base + ts + pad, cols]
        tail_ref[s, :, cols] = last
        ybuf_ref[base:base + pad, cols] = last
        act = acc * _sigmoid(acc)
        heads = []
        for h in range(cw // hd):
            xh = act[:, h * hd:(h + 1) * hd]
            gh = (c + h * hd) // hd
            if gh < n_norm:
                xh = xh * lax.rsqrt(jnp.sum(xh * xh, axis=-1, keepdims=True) + EPS)
                if gh < n_scaled:
                    xh = xh * scale
            heads.append(xh)
        out.append(jnp.concatenate(heads, axis=-1))
    return jnp.concatenate(out, axis=0)


def _norm_matmul_body(*refs, groups, gated, conv, col_chunk):
    x_ref, g_ref, w_ref = refs[:3]
    refs = refs[3:]
    if gated:
        gp_ref, refs = refs[0], refs[1:]
    if conv:
        cprev_ref, cw_ref, refs = refs[0], refs[1], refs[2:]
        tail_ref, ybuf_ref, refs = refs[-2], refs[-1], refs[:-2]
    x = x_ref[...]
    yb = _rms(x, g_ref[...]).astype(BF16)
    chunks, off, k = [], 0, 0
    for gi, (n, outs) in enumerate(groups):
        chunks += [(gi, n, outs, k, off, c, min(col_chunk, n - c)) for c in range(0, n, col_chunk)]
        off += n
        k += len(outs)

    def project(chunk):
        _, _, _, _, off, c, cw = chunk
        return _dot(yb, w_ref[:, off + c:off + c + cw])

    r_next = project(chunks[0])
    for idx, (gi, n, outs, k, off, c, cw) in enumerate(chunks):
        r = r_next
        if idx + 1 < len(chunks):
            r_next = project(chunks[idx + 1])
        if conv and gi == 0:
            r = _conv_silu_norm(r, c, cw, conv, cprev_ref, cw_ref, tail_ref, ybuf_ref)
        if gated and gi == len(groups) - 1:
            rb = r + gp_ref[0:1, :]
            lane = lax.broadcasted_iota(jnp.int32, rb.shape, 1)
            e = jnp.exp(-jnp.abs(rb))
            l1p = jnp.log(1.0 + e)
            sig = _sigmoid(rb)
            gdec = -jnp.exp(gp_ref[1:2, :]) * (jnp.maximum(rb, 0.0) + l1p)
            logsig = jnp.minimum(rb, 0.0) - l1p
            r = jnp.where(lane < 4, sig, jnp.where(lane < 8, gdec, jnp.where(lane < 12, logsig, 0.0)))
        for j, (dtype, scale, heads) in enumerate(outs):
            o_ref = refs[k + j]
            rr = (r if scale == 1.0 else r * scale).astype(dtype)
            if heads == 1:
                o_ref[:, c:c + cw] = rr
            else:
                hw = n // heads
                tm = x_ref.shape[0]
                for h in range(c // hw, (c + cw) // hw):
                    o_ref[pl.ds(h, tm, stride=heads), :] = rr[:, h * hw - c:(h + 1) * hw - c]


def norm_matmul(x, g, w, groups, gate_params=None, conv=None, tile_cap=512):
    t, d = x.shape
    tm = _token_tile(t, tile_cap)
    gated = gate_params is not None
    in_specs = [pl.BlockSpec((tm, d), lambda i: (i, 0)),
                pl.BlockSpec((1, d), lambda i: (0, 0)),
                pl.BlockSpec(w.shape, lambda i: (0, 0))]
    args = [x, g.reshape(1, d), w]
    if gated:
        in_specs.append(pl.BlockSpec(gate_params.shape, lambda i: (0, 0)))
        args.append(gate_params)
    out_shape, out_specs, out_bytes = [], [], 0
    for n, outs in groups:
        for dtype, _, heads in outs:
            out_shape.append(jax.ShapeDtypeStruct((t * heads, n // heads), dtype))
            out_specs.append(pl.BlockSpec((tm * heads, n // heads), lambda i: (i, 0)))
            out_bytes += tm * n * jnp.dtype(dtype).itemsize
    vmem = 2 * (tm * d * 4 + w.size * 2 + out_bytes) + tm * d * 8
    scratch, conv_static, semantics = [], None, ("parallel",)
    if conv is not None:
        conv_prev, taps, seq_len, hd, n_norm, n_scaled, scale = conv
        n0 = groups[0][0]
        segs, tiles_per_seq = max(1, tm // seq_len), max(1, seq_len // tm)
        assert (tm % seq_len == 0 or seq_len % tm == 0) and (tm // segs) % SUBLANE == 0
        cprev = jnp.pad(conv_prev, ((0, 0), (SUBLANE - (CONV_W - 1), 0), (0, 0)))
        in_specs += [pl.BlockSpec((segs, SUBLANE, n0), lambda i: (i // tiles_per_seq, 0, 0)),
                     pl.BlockSpec((SUBLANE, n0), lambda i: (0, 0))]
        args += [cprev, jnp.pad(taps, ((0, SUBLANE - CONV_W), (0, 0)))]
        out_shape.append(jax.ShapeDtypeStruct((conv_prev.shape[0], SUBLANE, n0), F32))
        out_specs.append(pl.BlockSpec((segs, SUBLANE, n0), lambda i: (i // tiles_per_seq, 0, 0)))
        scratch = [pltpu.VMEM((tm + segs * SUBLANE, n0), F32)]
        conv_static = (segs, tiles_per_seq, hd, n_norm, n_scaled, scale)
        semantics = ("arbitrary",)
        vmem += (tm + segs * SUBLANE) * n0 * 4 + 6 * segs * SUBLANE * n0 * 4
    return pl.pallas_call(
        functools.partial(_norm_matmul_body, groups=groups, gated=gated, conv=conv_static, col_chunk=512),
        grid=(t // tm,), in_specs=in_specs, out_specs=out_specs, out_shape=out_shape, scratch_shapes=scratch,
        compiler_params=_params(semantics, vmem), name="norm_matmul")(*args)


def _post_body(*refs, n_in, heads, hd, ff_chunk, final_norm, col_chunk):
    x_ref = refs[0]
    a_refs, w_refs = refs[1:1 + n_in], refs[1 + n_in:1 + 2 * n_in]
    gc_ref, wq_ref, mk_ref, mv_ref, wo_ref, gm_ref, wu_ref, wd_ref = refs[1 + 2 * n_in:9 + 2 * n_in]
    gf_ref = refs[9 + 2 * n_in] if final_norm else None
    o_ref = refs[-1]
    bt, tm, d = x_ref.shape
    rows = bt * tm
    mem_len = mk_ref.shape[1] // heads
    x0 = x_ref[...].reshape(rows, d)
    a_vals = [a[...].reshape(rows, a.shape[-1]).astype(BF16) for a in a_refs]
    x1 = jnp.concatenate(
        [x0[:, c:c + col_chunk] + sum(_dot(a, w[:, c:c + col_chunk]) for a, w in zip(a_vals, w_refs))
         for c in range(0, d, col_chunk)], axis=-1)
    hb = _rms(x1, gc_ref[...]).astype(BF16)
    q = (_dot(hb, wq_ref[...]) * (hd ** -0.5)).astype(BF16)
    att = []
    for bi in range(bt):
        outs = []
        for h in range(heads):
            mem_rows = pl.ds(h, mem_len, stride=heads)
            s = _dot_nt(q[bi * tm:(bi + 1) * tm, h * hd:(h + 1) * hd], mk_ref[bi, mem_rows, :].astype(BF16))
            m = jnp.max(s, axis=-1, keepdims=True)
            p = jnp.exp(s - m)
            l = jnp.sum(p, axis=-1, keepdims=True)
            outs.append((_dot(p.astype(BF16), mv_ref[bi, mem_rows, :].astype(BF16)) / l).astype(BF16))
        att.append(jnp.concatenate(outs, axis=-1))
    x2 = x1 + _dot(jnp.concatenate(att, axis=0), wo_ref[...])
    hb = _rms(x2, gm_ref[...]).astype(BF16)
    acc = x2
    for c in range(0, wu_ref.shape[1], ff_chunk):
        u = jnp.maximum(_dot(hb, wu_ref[:, c:c + ff_chunk]), 0.0)
        acc = acc + _dot((u * u).astype(BF16), wd_ref[c:c + ff_chunk, :])
    if final_norm:
        acc = _rms(acc, gf_ref[...])
    o_ref[...] = acc.reshape(bt, tm, d)


def post_mixer(x, a_list, w_list, g_cross, wq, mk, mv, wo, g_mlp, w_up, w_down, heads, g_final=None, tile_cap=512):
    b, l, d = x.shape
    mh, hd = mk.shape[1], mk.shape[2]
    m, hw = mh // heads, hd * heads
    tm = _token_tile(l, tile_cap)
    bt = math.gcd(b, max(1, tile_cap // tm))
    rows = bt * tm
    n_in = len(a_list)
    final_norm = g_final is not None
    once = pl.Buffered(1)

    def const(shape):
        return pl.BlockSpec(shape, lambda i, j: (0,) * len(shape), pipeline_mode=once)

    in_specs = [pl.BlockSpec((bt, tm, d), lambda i, j: (i, j, 0))]
    in_specs += [pl.BlockSpec((bt, tm, a.shape[2]), lambda i, j: (i, j, 0)) for a in a_list]
    in_specs += [const(w.shape) for w in w_list]
    in_specs += [const((1, d)), const(wq.shape),
                 pl.BlockSpec((bt, mh, hd), lambda i, j: (i, 0, 0)),
                 pl.BlockSpec((bt, mh, hd), lambda i, j: (i, 0, 0)),
                 const(wo.shape), const((1, d)), const(w_up.shape), const(w_down.shape)]
    args = [x, *a_list, *w_list, g_cross.reshape(1, d), wq, mk, mv, wo, g_mlp.reshape(1, d), w_up, w_down]
    if final_norm:
        in_specs.append(const((1, d)))
        args.append(g_final.reshape(1, d))
    weights = sum(w.size for w in w_list) + wq.size + wo.size + w_up.size + w_down.size
    vmem = (weights * 2 + 2 * (2 * rows * d * 4 + 2 * bt * m * hw * 4
                               + sum(rows * a.shape[2] * a.dtype.itemsize for a in a_list))
            + rows * (d * 16 + hw * 8 + m * 8 + 512 * 8))
    return pl.pallas_call(
        functools.partial(_post_body, n_in=n_in, heads=heads, hd=hw // heads, ff_chunk=512,
                          final_norm=final_norm, col_chunk=512),
        grid=(b // bt, l // tm), in_specs=in_specs,
        out_specs=pl.BlockSpec((bt, tm, d), lambda i, j: (i, j, 0)),
        out_shape=jax.ShapeDtypeStruct((b, l, d), F32),
        compiler_params=_params(("parallel", "parallel"), vmem), name="post_mixer")(*args)


def _cumsum_body(x_ref, o_ref, *, minus_total, scale):
    r, l = x_ref.shape
    row = lax.broadcasted_iota(jnp.int32, (LANE, LANE), 0)
    col = lax.broadcasted_iota(jnp.int32, (LANE, LANE), 1)
    upper = (row <= col).astype(BF16)
    carry = jnp.zeros((r, 1), F32)
    for c in range(0, l, LANE):
        x = x_ref[:, c:c + LANE]
        x1 = x.astype(BF16)
        res = x - x1.astype(F32)
        x2 = res.astype(BF16)
        x3 = (res - x2.astype(F32)).astype(BF16)
        y3 = _dot(jnp.concatenate([x1, x2, x3], axis=0), upper)
        y = y3[:r] + y3[r:2 * r] + y3[2 * r:] + carry
        o_ref[:, c:c + LANE] = y
        carry = y[:, LANE - 1:LANE]
    total = carry if minus_total else 0.0
    o_ref[...] = (o_ref[...] - total) * scale


def cumsum_rows(x, scale, minus_total=False):
    r, l = x.shape
    assert l % LANE == 0 and r % SUBLANE == 0
    return pl.pallas_call(
        functools.partial(_cumsum_body, minus_total=minus_total, scale=scale),
        grid=(1,), in_specs=[pl.BlockSpec((r, l), lambda i: (0, 0))],
        out_specs=pl.BlockSpec((r, l), lambda i: (0, 0)),
        out_shape=jax.ShapeDtypeStruct((r, l), F32),
        compiler_params=_params(("arbitrary",), 4 * r * l * 4), name="cumsum_rows")(x)


def _fox_body(*refs, tq, tkp, n_past, nsub, heads):
    if n_past:
        q_ref, k_ref, v_ref, fk_ref, kp_ref, vp_ref, fp_ref, o_ref = refs
    else:
        q_ref, k_ref, v_ref, fk_ref, o_ref = refs
    i = pl.program_id(2)
    sub = tq // nsub
    hd = q_ref.shape[-1]
    qs = [q_ref[c * sub:(c + 1) * sub, :] for c in range(nsub)]

    def update(carry, ss, vbs):
        ms, ls, accs = carry[0::3], carry[1::3], carry[2::3]
        m_new = [jnp.maximum(m, jnp.max(s, axis=-1, keepdims=True)) for m, s in zip(ms, ss)]
        alpha = [jnp.exp2(m - mn) for m, mn in zip(ms, m_new)]
        ps = [jnp.exp2(s - mn) for s, mn in zip(ss, m_new)]
        ls = [a * l + jnp.sum(p, axis=-1, keepdims=True) for a, l, p in zip(alpha, ls, ps)]
        pv = [_dot(p.astype(BF16), vb) for p, vb in zip(ps, vbs)]
        accs = [a * acc + x for a, acc, x in zip(alpha, accs, pv)]
        out = []
        for c in range(nsub):
            out += [m_new[c], ls[c], accs[c]]
        return tuple(out)

    carry = (jnp.full((sub, 1), NEG, F32), jnp.zeros((sub, 1), F32), jnp.zeros((sub, hd), F32)) * nsub

    if n_past:
        def past_step(j, carry):
            rows = pl.ds(j * tkp * heads + pl.program_id(1), tkp, stride=heads)
            kb = kp_ref[rows, :].astype(BF16)
            vb = vp_ref[rows, :].astype(BF16)
            fk = fp_ref[pl.ds(j, 1), :]
            return update(carry, [_dot_nt(q, kb) - fk for q in qs], [vb] * nsub)
        carry = lax.fori_loop(0, n_past, past_step, carry)

    def self_step(j, carry):
        start = pl.multiple_of(j * sub, sub)
        kb = k_ref[pl.ds(start, sub), :]
        vb = v_ref[pl.ds(start, sub), :]
        fk = fk_ref[pl.ds(j, 1), :][:, :sub]
        return update(carry, [_dot_nt(q, kb) - fk for q in qs], [vb] * nsub)
    carry = lax.fori_loop(0, i * nsub, self_step, carry)

    start = pl.multiple_of(i * tq, tq)
    fks = [fk_ref[pl.ds(i * nsub + c, 1), :] for c in range(nsub)]
    ss, vbs = [], []
    for c in range(nsub):
        w = (c + 1) * sub
        fk = fks[0][:, :sub] if c == 0 else jnp.concatenate(fks[:c + 1], axis=1)
        s = _dot_nt(qs[c], k_ref[pl.ds(start, w), :]) - fk
        row = lax.broadcasted_iota(jnp.int32, (sub, w), 0) + c * sub
        col = lax.broadcasted_iota(jnp.int32, (sub, w), 1)
        ss.append(jnp.where(col <= row, s, NEG))
        vbs.append(v_ref[pl.ds(start, w), :])
    carry = update(carry, ss, vbs)
    for c in range(nsub):
        o_ref[c * sub:(c + 1) * sub, :] = (carry[3 * c + 2] / carry[3 * c + 1]).astype(o_ref.dtype)


def fox_attention(q, k, v, f_new, heads, past=None, tq_cap=1024, sub_cap=256, score_tile=64 * 1024):
    b, l, hw = q.shape
    hd = hw // heads
    tq = min(l, tq_cap)
    nq = l // tq
    nsub = max(1, tq // sub_cap)
    p_len = 0 if past is None else past[0].shape[1] // heads
    tkp = min(p_len, max(sub_cap, score_tile // tq))
    sub = tq // nsub
    nk = l // sub
    wk = -(-sub // LANE) * LANE
    if wk != sub:
        assert nk == 1
        f_new = jnp.pad(f_new, ((0, 0), (0, 0), (0, wk - sub)))
    in_specs = [pl.BlockSpec((None, tq, hd), lambda bi, h, i: (bi, i, h)),
                pl.BlockSpec((None, l, hd), lambda bi, h, i: (bi, 0, h)),
                pl.BlockSpec((None, l, hd), lambda bi, h, i: (bi, 0, h)),
                pl.BlockSpec((None, None, nk, wk), lambda bi, h, i: (bi, h, 0, 0))]
    args = [q, k, v, f_new.reshape(b, heads, nk, wk)]
    n_past = 0
    vmem = 2 * (2 * tq * hd * 2 + 2 * l * hd * 2) + 8 * tq * max(tq, tkp) * 4
    if past is not None:
        kp, vp, fp = past
        n_past = p_len // tkp
        in_specs += [pl.BlockSpec((None, p_len * heads, hd), lambda bi, h, i: (bi, 0, 0)),
                     pl.BlockSpec((None, p_len * heads, hd), lambda bi, h, i: (bi, 0, 0)),
                     pl.BlockSpec((None, None, n_past, tkp), lambda bi, h, i: (bi, h, 0, 0))]
        args += [kp, vp, fp.reshape(b, heads, n_past, tkp)]
        vmem += 2 * 2 * p_len * heads * hd * 4
    return pl.pallas_call(
        functools.partial(_fox_body, tq=tq, tkp=tkp, n_past=n_past, nsub=nsub, heads=heads),
        grid=(b, heads, nq), in_specs=in_specs,
        out_specs=pl.BlockSpec((None, tq, hd), lambda bi, h, i: (bi, i, h)),
        out_shape=jax.ShapeDtypeStruct((b, l, hw), BF16),
        compiler_params=_params(("parallel", "parallel", "arbitrary"), vmem), name="fox_attention")(*args)


def _gdn_body(qkv_ref, z_ref, gates_ref, s0_ref, og_ref, o_ref, st_ref, *, c, nc, heads, dk):
    hw = heads * dk

    @pl.when(pl.program_id(1) == 0)
    def _():
        st_ref[...] = s0_ref[...]

    row = lax.broadcasted_iota(jnp.int32, (c, c), 0)
    col = lax.broadcasted_iota(jnp.int32, (c, c), 1)
    incl = row >= col
    strict = row > col
    tril = incl.astype(BF16)
    n_double = int(math.log2(c)) - 1
    og = og_ref[...]

    chains = []
    for ci in range(nc):
        r0 = ci * c
        gs = gates_ref[r0:r0 + c, :]
        g1 = gs.astype(BF16)
        res = gs - g1.astype(F32)
        g2 = res.astype(BF16)
        g3 = (res - g2.astype(F32)).astype(BF16)
        cs3 = _dot(tril, jnp.concatenate([g1, g2, g3], axis=-1))
        cs = cs3[:, :LANE] + cs3[:, LANE:2 * LANE] + cs3[:, 2 * LANE:]
        cs_t = jnp.concatenate([cs, jnp.zeros((LANE - c, LANE), F32)], axis=0).T
        for h in range(heads):
            qh = qkv_ref[r0:r0 + c, h * dk:(h + 1) * dk]
            kh = qkv_ref[r0:r0 + c, hw + h * dk:hw + (h + 1) * dk]
            vh = qkv_ref[r0:r0 + c, 2 * hw + h * dk:2 * hw + (h + 1) * dk]
            beta = gs[:, h:h + 1]
            gc = cs[:, heads + h:heads + h + 1]
            gc_row = cs_t[heads + h:heads + h + 1, :c]
            kb = kh * beta
            egc = jnp.exp(gc)
            g_last = gc[c - 1:c, :]
            chains.append(dict(
                decay=jnp.exp(jnp.where(incl, gc - gc_row, NEG)),
                kq=jnp.concatenate([kb, qh], axis=0).astype(BF16), khb=kh.astype(BF16),
                rhs=jnp.concatenate([vh * beta, kb * egc], axis=-1), qg=qh * egc,
                kd=(kh * jnp.exp(g_last - gc)).astype(BF16), ds=jnp.exp(g_last)))

    for ch in chains:
        ch["a1"] = _dot_nt(ch["kq"], ch["khb"])
    for ch in chains:
        lmat = jnp.where(strict, ch["a1"][:c] * ch["decay"], 0.0)
        ch["attn"] = jnp.where(incl, ch["a1"][c:] * ch["decay"], 0.0).astype(BF16)
        ch["toff"] = -lmat
        ch["lb"] = lmat.astype(BF16)
    for ch in chains:
        ch["pw"] = _dot(ch["lb"], ch["lb"])
    for step in range(n_double):
        last = step == n_double - 1
        for ch in chains:
            pwb = ch["pw"].astype(BF16)
            lhs = ch["toff"].astype(BF16) if last else jnp.concatenate([ch["toff"].astype(BF16), pwb], axis=0)
            ch["tp"] = _dot(lhs, pwb)
        for ch in chains:
            ch["toff"] = ch["toff"] + ch["pw"] + ch["tp"][:c]
            if not last:
                ch["pw"] = ch["tp"][c:]
    for ch in chains:
        ch["uw"] = _dot(ch["toff"].astype(BF16), ch["rhs"].astype(BF16))
    for ch in chains:
        uw = ch["rhs"] + ch["uw"]
        ch["u"] = uw[:, :dk]
        ch["wq"] = jnp.concatenate([uw[:, dk:], ch["qg"]], axis=0).astype(BF16)

    states = [st_ref[h] for h in range(heads)]
    for ci in range(nc):
        group = chains[ci * heads:(ci + 1) * heads]
        t1 = [_dot(ch["wq"], s.astype(BF16)) for ch, s in zip(group, states)]
        vnb = [(ch["u"] - t[:c]).astype(BF16) for ch, t in zip(group, t1)]
        o2 = [_dot(ch["attn"], v) for ch, v in zip(group, vnb)]
        sd = [_dot_tn(ch["kd"], v) for ch, v in zip(group, vnb)]
        states = [s * ch["ds"] + d for ch, s, d in zip(group, states, sd)]
        for h in range(heads):
            o = t1[h][c:] + o2[h]
            zh = z_ref[ci * c:(ci + 1) * c, h * dk:(h + 1) * dk]
            o_ref[ci * c:(ci + 1) * c, h * dk:(h + 1) * dk] = (_rms(o, og) * (zh * _sigmoid(zh))).astype(o_ref.dtype)
    for h in range(heads):
        st_ref[h] = states[h]


def gated_delta_net(qkv, z, gates, s0, onorm_g, heads, dk, chunks_per_step=4):
    b, l, cd = qkv.shape
    c = min(CHUNK, l)
    nc = min(chunks_per_step, l // c)
    r = c * nc
    hw = heads * dk
    vmem = 2 * (r * cd * 4 + r * hw * 4 + r * LANE * 4 + 2 * heads * dk * dk * 4 + r * hw * 2) + r * cd * 4 * 4
    return pl.pallas_call(
        functools.partial(_gdn_body, c=c, nc=nc, heads=heads, dk=dk),
        grid=(b, l // r),
        in_specs=[pl.BlockSpec((None, r, cd), lambda i, n: (i, n, 0)),
                  pl.BlockSpec((None, r, hw), lambda i, n: (i, n, 0)),
                  pl.BlockSpec((None, r, LANE), lambda i, n: (i, n, 0)),
                  pl.BlockSpec((None, heads, dk, dk), lambda i, n: (i, 0, 0, 0)),
                  pl.BlockSpec((1, dk), lambda i, n: (0, 0))],
        out_specs=[pl.BlockSpec((None, r, hw), lambda i, n: (i, n, 0)),
                   pl.BlockSpec((None, heads, dk, dk), lambda i, n: (i, 0, 0, 0))],
        out_shape=[jax.ShapeDtypeStruct((b, l, hw), BF16),
                   jax.ShapeDtypeStruct((b, heads, dk, dk), F32)],
        compiler_params=_params(("parallel", "arbitrary"), vmem), name="gated_delta_net")(
            qkv, z, gates, s0, onorm_g.reshape(1, dk))


def _swa_body(*refs, n_seg, cq, nchunk, win, hkv, grp, hd, chunked):
    sink_ref, q_ref = refs[0], refs[1]
    k_refs, v_refs, o_ref = refs[2:2 + n_seg], refs[2 + n_seg:2 + 2 * n_seg], refs[2 + 2 * n_seg]
    kk = jnp.concatenate([r[...] for r in k_refs], axis=0).astype(BF16)
    vv = jnp.concatenate([r[...] for r in v_refs], axis=0).astype(BF16)
    row = lax.broadcasted_iota(jnp.int32, (cq, win), 0)
    col = lax.broadcasted_iota(jnp.int32, (cq, win), 1)
    dist = jnp.abs(row - col + WINDOW).astype(F32)
    hq = hkv * grp
    k_heads = [kk[:, j * hd:(j + 1) * hd] for j in range(hkv)]
    v_heads = [vv[:, j * hd:(j + 1) * hd] for j in range(hkv)]
    chains = []
    for c in range(nchunk):
        dist_c = dist
        if chunked and c * cq < WINDOW:
            first_valid = WINDOW - (pl.program_id(1) * nchunk + c) * cq
            dist_c = jnp.where(col >= first_valid, dist, -NEG)
        for j in range(hkv):
            q_stack = jnp.concatenate(
                [q_ref[c * cq:(c + 1) * cq, (j * grp + g) * hd:(j * grp + g + 1) * hd] for g in range(grp)], axis=0)
            chains.append(dict(c=c, j=j, dist=dist_c, q=q_stack,
                               k=k_heads[j][c * cq:c * cq + win, :], v=v_heads[j][c * cq:c * cq + win, :]))

    def scores(group):
        for ch in group:
            ch["s"] = _dot_nt(ch["q"], ch["k"])

    def softmax_pv(group):
        heads_of = [(ch, g, ch["j"] * grp + g) for ch in group for g in range(grp)]
        slopes = [LOG2E * 2.0 ** (-8.0 * (h + 1) / hq) for _, _, h in heads_of]
        sinks = [LOG2E * sink_ref[h] for _, _, h in heads_of]
        sg = [ch["s"][g * cq:(g + 1) * cq, :] - sl * ch["dist"] for (ch, g, _), sl in zip(heads_of, slopes)]
        m = [jnp.maximum(jnp.max(x, axis=-1, keepdims=True), sk) for x, sk in zip(sg, sinks)]
        p = [jnp.exp2(x - mm) for x, mm in zip(sg, m)]
        l = [jnp.sum(x, axis=-1, keepdims=True) + jnp.exp2(sk - mm) for x, mm, sk in zip(p, m, sinks)]
        pb = [x.astype(BF16) for x in p]
        for ci, ch in enumerate(group):
            ch["l"] = l[ci * grp:(ci + 1) * grp]
            ch["pv"] = _dot(jnp.concatenate(pb[ci * grp:(ci + 1) * grp], axis=0), ch["v"])

    def emit(group):
        for ch in group:
            c, j = ch["c"], ch["j"]
            outs = [(ch["pv"][g * cq:(g + 1) * cq, :] / ch["l"][g]).astype(o_ref.dtype) for g in range(grp)]
            o_ref[c * cq:(c + 1) * cq, j * grp * hd:(j + 1) * grp * hd] = jnp.concatenate(outs, axis=-1)

    gsz = 2
    groups = [chains[i:i + gsz] for i in range(0, len(chains), gsz)]
    scores(groups[0])
    for i, group in enumerate(groups):
        if i + 1 < len(groups):
            scores(groups[i + 1])
        softmax_pv(group)
        if i > 0:
            emit(groups[i - 1])
    emit(groups[-1])


def swa_attention(q, segs, sinks, hkv, grp, hd, chunked, chunks_per_step=4):
    b, l, qw = q.shape
    kw = hkv * hd
    if chunked:
        cq, nchunk, n_seg, win = CHUNK, chunks_per_step, 2, WINDOW + CHUNK
        rows = cq * nchunk
        assert l % rows == 0 and rows % WINDOW == 0
        k, v = segs
        k_specs = [pl.BlockSpec((None, WINDOW, kw), lambda i, n: (i, jnp.maximum(n * (rows // WINDOW) - 1, 0), 0)),
                   pl.BlockSpec((None, rows, kw), lambda i, n: (i, n, 0))]
        k_args, v_args = [k] * n_seg, [v] * n_seg
        key_rows = WINDOW + rows
    else:
        cq, nchunk, n_seg = l, 1, len(segs)
        rows = l
        k_specs = [pl.BlockSpec((None, kk.shape[1], kw), lambda i, n: (i, 0, 0)) for kk, _ in segs]
        k_args, v_args = [kk for kk, _ in segs], [vv for _, vv in segs]
        key_rows = win = sum(kk.shape[1] for kk in k_args)
    vmem = 2 * (2 * rows * qw * 2 + 2 * key_rows * kw * 4) + 3 * nchunk * hkv * grp * cq * 256 * 4
    return pl.pallas_call(
        functools.partial(_swa_body, n_seg=n_seg, cq=cq, nchunk=nchunk, win=win, hkv=hkv, grp=grp, hd=hd,
                          chunked=chunked),
        grid=(b, l // rows),
        in_specs=[pl.BlockSpec(memory_space=pltpu.SMEM),
                  pl.BlockSpec((None, rows, qw), lambda i, n: (i, n, 0))] + k_specs + k_specs,
        out_specs=pl.BlockSpec((None, rows, qw), lambda i, n: (i, n, 0)),
        out_shape=jax.ShapeDtypeStruct((b, l, qw), BF16),
        compiler_params=_params(("parallel", "parallel"), vmem), name="swa_attention")(
            sinks, q, *k_args, *v_args)


def _ab_layer(x, conv_prev, s0, past, w_in, conv_w, a_log, dt_bias, onorm_g, f_bias, w_out, g_mix, dims):
    b, l, d = x.shape
    h_a, dk, h_b, hd_b = dims
    a_qk, a_v, b_w = h_a * dk, h_a * dk, h_b * hd_b
    conv_dim = 2 * a_qk + a_v
    o1 = conv_dim + a_v
    o2 = o1 + 2 * h_a
    small = jnp.concatenate([w_in[:, o1:o2], w_in[:, o2 + 3 * b_w:],
                             jnp.zeros((d, LANE - 2 * h_a - h_b), F32)], axis=1)
    w_pack = jnp.concatenate([w_in[:, :o1], w_in[:, o2:o2 + 3 * b_w], small], axis=1).astype(BF16)
    zeros4 = jnp.zeros((h_a,), F32)
    gp = jnp.zeros((SUBLANE, LANE), F32)
    gp = gp.at[0, :3 * h_a].set(jnp.concatenate([zeros4, dt_bias, f_bias]))
    gp = gp.at[1, h_a:2 * h_a].set(a_log)
    f32o, b16, cache = (F32, 1.0, 1), (BF16, 1.0, 1), (F32, 1.0, h_b)
    groups = ((conv_dim, (f32o,)), (a_v, (f32o,)), (b_w, ((BF16, LOG2E * hd_b ** -0.5, 1),)),
              (b_w, (cache, b16)), (b_w, (cache, b16)), (LANE, (f32o,)))
    conv = (conv_prev, conv_w, l, dk, 2 * h_a, h_a, dk ** -0.5)
    qkv, z, qb, kb, kb16, vb, vb16, gates, tail = norm_matmul(x.reshape(b * l, d), g_mix, w_pack, groups,
                                                             gate_params=gp, conv=conv)
    r3 = lambda a: a.reshape(b, l, a.shape[-1])
    gates3 = r3(gates)
    oa, s_new = gated_delta_net(r3(qkv), r3(z), gates3, s0, onorm_g, h_a, dk)
    conv_new = tail[:, SUBLANE - (CONV_W - 1):, :]
    logf = gates3[:, :, 2 * h_a:2 * h_a + h_b]
    lf_rows = jnp.pad(logf.transpose(0, 2, 1).reshape(b * h_b, l), ((0, 0), (0, -l % LANE)))
    f_new = cumsum_rows(lf_rows, LOG2E)[:, :l].reshape(b, h_b, l)
    fox_past = None
    if past is not None:
        pk, pv, plf = past
        p_len = pk.shape[1]
        f_past = cumsum_rows(plf.transpose(0, 2, 1).reshape(b * h_b, p_len), LOG2E, minus_total=True)
        fox_past = (pk.reshape(b, p_len * h_b, hd_b), pv.reshape(b, p_len * h_b, hd_b), f_past.reshape(b, h_b, p_len))
    ob = fox_attention(r3(qb), r3(kb16), r3(vb16), f_new, h_b, past=fox_past)
    w_out16 = w_out.astype(BF16)
    mixed = ([oa, ob], [w_out16[:a_v], w_out16[a_v:]])
    return (mixed, s_new, conv_new, kb.reshape(b, l, h_b, hd_b), vb.reshape(b, l, h_b, hd_b), logf)


def _c_layer(x, past, w_in, sinks, w_out, g_mix, dims):
    b, l, d = x.shape
    hq, hkv, hd = dims
    grp = hq // hkv
    f32o = (F32, 1.0, 1)
    q, k, v = norm_matmul(x.reshape(b * l, d), g_mix, w_in.astype(BF16),
                          ((hq * hd, ((BF16, LOG2E * hd ** -0.5, 1),)), (hkv * hd, (f32o,)), (hkv * hd, (f32o,))))
    q3, k3, v3 = q.reshape(b, l, hq * hd), k.reshape(b, l, hkv * hd), v.reshape(b, l, hkv * hd)
    if past is None:
        o = swa_attention(q3, (k3, v3), sinks, hkv, grp, hd, chunked=True)
        new_k, new_v = k3[:, l - WINDOW:], v3[:, l - WINDOW:]
    else:
        pk, pv = past
        w_len = pk.shape[1]
        o = swa_attention(q3, ((pk.reshape(b, w_len, hkv * hd), pv.reshape(b, w_len, hkv * hd)), (k3, v3)),
                          sinks, hkv, grp, hd, chunked=False)
        new_k, new_v = k3, v3
    return ([o], [w_out.astype(BF16)]), new_k.reshape(b, -1, hkv, hd), new_v.reshape(b, -1, hkv, hd)


def kernel(x_prompt, mem_prompt, x_sample, state_gdn, state_gdn_conv, cache_fox_k, cache_fox_v, cache_fox_logf, cache_swa_k, cache_swa_v, cache_mem_k, cache_mem_v, g_mix, w_in_ab, conv_w_a, a_log_a, dt_bias_a, onorm_g_a, f_bias_b, w_out_ab, w_in_c, sinks_c, w_out_c, g_mem, w_xkv, g_cross, w_xq, w_xo, g_mlp, w_up, w_down, g_final):
    depth = g_mix.shape[0]
    bp, lp, d = x_prompt.shape
    bs, ls, _ = x_sample.shape
    h_a, dk = state_gdn.shape[2], state_gdn.shape[3]
    h_b, hd_b = cache_fox_k.shape[3], cache_fox_k.shape[4]
    hkv, hd_c = cache_swa_k.shape[3], cache_swa_k.shape[4]
    hq = sinks_c.shape[1]
    m_len, h_x, hd_x = cache_mem_k.shape[2], cache_mem_k.shape[3], cache_mem_k.shape[4]
    xw = h_x * hd_x
    conv_dim = state_gdn_conv.shape[-1]

    xp, xs = x_prompt, x_sample
    p_out = {k: [] for k in ("s", "c", "fk", "fv", "fl", "wk", "wv", "mk", "mv")}
    s_out = {k: [] for k in ("s", "c", "fk", "fv", "fl", "wk", "wv")}
    for layer in range(depth):
        i = layer // 2
        if layer % 2 == 0:
            wts = (w_in_ab[i], conv_w_a[i], a_log_a[i], dt_bias_a[i], onorm_g_a[i], f_bias_b[i], w_out_ab[i],
                   g_mix[layer], (h_a, dk, h_b, hd_b))
            mix_p, st, cv, fk, fv, fl = _ab_layer(xp, jnp.zeros((bp, CONV_W - 1, conv_dim), F32),
                                                  jnp.zeros((bp, h_a, dk, dk), F32), None, *wts)
            for key, val in zip(("s", "c", "fk", "fv", "fl"), (st, cv, fk, fv, fl)):
                p_out[key].append(val)
            mix_s, st, cv, fk, fv, fl = _ab_layer(xs, state_gdn_conv[i], state_gdn[i],
                                                  (cache_fox_k[i], cache_fox_v[i], cache_fox_logf[i]), *wts)
            for key, val in zip(("s", "c", "fk", "fv", "fl"), (st, cv, fk, fv, fl)):
                s_out[key].append(val)
        else:
            wts = (w_in_c[i], sinks_c[i], w_out_c[i], g_mix[layer], (hq, hkv, hd_c))
            mix_p, wk, wv = _c_layer(xp, None, *wts)
            p_out["wk"].append(wk); p_out["wv"].append(wv)
            mix_s, wk, wv = _c_layer(xs, (cache_swa_k[i], cache_swa_v[i]), *wts)
            s_out["wk"].append(wk); s_out["wv"].append(wv)
        mem_outs = ((F32, 1.0, h_x),)
        mk, mv = norm_matmul(mem_prompt.reshape(bp * m_len, d), g_mem[layer], w_xkv[layer].astype(BF16),
                             ((xw, mem_outs), (xw, mem_outs)))
        p_out["mk"].append(mk.reshape(bp, m_len, h_x, hd_x)); p_out["mv"].append(mv.reshape(bp, m_len, h_x, hd_x))
        post_w = (g_cross[layer], w_xq[layer].astype(BF16))
        mlp_w = (w_xo[layer].astype(BF16), g_mlp[layer], w_up[layer].astype(BF16), w_down[layer].astype(BF16), h_x,
                 g_final if layer == depth - 1 else None)
        xp = post_mixer(xp, *mix_p, *post_w, mk.reshape(bp, m_len * h_x, hd_x), mv.reshape(bp, m_len * h_x, hd_x),
                        *mlp_w)
        xs = post_mixer(xs, *mix_s, *post_w, cache_mem_k[layer].reshape(bs, m_len * h_x, hd_x),
                        cache_mem_v[layer].reshape(bs, m_len * h_x, hd_x), *mlp_w)
    st = jnp.stack
    return (xp, xs,
            st(p_out["s"]), st(p_out["c"]), st(p_out["fk"]), st(p_out["fv"]), st(p_out["fl"]),
            st(p_out["wk"]), st(p_out["wv"]), st(p_out["mk"]), st(p_out["mv"]),
            st(s_out["s"]), st(s_out["c"]), st(s_out["fk"]), st(s_out["fv"]), st(s_out["fl"]),
            st(s_out["wk"]), st(s_out["wv"]))
```

```python
import functools
import math

import jax
import jax.numpy as jnp
from jax import lax
from jax.experimental import pallas as pl
from jax.experimental.pallas import tpu as pltpu

F32 = jnp.float32
BF16 = jnp.bfloat16
LOG2E = math.log2(math.e)
EPS = 1e-6
NEG = -1e30

CHUNK = 64
CONV_W = 4
WINDOW = 128
LANE = 128
SUBLANE = 8
VMEM_BYTES = 64 * 1024 * 1024


def _params(semantics, vmem_bytes):
    limit = int(min(max(vmem_bytes * 1.25, 16 * 1024 * 1024), VMEM_BYTES - 8 * 1024 * 1024))
    return pltpu.CompilerParams(dimension_semantics=semantics, vmem_limit_bytes=limit)


def _dot(a, b):
    return jnp.dot(a, b, preferred_element_type=F32)


def _dot_nt(a, b):
    return lax.dot_general(a, b, (((1,), (1,)), ((), ())), preferred_element_type=F32)


def _dot_tn(a, b):
    return lax.dot_general(a, b, (((0,), (0,)), ((), ())), preferred_element_type=F32)


def _rms(x, g):
    return x * lax.rsqrt(jnp.mean(x * x, axis=-1, keepdims=True) + EPS) * g


def _sigmoid(x):
    return 1.0 / (1.0 + jnp.exp2(x * (-LOG2E)))


def _softplus(x):
    return jnp.maximum(x, 0.0) + jnp.log(1.0 + jnp.exp(-jnp.abs(x)))


def _token_tile(t, cap):
    tm = min(t, cap)
    assert t % tm == 0
    return tm


def _norm_matmul_body(*refs, groups, gated, col_chunk):
    x_ref, g_ref, w_ref = refs[:3]
    refs = refs[3:]
    if gated:
        gp_ref, refs = refs[0], refs[1:]
    x = x_ref[...]
    yb = _rms(x, g_ref[...]).astype(BF16)
    chunks, off, k = [], 0, 0
    for gi, (n, outs) in enumerate(groups):
        chunks += [(gi, n, outs, k, off, c, min(col_chunk, n - c)) for c in range(0, n, col_chunk)]
        off += n
        k += len(outs)

    def project(chunk):
        _, _, _, _, off, c, cw = chunk
        return _dot(yb, w_ref[:, off + c:off + c + cw])

    r_next = project(chunks[0])
    for idx, (gi, n, outs, k, off, c, cw) in enumerate(chunks):
        r = r_next
        if idx + 1 < len(chunks):
            r_next = project(chunks[idx + 1])
        if gated and gi == len(groups) - 1:
            rb = r + gp_ref[0:1, :]
            lane = lax.broadcasted_iota(jnp.int32, rb.shape, 1)
            e = jnp.exp(-jnp.abs(rb))
            l1p = jnp.log(1.0 + e)
            sig = _sigmoid(rb)
            gdec = -jnp.exp(gp_ref[1:2, :]) * (jnp.maximum(rb, 0.0) + l1p)
            logsig = jnp.minimum(rb, 0.0) - l1p
            r = jnp.where(lane < 4, sig, jnp.where(lane < 8, gdec, jnp.where(lane < 12, logsig, 0.0)))
        for j, (dtype, scale, heads) in enumerate(outs):
            o_ref = refs[k + j]
            rr = (r if scale == 1.0 else r * scale).astype(dtype)
            if heads == 1:
                o_ref[:, c:c + cw] = rr
            else:
                hw = n // heads
                tm = x_ref.shape[0]
                for h in range(c // hw, (c + cw) // hw):
                    o_ref[pl.ds(h, tm, stride=heads), :] = rr[:, h * hw - c:(h + 1) * hw - c]


def norm_matmul(x, g, w, groups, gate_params=None, tile_cap=1024):
    t, d = x.shape
    tm = _token_tile(t, tile_cap)
    gated = gate_params is not None
    in_specs = [pl.BlockSpec((tm, d), lambda i: (i, 0)),
                pl.BlockSpec((1, d), lambda i: (0, 0)),
                pl.BlockSpec(w.shape, lambda i: (0, 0), pipeline_mode=pl.Buffered(1))]
    args = [x, g.reshape(1, d), w]
    if gated:
        in_specs.append(pl.BlockSpec(gate_params.shape, lambda i: (0, 0)))
        args.append(gate_params)
    out_shape, out_specs, out_bytes = [], [], 0
    for n, outs in groups:
        for dtype, _, heads in outs:
            out_shape.append(jax.ShapeDtypeStruct((t * heads, n // heads), dtype))
            out_specs.append(pl.BlockSpec((tm * heads, n // heads), lambda i: (i, 0)))
            out_bytes += tm * n * jnp.dtype(dtype).itemsize
    vmem = 2 * (tm * d * 4 + out_bytes) + w.size * 2 + tm * d * 8
    return pl.pallas_call(
        functools.partial(_norm_matmul_body, groups=groups, gated=gated, col_chunk=512),
        grid=(t // tm,), in_specs=in_specs, out_specs=out_specs, out_shape=out_shape,
        compiler_params=_params(("parallel",), vmem), name="norm_matmul")(*args)


def _post_body(*refs, n_in, heads, hd, ff_chunk, final_norm, col_chunk):
    x_ref = refs[0]
    a_refs, w_refs = refs[1:1 + n_in], refs[1 + n_in:1 + 2 * n_in]
    gc_ref, wq_ref, mk_ref, mv_ref, wo_ref, gm_ref, wu_ref, wd_ref = refs[1 + 2 * n_in:9 + 2 * n_in]
    gf_ref = refs[9 + 2 * n_in] if final_norm else None
    o_ref = refs[-1]
    bt, tm, d = x_ref.shape
    rows = bt * tm
    mem_len = mk_ref.shape[1] // heads
    x0 = x_ref[...].reshape(rows, d)
    a_vals = [a[...].reshape(rows, a.shape[-1]).astype(BF16) for a in a_refs]
    x1 = jnp.concatenate(
        [x0[:, c:c + col_chunk] + sum(_dot(a, w[:, c:c + col_chunk]) for a, w in zip(a_vals, w_refs))
         for c in range(0, d, col_chunk)], axis=-1)
    hb = _rms(x1, gc_ref[...]).astype(BF16)
    q = (_dot(hb, wq_ref[...]) * (hd ** -0.5)).astype(BF16)
    att = []
    for bi in range(bt):
        outs = []
        for h in range(heads):
            mem_rows = pl.ds(h, mem_len, stride=heads)
            s = _dot_nt(q[bi * tm:(bi + 1) * tm, h * hd:(h + 1) * hd], mk_ref[bi, mem_rows, :].astype(BF16))
            m = jnp.max(s, axis=-1, keepdims=True)
            p = jnp.exp(s - m)
            l = jnp.sum(p, axis=-1, keepdims=True)
            outs.append((_dot(p.astype(BF16), mv_ref[bi, mem_rows, :].astype(BF16)) / l).astype(BF16))
        att.append(jnp.concatenate(outs, axis=-1))
    x2 = x1 + _dot(jnp.concatenate(att, axis=0), wo_ref[...])
    hb = _rms(x2, gm_ref[...]).astype(BF16)
    acc = x2
    for c in range(0, wu_ref.shape[1], ff_chunk):
        u = jnp.maximum(_dot(hb, wu_ref[:, c:c + ff_chunk]), 0.0)
        acc = acc + _dot((u * u).astype(BF16), wd_ref[c:c + ff_chunk, :])
    if final_norm:
        acc = _rms(acc, gf_ref[...])
    o_ref[...] = acc.reshape(bt, tm, d)


def post_mixer(x, a_list, w_list, g_cross, wq, mk, mv, wo, g_mlp, w_up, w_down, heads, g_final=None, tile_cap=512):
    b, l, d = x.shape
    mh, hd = mk.shape[1], mk.shape[2]
    m, hw = mh // heads, hd * heads
    tm = _token_tile(l, tile_cap)
    bt = math.gcd(b, max(1, tile_cap // tm))
    rows = bt * tm
    n_in = len(a_list)
    final_norm = g_final is not None
    once = pl.Buffered(1)

    def const(shape):
        return pl.BlockSpec(shape, lambda i, j: (0,) * len(shape), pipeline_mode=once)

    in_specs = [pl.BlockSpec((bt, tm, d), lambda i, j: (i, j, 0))]
    in_specs += [pl.BlockSpec((bt, tm, a.shape[2]), lambda i, j: (i, j, 0)) for a in a_list]
    in_specs += [const(w.shape) for w in w_list]
    in_specs += [const((1, d)), const(wq.shape),
                 pl.BlockSpec((bt, mh, hd), lambda i, j: (i, 0, 0)),
                 pl.BlockSpec((bt, mh, hd), lambda i, j: (i, 0, 0)),
                 const(wo.shape), const((1, d)), const(w_up.shape), const(w_down.shape)]
    args = [x, *a_list, *w_list, g_cross.reshape(1, d), wq, mk, mv, wo, g_mlp.reshape(1, d), w_up, w_down]
    if final_norm:
        in_specs.append(const((1, d)))
        args.append(g_final.reshape(1, d))
    weights = sum(w.size for w in w_list) + wq.size + wo.size + w_up.size + w_down.size
    vmem = (weights * 2 + 2 * (2 * rows * d * 4 + 2 * bt * m * hw * 4
                               + sum(rows * a.shape[2] * a.dtype.itemsize for a in a_list))
            + rows * (d * 16 + hw * 8 + m * 8 + 512 * 8))
    return pl.pallas_call(
        functools.partial(_post_body, n_in=n_in, heads=heads, hd=hw // heads, ff_chunk=512,
                          final_norm=final_norm, col_chunk=512),
        grid=(b // bt, l // tm), in_specs=in_specs,
        out_specs=pl.BlockSpec((bt, tm, d), lambda i, j: (i, j, 0)),
        out_shape=jax.ShapeDtypeStruct((b, l, d), F32),
        compiler_params=_params(("parallel", "parallel"), vmem), name="post_mixer")(*args)


def _cumsum_body(x_ref, o_ref, *, minus_total, scale):
    r, l = x_ref.shape
    row = lax.broadcasted_iota(jnp.int32, (LANE, LANE), 0)
    col = lax.broadcasted_iota(jnp.int32, (LANE, LANE), 1)
    upper = (row <= col).astype(BF16)
    carry = jnp.zeros((r, 1), F32)
    for c in range(0, l, LANE):
        x = x_ref[:, c:c + LANE]
        x1 = x.astype(BF16)
        res = x - x1.astype(F32)
        x2 = res.astype(BF16)
        x3 = (res - x2.astype(F32)).astype(BF16)
        y3 = _dot(jnp.concatenate([x1, x2, x3], axis=0), upper)
        y = y3[:r] + y3[r:2 * r] + y3[2 * r:] + carry
        o_ref[:, c:c + LANE] = y
        carry = y[:, LANE - 1:LANE]
    total = carry if minus_total else 0.0
    o_ref[...] = (o_ref[...] - total) * scale


def cumsum_rows(x, scale, minus_total=False):
    r, l = x.shape
    assert l % LANE == 0 and r % SUBLANE == 0
    return pl.pallas_call(
        functools.partial(_cumsum_body, minus_total=minus_total, scale=scale),
        grid=(1,), in_specs=[pl.BlockSpec((r, l), lambda i: (0, 0))],
        out_specs=pl.BlockSpec((r, l), lambda i: (0, 0)),
        out_shape=jax.ShapeDtypeStruct((r, l), F32),
        compiler_params=_params(("arbitrary",), 4 * r * l * 4), name="cumsum_rows")(x)


def _fox_body(*refs, tq, tkp, n_past, nsub, heads):
    if n_past:
        q_ref, k_ref, v_ref, fk_ref, kp_ref, vp_ref, fp_ref, o_ref = refs
    else:
        q_ref, k_ref, v_ref, fk_ref, o_ref = refs
    i = pl.program_id(2)
    sub = tq // nsub
    hd = q_ref.shape[-1]
    qs = [q_ref[c * sub:(c + 1) * sub, :] for c in range(nsub)]

    def update(carry, ss, vbs):
        ms, ls, accs = carry[0::3], carry[1::3], carry[2::3]
        m_new = [jnp.maximum(m, jnp.max(s, axis=-1, keepdims=True)) for m, s in zip(ms, ss)]
        alpha = [jnp.exp2(m - mn) for m, mn in zip(ms, m_new)]
        ps = [jnp.exp2(s - mn) for s, mn in zip(ss, m_new)]
        ls = [a * l + jnp.sum(p, axis=-1, keepdims=True) for a, l, p in zip(alpha, ls, ps)]
        pv = [_dot(p.astype(BF16), vb) for p, vb in zip(ps, vbs)]
        accs = [a * acc + x for a, acc, x in zip(alpha, accs, pv)]
        out = []
        for c in range(nsub):
            out += [m_new[c], ls[c], accs[c]]
        return tuple(out)

    carry = (jnp.full((sub, 1), NEG, F32), jnp.zeros((sub, 1), F32), jnp.zeros((sub, hd), F32)) * nsub

    if n_past:
        def past_step(j, carry):
            rows = pl.ds(j * tkp * heads + pl.program_id(1), tkp, stride=heads)
            kb = kp_ref[rows, :].astype(BF16)
            vb = vp_ref[rows, :].astype(BF16)
            fk = fp_ref[pl.ds(j, 1), :]
            return update(carry, [_dot_nt(q, kb) - fk for q in qs], [vb] * nsub)
        carry = lax.fori_loop(0, n_past, past_step, carry)

    def self_step(j, carry):
        start = pl.multiple_of(j * sub, sub)
        kb = k_ref[pl.ds(start, sub), :]
        vb = v_ref[pl.ds(start, sub), :]
        fk = fk_ref[pl.ds(j, 1), :][:, :sub]
        return update(carry, [_dot_nt(q, kb) - fk for q in qs], [vb] * nsub)
    carry = lax.fori_loop(0, i * nsub, self_step, carry)

    start = pl.multiple_of(i * tq, tq)
    fks = [fk_ref[pl.ds(i * nsub + c, 1), :] for c in range(nsub)]
    ss, vbs = [], []
    for c in range(nsub):
        w = (c + 1) * sub
        fk = fks[0][:, :sub] if c == 0 else jnp.concatenate(fks[:c + 1], axis=1)
        s = _dot_nt(qs[c], k_ref[pl.ds(start, w), :]) - fk
        row = lax.broadcasted_iota(jnp.int32, (sub, w), 0) + c * sub
        col = lax.broadcasted_iota(jnp.int32, (sub, w), 1)
        ss.append(jnp.where(col <= row, s, NEG))
        vbs.append(v_ref[pl.ds(start, w), :])
    carry = update(carry, ss, vbs)
    for c in range(nsub):
        o_ref[c * sub:(c + 1) * sub, :] = (carry[3 * c + 2] / carry[3 * c + 1]).astype(o_ref.dtype)


def fox_attention(q, k, v, f_new, heads, past=None, tq_cap=1024, sub_cap=256, score_tile=64 * 1024):
    b, l, hw = q.shape
    hd = hw // heads
    tq = min(l, tq_cap)
    nq = l // tq
    nsub = max(1, tq // sub_cap)
    p_len = 0 if past is None else past[0].shape[1] // heads
    tkp = min(p_len, max(sub_cap, score_tile // tq))
    sub = tq // nsub
    nk = l // sub
    wk = -(-sub // LANE) * LANE
    if wk != sub:
        assert nk == 1
        f_new = jnp.pad(f_new, ((0, 0), (0, 0), (0, wk - sub)))
    in_specs = [pl.BlockSpec((None, tq, hd), lambda bi, h, i: (bi, i, h)),
                pl.BlockSpec((None, l, hd), lambda bi, h, i: (bi, 0, h)),
                pl.BlockSpec((None, l, hd), lambda bi, h, i: (bi, 0, h)),
                pl.BlockSpec((None, None, nk, wk), lambda bi, h, i: (bi, h, 0, 0))]
    args = [q, k, v, f_new.reshape(b, heads, nk, wk)]
    n_past = 0
    vmem = 2 * (2 * tq * hd * 2 + 2 * l * hd * 2) + 8 * tq * max(tq, tkp) * 4
    if past is not None:
        kp, vp, fp = past
        n_past = p_len // tkp
        in_specs += [pl.BlockSpec((None, p_len * heads, hd), lambda bi, h, i: (bi, 0, 0)),
                     pl.BlockSpec((None, p_len * heads, hd), lambda bi, h, i: (bi, 0, 0)),
                     pl.BlockSpec((None, None, n_past, tkp), lambda bi, h, i: (bi, h, 0, 0))]
        args += [kp, vp, fp.reshape(b, heads, n_past, tkp)]
        vmem += 2 * 2 * p_len * heads * hd * 4
    return pl.pallas_call(
        functools.partial(_fox_body, tq=tq, tkp=tkp, n_past=n_past, nsub=nsub, heads=heads),
        grid=(b, heads, nq), in_specs=in_specs,
        out_specs=pl.BlockSpec((None, tq, hd), lambda bi, h, i: (bi, i, h)),
        out_shape=jax.ShapeDtypeStruct((b, l, hw), BF16),
        compiler_params=_params(("parallel", "parallel", "arbitrary"), vmem), name="fox_attention")(*args)


def _gdn_body(qkv_ref, z_ref, gates_ref, cprev_ref, cw_ref, s0_ref, og_ref, o_ref, st_ref, ybuf_ref,
              *, c, nc, heads, dk):
    pad = SUBLANE
    r = c * nc
    hw = heads * dk

    @pl.when(pl.program_id(1) == 0)
    def _():
        ybuf_ref[0:pad, :] = cprev_ref[...]
        st_ref[...] = s0_ref[...]

    ybuf_ref[pad:pad + r, :] = qkv_ref[...]

    row = lax.broadcasted_iota(jnp.int32, (c, c), 0)
    col = lax.broadcasted_iota(jnp.int32, (c, c), 1)
    incl = row >= col
    strict = row > col
    tril = incl.astype(BF16)
    n_double = int(math.log2(c)) - 1
    og = og_ref[...]
    states = [st_ref[h] for h in range(heads)]

    def prep(g):
        r0 = g["ci"] * c
        conv = qkv_ref[r0:r0 + c, :] * cw_ref[CONV_W - 1:CONV_W, :]
        for k in range(1, CONV_W):
            conv = conv + ybuf_ref[pad + r0 - k:pad + r0 - k + c, :] * cw_ref[CONV_W - 1 - k:CONV_W - k, :]
        act = conv * _sigmoid(conv)
        gs = gates_ref[r0:r0 + c, :]
        g1 = gs.astype(BF16)
        res = gs - g1.astype(F32)
        g2 = res.astype(BF16)
        g3 = (res - g2.astype(F32)).astype(BF16)
        cs3 = _dot(tril, jnp.concatenate([g1, g2, g3], axis=-1))
        cs = cs3[:, :LANE] + cs3[:, LANE:2 * LANE] + cs3[:, 2 * LANE:]
        cs_t = jnp.concatenate([cs, jnp.zeros((LANE - c, LANE), F32)], axis=0).T
        g["chains"] = []
        for h in range(heads):
            qh = act[:, h * dk:(h + 1) * dk]
            kh = act[:, hw + h * dk:hw + (h + 1) * dk]
            vh = act[:, 2 * hw + h * dk:2 * hw + (h + 1) * dk]
            qh = qh * (lax.rsqrt(jnp.sum(qh * qh, axis=-1, keepdims=True) + EPS) * (dk ** -0.5))
            kh = kh * lax.rsqrt(jnp.sum(kh * kh, axis=-1, keepdims=True) + EPS)
            beta = gs[:, h:h + 1]
            gc = cs[:, heads + h:heads + h + 1]
            gc_row = cs_t[heads + h:heads + h + 1, :c]
            kb = kh * beta
            egc = jnp.exp(gc)
            g_last = gc[c - 1:c, :]
            g["chains"].append(dict(
                decay=jnp.exp(jnp.where(incl, gc - gc_row, NEG)),
                kq=jnp.concatenate([kb, qh], axis=0).astype(BF16), khb=kh.astype(BF16),
                rhs=jnp.concatenate([vh * beta, kb * egc], axis=-1), qg=qh * egc,
                kd=(kh * jnp.exp(g_last - gc)).astype(BF16), ds=jnp.exp(g_last)))

    def gram(g):
        for ch in g["chains"]:
            ch["a1"] = _dot_nt(ch["kq"], ch["khb"])

    def masks(g):
        for ch in g["chains"]:
            lmat = jnp.where(strict, ch["a1"][:c] * ch["decay"], 0.0)
            ch["attn"] = jnp.where(incl, ch["a1"][c:] * ch["decay"], 0.0).astype(BF16)
            ch["toff"] = -lmat
            ch["lb"] = lmat.astype(BF16)
        for ch in g["chains"]:
            ch["pw"] = _dot(ch["lb"], ch["lb"])

    def double(last):
        def stage(g):
            for ch in g["chains"]:
                pwb = ch["pw"].astype(BF16)
                lhs = ch["toff"].astype(BF16) if last else jnp.concatenate([ch["toff"].astype(BF16), pwb], axis=0)
                ch["tp"] = _dot(lhs, pwb)
            for ch in g["chains"]:
                ch["toff"] = ch["toff"] + ch["pw"] + ch["tp"][:c]
                if not last:
                    ch["pw"] = ch["tp"][c:]
        return stage

    def solve(g):
        for ch in g["chains"]:
            ch["uw"] = _dot(ch["toff"].astype(BF16), ch["rhs"].astype(BF16))
        for ch in g["chains"]:
            uw = ch["rhs"] + ch["uw"]
            ch["u"] = uw[:, :dk]
            ch["wq"] = jnp.concatenate([uw[:, dk:], ch["qg"]], axis=0).astype(BF16)

    def recur(g):
        group, r0 = g["chains"], g["ci"] * c
        t1 = [_dot(ch["wq"], s.astype(BF16)) for ch, s in zip(group, states)]
        vnb = [(ch["u"] - t[:c]).astype(BF16) for ch, t in zip(group, t1)]
        o2 = [_dot(ch["attn"], v) for ch, v in zip(group, vnb)]
        sd = [_dot_tn(ch["kd"], v) for ch, v in zip(group, vnb)]
        states[:] = [s * ch["ds"] + d for ch, s, d in zip(group, states, sd)]
        for h in range(heads):
            o = t1[h][c:] + o2[h]
            zh = z_ref[r0:r0 + c, h * dk:(h + 1) * dk]
            o_ref[r0:r0 + c, h * dk:(h + 1) * dk] = (_rms(o, og) * (zh * _sigmoid(zh))).astype(o_ref.dtype)

    stages = [prep, gram, masks] + [double(s == n_double - 1) for s in range(n_double)] + [solve, recur]
    groups = [dict(ci=ci) for ci in range(nc)]
    for t in range(len(stages) + nc - 1):
        for ci in range(nc):
            if 0 <= t - ci < len(stages):
                stages[t - ci](groups[ci])

    for h in range(heads):
        st_ref[h] = states[h]
    ybuf_ref[0:pad, :] = ybuf_ref[r:r + pad, :]


def gated_delta_net(qkv, z, gates, conv_prev, conv_w, s0, onorm_g, heads, dk, chunks_per_step=8):
    b, l, cd = qkv.shape
    c = min(CHUNK, l)
    nc = min(chunks_per_step, l // c)
    r = c * nc
    hw = heads * dk
    cprev = jnp.pad(conv_prev, ((0, 0), (SUBLANE - (CONV_W - 1), 0), (0, 0)))
    cw = jnp.pad(conv_w, ((0, SUBLANE - CONV_W), (0, 0)))
    vmem = 2 * (r * cd * 4 + r * hw * 4 + r * LANE * 4 + 2 * SUBLANE * cd * 4 + 2 * heads * dk * dk * 4
                + r * hw * 2) + (r + SUBLANE) * cd * 4 * 4
    return pl.pallas_call(
        functools.partial(_gdn_body, c=c, nc=nc, heads=heads, dk=dk),
        grid=(b, l // r),
        in_specs=[pl.BlockSpec((None, r, cd), lambda i, n: (i, n, 0)),
                  pl.BlockSpec((None, r, hw), lambda i, n: (i, n, 0)),
                  pl.BlockSpec((None, r, LANE), lambda i, n: (i, n, 0)),
                  pl.BlockSpec((None, SUBLANE, cd), lambda i, n: (i, 0, 0)),
                  pl.BlockSpec((SUBLANE, cd), lambda i, n: (0, 0)),
                  pl.BlockSpec((None, heads, dk, dk), lambda i, n: (i, 0, 0, 0)),
                  pl.BlockSpec((1, dk), lambda i, n: (0, 0))],
        out_specs=[pl.BlockSpec((None, r, hw), lambda i, n: (i, n, 0)),
                   pl.BlockSpec((None, heads, dk, dk), lambda i, n: (i, 0, 0, 0))],
        out_shape=[jax.ShapeDtypeStruct((b, l, hw), BF16),
                   jax.ShapeDtypeStruct((b, heads, dk, dk), F32)],
        scratch_shapes=[pltpu.VMEM((r + SUBLANE, cd), F32)],
        compiler_params=_params(("parallel", "arbitrary"), vmem), name="gated_delta_net")(
            qkv, z, gates, cprev, cw, s0, onorm_g.reshape(1, dk))


def _swa_body(*refs, n_seg, cq, nchunk, win, hkv, grp, hd, chunked):
    sink_ref, q_ref = refs[0], refs[1]
    k_refs, v_refs, o_ref = refs[2:2 + n_seg], refs[2 + n_seg:2 + 2 * n_seg], refs[2 + 2 * n_seg]
    kk = jnp.concatenate([r[...] for r in k_refs], axis=0).astype(BF16)
    vv = jnp.concatenate([r[...] for r in v_refs], axis=0).astype(BF16)
    row = lax.broadcasted_iota(jnp.int32, (cq, win), 0)
    col = lax.broadcasted_iota(jnp.int32, (cq, win), 1)
    dist = jnp.abs(row - col + WINDOW).astype(F32)
    hq = hkv * grp
    k_heads = [kk[:, j * hd:(j + 1) * hd] for j in range(hkv)]
    v_heads = [vv[:, j * hd:(j + 1) * hd] for j in range(hkv)]
    chains = []
    for c in range(nchunk):
        dist_c = dist
        if chunked and c * cq < WINDOW:
            first_valid = WINDOW - (pl.program_id(1) * nchunk + c) * cq
            dist_c = jnp.where(col >= first_valid, dist, -NEG)
        for j in range(hkv):
            q_stack = jnp.concatenate(
                [q_ref[c * cq:(c + 1) * cq, (j * grp + g) * hd:(j * grp + g + 1) * hd] for g in range(grp)], axis=0)
            chains.append(dict(c=c, j=j, dist=dist_c, q=q_stack,
                               k=k_heads[j][c * cq:c * cq + win, :], v=v_heads[j][c * cq:c * cq + win, :]))

    def scores(group):
        for ch in group:
            ch["s"] = _dot_nt(ch["q"], ch["k"])

    def softmax_pv(group):
        heads_of = [(ch, g, ch["j"] * grp + g) for ch in group for g in range(grp)]
        slopes = [LOG2E * 2.0 ** (-8.0 * (h + 1) / hq) for _, _, h in heads_of]
        sinks = [LOG2E * sink_ref[h] for _, _, h in heads_of]
        sg = [ch["s"][g * cq:(g + 1) * cq, :] - sl * ch["dist"] for (ch, g, _), sl in zip(heads_of, slopes)]
        m = [jnp.maximum(jnp.max(x, axis=-1, keepdims=True), sk) for x, sk in zip(sg, sinks)]
        p = [jnp.exp2(x - mm) for x, mm in zip(sg, m)]
        l = [jnp.sum(x, axis=-1, keepdims=True) + jnp.exp2(sk - mm) for x, mm, sk in zip(p, m, sinks)]
        pb = [x.astype(BF16) for x in p]
        for ci, ch in enumerate(group):
            ch["l"] = l[ci * grp:(ci + 1) * grp]
            ch["pv"] = _dot(jnp.concatenate(pb[ci * grp:(ci + 1) * grp], axis=0), ch["v"])

    def emit(group):
        for ch in group:
            c, j = ch["c"], ch["j"]
            outs = [(ch["pv"][g * cq:(g + 1) * cq, :] / ch["l"][g]).astype(o_ref.dtype) for g in range(grp)]
            o_ref[c * cq:(c + 1) * cq, j * grp * hd:(j + 1) * grp * hd] = jnp.concatenate(outs, axis=-1)

    gsz = 2
    groups = [chains[i:i + gsz] for i in range(0, len(chains), gsz)]
    scores(groups[0])
    for i, group in enumerate(groups):
        if i + 1 < len(groups):
            scores(groups[i + 1])
        softmax_pv(group)
        if i > 0:
            emit(groups[i - 1])
    emit(groups[-1])


def swa_attention(q, segs, sinks, hkv, grp, hd, chunked, chunks_per_step=8):
    b, l, qw = q.shape
    kw = hkv * hd
    if chunked:
        cq, nchunk, n_seg, win = CHUNK, chunks_per_step, 2, WINDOW + CHUNK
        rows = cq * nchunk
        assert l % rows == 0 and rows % WINDOW == 0
        k, v = segs
        k_specs = [pl.BlockSpec((None, WINDOW, kw), lambda i, n: (i, jnp.maximum(n * (rows // WINDOW) - 1, 0), 0)),
                   pl.BlockSpec((None, rows, kw), lambda i, n: (i, n, 0))]
        k_args, v_args = [k] * n_seg, [v] * n_seg
        key_rows = WINDOW + rows
    else:
        cq, nchunk, n_seg = l, 1, len(segs)
        rows = l
        k_specs = [pl.BlockSpec((None, kk.shape[1], kw), lambda i, n: (i, 0, 0)) for kk, _ in segs]
        k_args, v_args = [kk for kk, _ in segs], [vv for _, vv in segs]
        key_rows = win = sum(kk.shape[1] for kk in k_args)
    vmem = 2 * (2 * rows * qw * 2 + 2 * key_rows * kw * 4) + 3 * nchunk * hkv * grp * cq * 256 * 4
    return pl.pallas_call(
        functools.partial(_swa_body, n_seg=n_seg, cq=cq, nchunk=nchunk, win=win, hkv=hkv, grp=grp, hd=hd,
                          chunked=chunked),
        grid=(b, l // rows),
        in_specs=[pl.BlockSpec(memory_space=pltpu.SMEM),
                  pl.BlockSpec((None, rows, qw), lambda i, n: (i, n, 0))] + k_specs + k_specs,
        out_specs=pl.BlockSpec((None, rows, qw), lambda i, n: (i, n, 0)),
        out_shape=jax.ShapeDtypeStruct((b, l, qw), BF16),
        compiler_params=_params(("parallel", "parallel"), vmem), name="swa_attention")(
            sinks, q, *k_args, *v_args)


def _ab_layer(x, conv_prev, s0, past, w_in, conv_w, a_log, dt_bias, onorm_g, f_bias, w_out, g_mix, dims):
    b, l, d = x.shape
    h_a, dk, h_b, hd_b = dims
    a_qk, a_v, b_w = h_a * dk, h_a * dk, h_b * hd_b
    conv_dim = 2 * a_qk + a_v
    o1 = conv_dim + a_v
    o2 = o1 + 2 * h_a
    small = jnp.concatenate([w_in[:, o1:o2], w_in[:, o2 + 3 * b_w:],
                             jnp.zeros((d, LANE - 2 * h_a - h_b), F32)], axis=1)
    w_pack = jnp.concatenate([w_in[:, :o1], w_in[:, o2:o2 + 3 * b_w], small], axis=1).astype(BF16)
    zeros4 = jnp.zeros((h_a,), F32)
    gp = jnp.zeros((SUBLANE, LANE), F32)
    gp = gp.at[0, :3 * h_a].set(jnp.concatenate([zeros4, dt_bias, f_bias]))
    gp = gp.at[1, h_a:2 * h_a].set(a_log)
    f32o, b16, cache = (F32, 1.0, 1), (BF16, 1.0, 1), (F32, 1.0, h_b)
    groups = ((conv_dim, (f32o,)), (a_v, (f32o,)), (b_w, ((BF16, LOG2E * hd_b ** -0.5, 1),)),
              (b_w, (cache, b16)), (b_w, (cache, b16)), (LANE, (f32o,)))
    qkv, z, qb, kb, kb16, vb, vb16, gates = norm_matmul(x.reshape(b * l, d), g_mix, w_pack, groups, gate_params=gp)
    r3 = lambda a: a.reshape(b, l, a.shape[-1])
    qkv3, gates3 = r3(qkv), r3(gates)
    oa, s_new = gated_delta_net(qkv3, r3(z), gates3, conv_prev, conv_w, s0, onorm_g, h_a, dk)
    conv_new = qkv3[:, l - (CONV_W - 1):, :]
    logf = gates3[:, :, 2 * h_a:2 * h_a + h_b]
    lf_rows = jnp.pad(logf.transpose(0, 2, 1).reshape(b * h_b, l), ((0, 0), (0, -l % LANE)))
    f_new = cumsum_rows(lf_rows, LOG2E)[:, :l].reshape(b, h_b, l)
    fox_past = None
    if past is not None:
        pk, pv, plf = past
        p_len = pk.shape[1]
        f_past = cumsum_rows(plf.transpose(0, 2, 1).reshape(b * h_b, p_len), LOG2E, minus_total=True)
        fox_past = (pk.reshape(b, p_len * h_b, hd_b), pv.reshape(b, p_len * h_b, hd_b), f_past.reshape(b, h_b, p_len))
    ob = fox_attention(r3(qb), r3(kb16), r3(vb16), f_new, h_b, past=fox_past)
    w_out16 = w_out.astype(BF16)
    mixed = ([oa, ob], [w_out16[:a_v], w_out16[a_v:]])
    return (mixed, s_new, conv_new, kb.reshape(b, l, h_b, hd_b), vb.reshape(b, l, h_b, hd_b), logf)


def _c_layer(x, past, w_in, sinks, w_out, g_mix, dims):
    b, l, d = x.shape
    hq, hkv, hd = dims
    grp = hq // hkv
    f32o = (F32, 1.0, 1)
    q, k, v = norm_matmul(x.reshape(b * l, d), g_mix, w_in.astype(BF16),
                          ((hq * hd, ((BF16, LOG2E * hd ** -0.5, 1),)), (hkv * hd, (f32o,)), (hkv * hd, (f32o,))))
    q3, k3, v3 = q.reshape(b, l, hq * hd), k.reshape(b, l, hkv * hd), v.reshape(b, l, hkv * hd)
    if past is None:
        o = swa_attention(q3, (k3, v3), sinks, hkv, grp, hd, chunked=True)
        new_k, new_v = k3[:, l - WINDOW:], v3[:, l - WINDOW:]
    else:
        pk, pv = past
        w_len = pk.shape[1]
        o = swa_attention(q3, ((pk.reshape(b, w_len, hkv * hd), pv.reshape(b, w_len, hkv * hd)), (k3, v3)),
                          sinks, hkv, grp, hd, chunked=False)
        new_k, new_v = k3, v3
    return ([o], [w_out.astype(BF16)]), new_k.reshape(b, -1, hkv, hd), new_v.reshape(b, -1, hkv, hd)


def kernel(x_prompt, mem_prompt, x_sample, state_gdn, state_gdn_conv, cache_fox_k, cache_fox_v, cache_fox_logf, cache_swa_k, cache_swa_v, cache_mem_k, cache_mem_v, g_mix, w_in_ab, conv_w_a, a_log_a, dt_bias_a, onorm_g_a, f_bias_b, w_out_ab, w_in_c, sinks_c, w_out_c, g_mem, w_xkv, g_cross, w_xq, w_xo, g_mlp, w_up, w_down, g_final):
    depth = g_mix.shape[0]
    bp, lp, d = x_prompt.shape
    bs, ls, _ = x_sample.shape
    h_a, dk = state_gdn.shape[2], state_gdn.shape[3]
    h_b, hd_b = cache_fox_k.shape[3], cache_fox_k.shape[4]
    hkv, hd_c = cache_swa_k.shape[3], cache_swa_k.shape[4]
    hq = sinks_c.shape[1]
    m_len, h_x, hd_x = cache_mem_k.shape[2], cache_mem_k.shape[3], cache_mem_k.shape[4]
    xw = h_x * hd_x
    conv_dim = state_gdn_conv.shape[-1]

    xp, xs = x_prompt, x_sample
    p_out = {k: [] for k in ("s", "c", "fk", "fv", "fl", "wk", "wv", "mk", "mv")}
    s_out = {k: [] for k in ("s", "c", "fk", "fv", "fl", "wk", "wv")}
    for layer in range(depth):
        i = layer // 2
        if layer % 2 == 0:
            wts = (w_in_ab[i], conv_w_a[i], a_log_a[i], dt_bias_a[i], onorm_g_a[i], f_bias_b[i], w_out_ab[i],
                   g_mix[layer], (h_a, dk, h_b, hd_b))
            mix_p, st, cv, fk, fv, fl = _ab_layer(xp, jnp.zeros((bp, CONV_W - 1, conv_dim), F32),
                                                  jnp.zeros((bp, h_a, dk, dk), F32), None, *wts)
            for key, val in zip(("s", "c", "fk", "fv", "fl"), (st, cv, fk, fv, fl)):
                p_out[key].append(val)
            mix_s, st, cv, fk, fv, fl = _ab_layer(xs, state_gdn_conv[i], state_gdn[i],
                                                  (cache_fox_k[i], cache_fox_v[i], cache_fox_logf[i]), *wts)
            for key, val in zip(("s", "c", "fk", "fv", "fl"), (st, cv, fk, fv, fl)):
                s_out[key].append(val)
        else:
            wts = (w_in_c[i], sinks_c[i], w_out_c[i], g_mix[layer], (hq, hkv, hd_c))
            mix_p, wk, wv = _c_layer(xp, None, *wts)
            p_out["wk"].append(wk); p_out["wv"].append(wv)
            mix_s, wk, wv = _c_layer(xs, (cache_swa_k[i], cache_swa_v[i]), *wts)
            s_out["wk"].append(wk); s_out["wv"].append(wv)
        mem_outs = ((F32, 1.0, h_x),)
        mk, mv = norm_matmul(mem_prompt.reshape(bp * m_len, d), g_mem[layer], w_xkv[layer].astype(BF16),
                             ((xw, mem_outs), (xw, mem_outs)))
        p_out["mk"].append(mk.reshape(bp, m_len, h_x, hd_x)); p_out["mv"].append(mv.reshape(bp, m_len, h_x, hd_x))
        post_w = (g_cross[layer], w_xq[layer].astype(BF16))
        mlp_w = (w_xo[layer].astype(BF16), g_mlp[layer], w_up[layer].astype(BF16), w_down[layer].astype(BF16), h_x,
                 g_final if layer == depth - 1 else None)
        xp = post_mixer(xp, *mix_p, *post_w, mk.reshape(bp, m_len * h_x, hd_x), mv.reshape(bp, m_len * h_x, hd_x),
                        *mlp_w)
        xs = post_mixer(xs, *mix_s, *post_w, cache_mem_k[layer].reshape(bs, m_len * h_x, hd_x),
                        cache_mem_v[layer].reshape(bs, m_len * h_x, hd_x), *mlp_w)
    st = jnp.stack
    return (xp, xs,
            st(p_out["s"]), st(p_out["c"]), st(p_out["fk"]), st(p_out["fv"]), st(p_out["fl"]),
            st(p_out["wk"]), st(p_out["wv"]), st(p_out["mk"]), st(p_out["mv"]),
            st(s_out["s"]), st(s_out["c"]), st(s_out["fk"]), st(s_out["fv"]), st(s_out["fl"]),
            st(s_out["wk"]), st(s_out["wv"]))
```

```python
import functools
import math

import jax
import jax.numpy as jnp
from jax import lax
from jax.experimental import pallas as pl
from jax.experimental.pallas import tpu as pltpu

F32 = jnp.float32
BF16 = jnp.bfloat16
LOG2E = math.log2(math.e)
EPS = 1e-6
NEG = -1e30

CHUNK = 64
CONV_W = 4
WINDOW = 128
LANE = 128
SUBLANE = 8
VMEM_BYTES = 64 * 1024 * 1024


def _params(semantics, vmem_bytes):
    limit = int(min(max(vmem_bytes * 1.25, 16 * 1024 * 1024), VMEM_BYTES - 8 * 1024 * 1024))
    return pltpu.CompilerParams(dimension_semantics=semantics, vmem_limit_bytes=limit)


def _dot(a, b):
    return jnp.dot(a, b, preferred_element_type=F32)


def _dot_nt(a, b):
    return lax.dot_general(a, b, (((1,), (1,)), ((), ())), preferred_element_type=F32)


def _dot_tn(a, b):
    return lax.dot_general(a, b, (((0,), (0,)), ((), ())), preferred_element_type=F32)


def _rms(x, g):
    return x * lax.rsqrt(jnp.mean(x * x, axis=-1, keepdims=True) + EPS) * g


def _sigmoid(x):
    return 1.0 / (1.0 + jnp.exp2(x * (-LOG2E)))


def _softplus(x):
    return jnp.maximum(x, 0.0) + jnp.log(1.0 + jnp.exp(-jnp.abs(x)))


def _token_tile(t, cap):
    tm = min(t, cap)
    assert t % tm == 0
    return tm


def _norm_matmul_body(*refs, groups, gated, col_chunk):
    x_ref, g_ref, w_ref = refs[:3]
    refs = refs[3:]
    if gated:
        gp_ref, refs = refs[0], refs[1:]
    x = x_ref[...]
    yb = _rms(x, g_ref[...]).astype(BF16)
    chunks, off, k = [], 0, 0
    for gi, (n, outs) in enumerate(groups):
        chunks += [(gi, n, outs, k, off, c, min(col_chunk, n - c)) for c in range(0, n, col_chunk)]
        off += n
        k += len(outs)

    def project(chunk):
        _, _, _, _, off, c, cw = chunk
        return _dot(yb, w_ref[:, off + c:off + c + cw])

    r_next = project(chunks[0])
    for idx, (gi, n, outs, k, off, c, cw) in enumerate(chunks):
        r = r_next
        if idx + 1 < len(chunks):
            r_next = project(chunks[idx + 1])
        if gated and gi == len(groups) - 1:
            rb = r + gp_ref[0:1, :]
            lane = lax.broadcasted_iota(jnp.int32, rb.shape, 1)
            e = jnp.exp(-jnp.abs(rb))
            l1p = jnp.log(1.0 + e)
            sig = _sigmoid(rb)
            gdec = -jnp.exp(gp_ref[1:2, :]) * (jnp.maximum(rb, 0.0) + l1p)
            logsig = jnp.minimum(rb, 0.0) - l1p
            r = jnp.where(lane < 4, sig, jnp.where(lane < 8, gdec, jnp.where(lane < 12, logsig, 0.0)))
        for j, (dtype, scale, heads) in enumerate(outs):
            o_ref = refs[k + j]
            rr = (r if scale == 1.0 else r * scale).astype(dtype)
            if heads == 1:
                o_ref[:, c:c + cw] = rr
            else:
                hw = n // heads
                tm = x_ref.shape[0]
                for h in range(c // hw, (c + cw) // hw):
                    o_ref[pl.ds(h, tm, stride=heads), :] = rr[:, h * hw - c:(h + 1) * hw - c]


def norm_matmul(x, g, w, groups, gate_params=None, tile_cap=1024):
    t, d = x.shape
    tm = _token_tile(t, tile_cap)
    gated = gate_params is not None
    stacked = w.ndim == 3
    sets = w.shape[0] if stacked else 1
    n_total = w.shape[-1]
    in_specs = [pl.BlockSpec((tm, d), lambda s, i: (i, 0)),
                pl.BlockSpec((None, 1, d), lambda s, i: (s, 0, 0)),
                pl.BlockSpec((None, d, n_total), lambda s, i: (s, 0, 0), pipeline_mode=pl.Buffered(1))]
    args = [x, g.reshape(sets, 1, d), w.reshape(sets, d, n_total)]
    if gated:
        in_specs.append(pl.BlockSpec(gate_params.shape, lambda s, i: (0, 0)))
        args.append(gate_params)
    out_shape, out_specs, out_bytes = [], [], 0
    for n, outs in groups:
        for dtype, _, heads in outs:
            out_shape.append(jax.ShapeDtypeStruct((sets, t * heads, n // heads), dtype))
            out_specs.append(pl.BlockSpec((None, tm * heads, n // heads), lambda s, i: (s, i, 0)))
            out_bytes += tm * n * jnp.dtype(dtype).itemsize
    vmem = 2 * (tm * d * 4 + out_bytes) + d * n_total * 2 + tm * d * 8
    outs = pl.pallas_call(
        functools.partial(_norm_matmul_body, groups=groups, gated=gated, col_chunk=512),
        grid=(sets, t // tm), in_specs=in_specs, out_specs=out_specs, out_shape=out_shape,
        compiler_params=_params(("parallel", "parallel"), vmem), name="norm_matmul")(*args)
    return outs if stacked else [o[0] for o in outs]


def _post_body(*refs, n_in, heads, hd, ff_chunk, final_norm, col_chunk):
    x_ref = refs[0]
    a_refs, w_refs = refs[1:1 + n_in], refs[1 + n_in:1 + 2 * n_in]
    gc_ref, wq_ref, mk_ref, mv_ref, wo_ref, gm_ref, wu_ref, wd_ref = refs[1 + 2 * n_in:9 + 2 * n_in]
    gf_ref = refs[9 + 2 * n_in] if final_norm else None
    o_ref = refs[-1]
    bt, tm, d = x_ref.shape
    rows = bt * tm
    mem_len = mk_ref.shape[1] // heads
    x0 = x_ref[...].reshape(rows, d)
    a_vals = [a[...].reshape(rows, a.shape[-1]).astype(BF16) for a in a_refs]
    x1 = jnp.concatenate(
        [x0[:, c:c + col_chunk] + sum(_dot(a, w[:, c:c + col_chunk]) for a, w in zip(a_vals, w_refs))
         for c in range(0, d, col_chunk)], axis=-1)
    hb = _rms(x1, gc_ref[...]).astype(BF16)
    q = (_dot(hb, wq_ref[...]) * (hd ** -0.5)).astype(BF16)
    att = []
    for bi in range(bt):
        outs = []
        for h in range(heads):
            mem_rows = pl.ds(h, mem_len, stride=heads)
            s = _dot_nt(q[bi * tm:(bi + 1) * tm, h * hd:(h + 1) * hd], mk_ref[bi, mem_rows, :].astype(BF16))
            m = jnp.max(s, axis=-1, keepdims=True)
            p = jnp.exp(s - m)
            l = jnp.sum(p, axis=-1, keepdims=True)
            outs.append((_dot(p.astype(BF16), mv_ref[bi, mem_rows, :].astype(BF16)) / l).astype(BF16))
        att.append(jnp.concatenate(outs, axis=-1))
    x2 = x1 + _dot(jnp.concatenate(att, axis=0), wo_ref[...])
    hb = _rms(x2, gm_ref[...]).astype(BF16)
    acc = x2
    for c in range(0, wu_ref.shape[1], ff_chunk):
        u = jnp.maximum(_dot(hb, wu_ref[:, c:c + ff_chunk]), 0.0)
        acc = acc + _dot((u * u).astype(BF16), wd_ref[c:c + ff_chunk, :])
    if final_norm:
        acc = _rms(acc, gf_ref[...])
    o_ref[...] = acc.reshape(bt, tm, d)


def post_mixer(x, a_list, w_list, g_cross, wq, mk, mv, wo, g_mlp, w_up, w_down, heads, g_final=None, tile_cap=512):
    b, l, d = x.shape
    mh, hd = mk.shape[1], mk.shape[2]
    m, hw = mh // heads, hd * heads
    tm = _token_tile(l, tile_cap)
    bt = math.gcd(b, max(1, tile_cap // tm))
    rows = bt * tm
    n_in = len(a_list)
    final_norm = g_final is not None
    once = pl.Buffered(1)

    def const(shape):
        return pl.BlockSpec(shape, lambda i, j: (0,) * len(shape), pipeline_mode=once)

    in_specs = [pl.BlockSpec((bt, tm, d), lambda i, j: (i, j, 0))]
    in_specs += [pl.BlockSpec((bt, tm, a.shape[2]), lambda i, j: (i, j, 0)) for a in a_list]
    in_specs += [const(w.shape) for w in w_list]
    in_specs += [const((1, d)), const(wq.shape),
                 pl.BlockSpec((bt, mh, hd), lambda i, j: (i, 0, 0)),
                 pl.BlockSpec((bt, mh, hd), lambda i, j: (i, 0, 0)),
                 const(wo.shape), const((1, d)), const(w_up.shape), const(w_down.shape)]
    args = [x, *a_list, *w_list, g_cross.reshape(1, d), wq, mk, mv, wo, g_mlp.reshape(1, d), w_up, w_down]
    if final_norm:
        in_specs.append(const((1, d)))
        args.append(g_final.reshape(1, d))
    weights = sum(w.size for w in w_list) + wq.size + wo.size + w_up.size + w_down.size
    vmem = (weights * 2 + 2 * (2 * rows * d * 4 + 2 * bt * m * hw * 4
                               + sum(rows * a.shape[2] * a.dtype.itemsize for a in a_list))
            + rows * (d * 16 + hw * 8 + m * 8 + 512 * 8))
    return pl.pallas_call(
        functools.partial(_post_body, n_in=n_in, heads=heads, hd=hw // heads, ff_chunk=512,
                          final_norm=final_norm, col_chunk=512),
        grid=(b // bt, l // tm), in_specs=in_specs,
        out_specs=pl.BlockSpec((bt, tm, d), lambda i, j: (i, j, 0)),
        out_shape=jax.ShapeDtypeStruct((b, l, d), F32),
        compiler_params=_params(("parallel", "parallel"), vmem), name="post_mixer")(*args)


def _cumsum_body(x_ref, o_ref, *, minus_total, scale):
    r, l = x_ref.shape
    row = lax.broadcasted_iota(jnp.int32, (LANE, LANE), 0)
    col = lax.broadcasted_iota(jnp.int32, (LANE, LANE), 1)
    upper = (row <= col).astype(BF16)
    carry = jnp.zeros((r, 1), F32)
    for c in range(0, l, LANE):
        x = x_ref[:, c:c + LANE]
        x1 = x.astype(BF16)
        res = x - x1.astype(F32)
        x2 = res.astype(BF16)
        x3 = (res - x2.astype(F32)).astype(BF16)
        y3 = _dot(jnp.concatenate([x1, x2, x3], axis=0), upper)
        y = y3[:r] + y3[r:2 * r] + y3[2 * r:] + carry
        o_ref[:, c:c + LANE] = y
        carry = y[:, LANE - 1:LANE]
    total = carry if minus_total else 0.0
    o_ref[...] = (o_ref[...] - total) * scale


def cumsum_rows(x, scale, minus_total=False):
    r, l = x.shape
    assert l % LANE == 0 and r % SUBLANE == 0
    return pl.pallas_call(
        functools.partial(_cumsum_body, minus_total=minus_total, scale=scale),
        grid=(1,), in_specs=[pl.BlockSpec((r, l), lambda i: (0, 0))],
        out_specs=pl.BlockSpec((r, l), lambda i: (0, 0)),
        out_shape=jax.ShapeDtypeStruct((r, l), F32),
        compiler_params=_params(("arbitrary",), 4 * r * l * 4), name="cumsum_rows")(x)


def _fox_body(*refs, tq, tkp, n_past, nsub, heads):
    if n_past:
        q_ref, k_ref, v_ref, fk_ref, kp_ref, vp_ref, fp_ref, o_ref = refs
    else:
        q_ref, k_ref, v_ref, fk_ref, o_ref = refs
    i = pl.program_id(2)
    sub = tq // nsub
    hd = q_ref.shape[-1]
    qs = [q_ref[c * sub:(c + 1) * sub, :] for c in range(nsub)]

    def update(carry, ss, vbs):
        ms, ls, accs = carry[0::3], carry[1::3], carry[2::3]
        m_new = [jnp.maximum(m, jnp.max(s, axis=-1, keepdims=True)) for m, s in zip(ms, ss)]
        alpha = [jnp.exp2(m - mn) for m, mn in zip(ms, m_new)]
        ps = [jnp.exp2(s - mn) for s, mn in zip(ss, m_new)]
        ls = [a * l + jnp.sum(p, axis=-1, keepdims=True) for a, l, p in zip(alpha, ls, ps)]
        pv = [_dot(p.astype(BF16), vb) for p, vb in zip(ps, vbs)]
        accs = [a * acc + x for a, acc, x in zip(alpha, accs, pv)]
        out = []
        for c in range(len(ss)):
            out += [m_new[c], ls[c], accs[c]]
        return tuple(out)

    carry = (jnp.full((sub, 1), NEG, F32), jnp.zeros((sub, 1), F32), jnp.zeros((sub, hd), F32)) * nsub

    if n_past:
        def past_step(j, carry):
            rows = pl.ds(j * tkp * heads + pl.program_id(1), tkp, stride=heads)
            kb = kp_ref[rows, :].astype(BF16)
            vb = vp_ref[rows, :].astype(BF16)
            fk = fp_ref[pl.ds(j, 1), :]
            return update(carry, [_dot_nt(q, kb) - fk for q in qs], [vb] * nsub)
        carry = lax.fori_loop(0, n_past, past_step, carry)

    kmul = 2 if nsub % 2 == 0 else 1
    tk = kmul * sub

    def self_step(j, carry):
        start = pl.multiple_of(j * tk, tk)
        kb = k_ref[pl.ds(start, tk), :]
        vb = v_ref[pl.ds(start, tk), :]
        fk = jnp.concatenate([fk_ref[pl.ds(j * kmul + t, 1), :] for t in range(kmul)], axis=1)[:, :tk]
        return update(carry, [_dot_nt(q, kb) - fk for q in qs], [vb] * nsub)
    carry = lax.fori_loop(0, i * (nsub // kmul), self_step, carry)

    start = pl.multiple_of(i * tq, tq)
    fks = [fk_ref[pl.ds(i * nsub + c, 1), :] for c in range(nsub)]
    ss, vbs = [], []
    for c in range(nsub):
        w = (c + 1) * sub
        fk = fks[0][:, :sub] if c == 0 else jnp.concatenate(fks[:c + 1], axis=1)
        s = _dot_nt(qs[c], k_ref[pl.ds(start, w), :]) - fk
        row = lax.broadcasted_iota(jnp.int32, (sub, w), 0) + c * sub
        col = lax.broadcasted_iota(jnp.int32, (sub, w), 1)
        ss.append(jnp.where(col <= row, s, NEG))
        vbs.append(v_ref[pl.ds(start, w), :])
    carry = update(carry, ss, vbs)
    for c in range(nsub):
        o_ref[c * sub:(c + 1) * sub, :] = (carry[3 * c + 2] / carry[3 * c + 1]).astype(o_ref.dtype)


def fox_attention(q, k, v, f_new, heads, past=None, tq_cap=1024, sub_cap=256, score_tile=64 * 1024):
    b, l, hw = q.shape
    hd = hw // heads
    tq = min(l, tq_cap)
    nq = l // tq
    nsub = max(1, tq // sub_cap)
    p_len = 0 if past is None else past[0].shape[1] // heads
    tkp = min(p_len, max(sub_cap, score_tile // tq))
    sub = tq // nsub
    nk = l // sub
    wk = -(-sub // LANE) * LANE
    if wk != sub:
        assert nk == 1
        f_new = jnp.pad(f_new, ((0, 0), (0, 0), (0, wk - sub)))
    in_specs = [pl.BlockSpec((None, tq, hd), lambda bi, h, i: (bi, i, h)),
                pl.BlockSpec((None, l, hd), lambda bi, h, i: (bi, 0, h)),
                pl.BlockSpec((None, l, hd), lambda bi, h, i: (bi, 0, h)),
                pl.BlockSpec((None, None, nk, wk), lambda bi, h, i: (bi, h, 0, 0))]
    args = [q, k, v, f_new.reshape(b, heads, nk, wk)]
    n_past = 0
    vmem = 2 * (2 * tq * hd * 2 + 2 * l * hd * 2) + 8 * tq * max(tq, tkp) * 4
    if past is not None:
        kp, vp, fp = past
        n_past = p_len // tkp
        in_specs += [pl.BlockSpec((None, p_len * heads, hd), lambda bi, h, i: (bi, 0, 0)),
                     pl.BlockSpec((None, p_len * heads, hd), lambda bi, h, i: (bi, 0, 0)),
                     pl.BlockSpec((None, None, n_past, tkp), lambda bi, h, i: (bi, h, 0, 0))]
        args += [kp, vp, fp.reshape(b, heads, n_past, tkp)]
        vmem += 2 * 2 * p_len * heads * hd * 4
    return pl.pallas_call(
        functools.partial(_fox_body, tq=tq, tkp=tkp, n_past=n_past, nsub=nsub, heads=heads),
        grid=(b, heads, nq), in_specs=in_specs,
        out_specs=pl.BlockSpec((None, tq, hd), lambda bi, h, i: (bi, i, h)),
        out_shape=jax.ShapeDtypeStruct((b, l, hw), BF16),
        compiler_params=_params(("parallel", "parallel", "arbitrary"), vmem), name="fox_attention")(*args)


def _gdn_body(qkv_ref, z_ref, gates_ref, cprev_ref, cw_ref, s0_ref, og_ref, o_ref, st_ref, ybuf_ref,
              *, c, nc, heads, dk):
    pad = SUBLANE
    r = c * nc
    hw = heads * dk

    @pl.when(pl.program_id(1) == 0)
    def _():
        ybuf_ref[0:pad, :] = cprev_ref[...]
        st_ref[...] = s0_ref[...]

    ybuf_ref[pad:pad + r, :] = qkv_ref[...]

    row = lax.broadcasted_iota(jnp.int32, (c, c), 0)
    col = lax.broadcasted_iota(jnp.int32, (c, c), 1)
    incl = row >= col
    strict = row > col
    tril = incl.astype(BF16)
    n_double = int(math.log2(c)) - 1
    og = og_ref[...]
    states = [st_ref[h] for h in range(heads)]

    def prep(g):
        r0 = g["ci"] * c
        conv = qkv_ref[r0:r0 + c, :] * cw_ref[CONV_W - 1:CONV_W, :]
        for k in range(1, CONV_W):
            conv = conv + ybuf_ref[pad + r0 - k:pad + r0 - k + c, :] * cw_ref[CONV_W - 1 - k:CONV_W - k, :]
        act = conv * _sigmoid(conv)
        gs = gates_ref[r0:r0 + c, :]
        g1 = gs.astype(BF16)
        res = gs - g1.astype(F32)
        g2 = res.astype(BF16)
        g3 = (res - g2.astype(F32)).astype(BF16)
        cs3 = _dot(tril, jnp.concatenate([g1, g2, g3], axis=-1))
        cs = cs3[:, :LANE] + cs3[:, LANE:2 * LANE] + cs3[:, 2 * LANE:]
        cs_t = jnp.concatenate([cs, jnp.zeros((LANE - c, LANE), F32)], axis=0).T
        g["chains"] = []
        for h in range(heads):
            qh = act[:, h * dk:(h + 1) * dk]
            kh = act[:, hw + h * dk:hw + (h + 1) * dk]
            vh = act[:, 2 * hw + h * dk:2 * hw + (h + 1) * dk]
            qh = qh * (lax.rsqrt(jnp.sum(qh * qh, axis=-1, keepdims=True) + EPS) * (dk ** -0.5))
            kh = kh * lax.rsqrt(jnp.sum(kh * kh, axis=-1, keepdims=True) + EPS)
            beta = gs[:, h:h + 1]
            gc = cs[:, heads + h:heads + h + 1]
            gc_row = cs_t[heads + h:heads + h + 1, :c]
            kb = kh * beta
            egc = jnp.exp(gc)
            g_last = gc[c - 1:c, :]
            g["chains"].append(dict(
                decay=jnp.exp(jnp.where(incl, gc - gc_row, NEG)),
                kq=jnp.concatenate([kb, qh], axis=0).astype(BF16), khb=kh.astype(BF16),
                rhs=jnp.concatenate([vh * beta, kb * egc], axis=-1), qg=qh * egc,
                kd=(kh * jnp.exp(g_last - gc)).astype(BF16), ds=jnp.exp(g_last)))

    def gram(g):
        for ch in g["chains"]:
            ch["a1"] = _dot_nt(ch["kq"], ch["khb"])

    def masks(g):
        for ch in g["chains"]:
            lmat = jnp.where(strict, ch["a1"][:c] * ch["decay"], 0.0)
            ch["attn"] = jnp.where(incl, ch["a1"][c:] * ch["decay"], 0.0).astype(BF16)
            ch["toff"] = -lmat
            ch["lb"] = lmat.astype(BF16)
        for ch in g["chains"]:
            ch["pw"] = _dot(ch["lb"], ch["lb"])

    def double(last):
        def stage(g):
            for ch in g["chains"]:
                pwb = ch["pw"].astype(BF16)
                lhs = ch["toff"].astype(BF16) if last else jnp.concatenate([ch["toff"].astype(BF16), pwb], axis=0)
                ch["tp"] = _dot(lhs, pwb)
            for ch in g["chains"]:
                ch["toff"] = ch["toff"] + ch["pw"] + ch["tp"][:c]
                if not last:
                    ch["pw"] = ch["tp"][c:]
        return stage

    def solve(g):
        for ch in g["chains"]:
            ch["uw"] = _dot(ch["toff"].astype(BF16), ch["rhs"].astype(BF16))
        for ch in g["chains"]:
            uw = ch["rhs"] + ch["uw"]
            ch["u"] = uw[:, :dk]
            ch["wq"] = jnp.concatenate([uw[:, dk:], ch["qg"]], axis=0).astype(BF16)

    def recur(g):
        group, r0 = g["chains"], g["ci"] * c
        t1 = [_dot(ch["wq"], s.astype(BF16)) for ch, s in zip(group, states)]
        vnb = [(ch["u"] - t[:c]).astype(BF16) for ch, t in zip(group, t1)]
        o2 = [_dot(ch["attn"], v) for ch, v in zip(group, vnb)]
        sd = [_dot_tn(ch["kd"], v) for ch, v in zip(group, vnb)]
        states[:] = [s * ch["ds"] + d for ch, s, d in zip(group, states, sd)]
        for h in range(heads):
            o = t1[h][c:] + o2[h]
            zh = z_ref[r0:r0 + c, h * dk:(h + 1) * dk]
            o_ref[r0:r0 + c, h * dk:(h + 1) * dk] = (_rms(o, og) * (zh * _sigmoid(zh))).astype(o_ref.dtype)

    stages = [prep, gram, masks] + [double(s == n_double - 1) for s in range(n_double)] + [solve, recur]
    groups = [dict(ci=ci) for ci in range(nc)]
    for t in range(len(stages) + nc - 1):
        for ci in range(nc):
            if 0 <= t - ci < len(stages):
                stages[t - ci](groups[ci])

    for h in range(heads):
        st_ref[h] = states[h]
    ybuf_ref[0:pad, :] = ybuf_ref[r:r + pad, :]


def gated_delta_net(qkv, z, gates, conv_prev, conv_w, s0, onorm_g, heads, dk, chunks_per_step=8):
    b, l, cd = qkv.shape
    c = min(CHUNK, l)
    nc = min(chunks_per_step, l // c)
    r = c * nc
    hw = heads * dk
    cprev = jnp.pad(conv_prev, ((0, 0), (SUBLANE - (CONV_W - 1), 0), (0, 0)))
    cw = jnp.pad(conv_w, ((0, SUBLANE - CONV_W), (0, 0)))
    vmem = 2 * (r * cd * 4 + r * hw * 4 + r * LANE * 4 + 2 * SUBLANE * cd * 4 + 2 * heads * dk * dk * 4
                + r * hw * 2) + (r + SUBLANE) * cd * 4 * 4
    return pl.pallas_call(
        functools.partial(_gdn_body, c=c, nc=nc, heads=heads, dk=dk),
        grid=(b, l // r),
        in_specs=[pl.BlockSpec((None, r, cd), lambda i, n: (i, n, 0)),
                  pl.BlockSpec((None, r, hw), lambda i, n: (i, n, 0)),
                  pl.BlockSpec((None, r, LANE), lambda i, n: (i, n, 0)),
                  pl.BlockSpec((None, SUBLANE, cd), lambda i, n: (i, 0, 0)),
                  pl.BlockSpec((SUBLANE, cd), lambda i, n: (0, 0)),
                  pl.BlockSpec((None, heads, dk, dk), lambda i, n: (i, 0, 0, 0)),
                  pl.BlockSpec((1, dk), lambda i, n: (0, 0))],
        out_specs=[pl.BlockSpec((None, r, hw), lambda i, n: (i, n, 0)),
                   pl.BlockSpec((None, heads, dk, dk), lambda i, n: (i, 0, 0, 0))],
        out_shape=[jax.ShapeDtypeStruct((b, l, hw), BF16),
                   jax.ShapeDtypeStruct((b, heads, dk, dk), F32)],
        scratch_shapes=[pltpu.VMEM((r + SUBLANE, cd), F32)],
        compiler_params=_params(("parallel", "arbitrary"), vmem), name="gated_delta_net")(
            qkv, z, gates, cprev, cw, s0, onorm_g.reshape(1, dk))


def _swa_body(*refs, n_seg, cq, nchunk, win, hkv, grp, hd, chunked):
    sink_ref, q_ref = refs[0], refs[1]
    k_refs, v_refs, o_ref = refs[2:2 + n_seg], refs[2 + n_seg:2 + 2 * n_seg], refs[2 + 2 * n_seg]
    kk = jnp.concatenate([r[...] for r in k_refs], axis=0).astype(BF16)
    vv = jnp.concatenate([r[...] for r in v_refs], axis=0).astype(BF16)
    row = lax.broadcasted_iota(jnp.int32, (cq, win), 0)
    col = lax.broadcasted_iota(jnp.int32, (cq, win), 1)
    dist = jnp.abs(row - col + WINDOW).astype(F32)
    hq = hkv * grp
    k_heads = [kk[:, j * hd:(j + 1) * hd] for j in range(hkv)]
    v_heads = [vv[:, j * hd:(j + 1) * hd] for j in range(hkv)]
    chains = []
    for c in range(nchunk):
        dist_c = dist
        if chunked and c * cq < WINDOW:
            first_valid = WINDOW - (pl.program_id(1) * nchunk + c) * cq
            dist_c = jnp.where(col >= first_valid, dist, -NEG)
        for j in range(hkv):
            q_stack = jnp.concatenate(
                [q_ref[c * cq:(c + 1) * cq, (j * grp + g) * hd:(j * grp + g + 1) * hd] for g in range(grp)], axis=0)
            chains.append(dict(c=c, j=j, dist=dist_c, q=q_stack,
                               k=k_heads[j][c * cq:c * cq + win, :], v=v_heads[j][c * cq:c * cq + win, :]))

    def scores(group):
        for ch in group:
            ch["s"] = _dot_nt(ch["q"], ch["k"])

    def softmax_pv(group):
        heads_of = [(ch, g, ch["j"] * grp + g) for ch in group for g in range(grp)]
        slopes = [LOG2E * 2.0 ** (-8.0 * (h + 1) / hq) for _, _, h in heads_of]
        sinks = [LOG2E * sink_ref[h] for _, _, h in heads_of]
        sg = [ch["s"][g * cq:(g + 1) * cq, :] - sl * ch["dist"] for (ch, g, _), sl in zip(heads_of, slopes)]
        m = [jnp.maximum(jnp.max(x, axis=-1, keepdims=True), sk) for x, sk in zip(sg, sinks)]
        p = [jnp.exp2(x - mm) for x, mm in zip(sg, m)]
        l = [jnp.sum(x, axis=-1, keepdims=True) + jnp.exp2(sk - mm) for x, mm, sk in zip(p, m, sinks)]
        pb = [x.astype(BF16) for x in p]
        for ci, ch in enumerate(group):
            ch["l"] = l[ci * grp:(ci + 1) * grp]
            ch["pv"] = _dot(jnp.concatenate(pb[ci * grp:(ci + 1) * grp], axis=0), ch["v"])

    def emit(group):
        for ch in group:
            c, j = ch["c"], ch["j"]
            outs = [(ch["pv"][g * cq:(g + 1) * cq, :] / ch["l"][g]).astype(o_ref.dtype) for g in range(grp)]
            o_ref[c * cq:(c + 1) * cq, j * grp * hd:(j + 1) * grp * hd] = jnp.concatenate(outs, axis=-1)

    gsz = 2
    groups = [chains[i:i + gsz] for i in range(0, len(chains), gsz)]
    scores(groups[0])
    for i, group in enumerate(groups):
        if i + 1 < len(groups):
            scores(groups[i + 1])
        softmax_pv(group)
        if i > 0:
            emit(groups[i - 1])
    emit(groups[-1])


def swa_attention(q, segs, sinks, hkv, grp, hd, chunked, chunks_per_step=8):
    b, l, qw = q.shape
    kw = hkv * hd
    if chunked:
        cq, nchunk, n_seg, win = CHUNK, chunks_per_step, 2, WINDOW + CHUNK
        rows = cq * nchunk
        assert l % rows == 0 and rows % WINDOW == 0
        k, v = segs
        k_specs = [pl.BlockSpec((None, WINDOW, kw), lambda i, n: (i, jnp.maximum(n * (rows // WINDOW) - 1, 0), 0)),
                   pl.BlockSpec((None, rows, kw), lambda i, n: (i, n, 0))]
        k_args, v_args = [k] * n_seg, [v] * n_seg
        key_rows = WINDOW + rows
    else:
        cq, nchunk, n_seg = l, 1, len(segs)
        rows = l
        k_specs = [pl.BlockSpec((None, kk.shape[1], kw), lambda i, n: (i, 0, 0)) for kk, _ in segs]
        k_args, v_args = [kk for kk, _ in segs], [vv for _, vv in segs]
        key_rows = win = sum(kk.shape[1] for kk in k_args)
    vmem = 2 * (2 * rows * qw * 2 + 2 * key_rows * kw * 4) + 3 * nchunk * hkv * grp * cq * 256 * 4
    return pl.pallas_call(
        functools.partial(_swa_body, n_seg=n_seg, cq=cq, nchunk=nchunk, win=win, hkv=hkv, grp=grp, hd=hd,
                          chunked=chunked),
        grid=(b, l // rows),
        in_specs=[pl.BlockSpec(memory_space=pltpu.SMEM),
                  pl.BlockSpec((None, rows, qw), lambda i, n: (i, n, 0))] + k_specs + k_specs,
        out_specs=pl.BlockSpec((None, rows, qw), lambda i, n: (i, n, 0)),
        out_shape=jax.ShapeDtypeStruct((b, l, qw), BF16),
        compiler_params=_params(("parallel", "parallel"), vmem), name="swa_attention")(
            sinks, q, *k_args, *v_args)


def _ab_layer(x, conv_prev, s0, past, w_in, conv_w, a_log, dt_bias, onorm_g, f_bias, w_out, g_mix, dims):
    b, l, d = x.shape
    h_a, dk, h_b, hd_b = dims
    a_qk, a_v, b_w = h_a * dk, h_a * dk, h_b * hd_b
    conv_dim = 2 * a_qk + a_v
    o1 = conv_dim + a_v
    o2 = o1 + 2 * h_a
    small = jnp.concatenate([w_in[:, o1:o2], w_in[:, o2 + 3 * b_w:],
                             jnp.zeros((d, LANE - 2 * h_a - h_b), F32)], axis=1)
    w_pack = jnp.concatenate([w_in[:, :o1], w_in[:, o2:o2 + 3 * b_w], small], axis=1).astype(BF16)
    zeros4 = jnp.zeros((h_a,), F32)
    gp = jnp.zeros((SUBLANE, LANE), F32)
    gp = gp.at[0, :3 * h_a].set(jnp.concatenate([zeros4, dt_bias, f_bias]))
    gp = gp.at[1, h_a:2 * h_a].set(a_log)
    f32o, b16, cache = (F32, 1.0, 1), (BF16, 1.0, 1), (F32, 1.0, h_b)
    groups = ((conv_dim, (f32o,)), (a_v, (f32o,)), (b_w, ((BF16, LOG2E * hd_b ** -0.5, 1),)),
              (b_w, (cache, b16)), (b_w, (cache, b16)), (LANE, (f32o,)))
    qkv, z, qb, kb, kb16, vb, vb16, gates = norm_matmul(x.reshape(b * l, d), g_mix, w_pack, groups, gate_params=gp)
    r3 = lambda a: a.reshape(b, l, a.shape[-1])
    qkv3, gates3 = r3(qkv), r3(gates)
    oa, s_new = gated_delta_net(qkv3, r3(z), gates3, conv_prev, conv_w, s0, onorm_g, h_a, dk)
    conv_new = qkv3[:, l - (CONV_W - 1):, :]
    logf = gates3[:, :, 2 * h_a:2 * h_a + h_b]
    lf_rows = jnp.pad(logf.transpose(0, 2, 1).reshape(b * h_b, l), ((0, 0), (0, -l % LANE)))
    f_new = cumsum_rows(lf_rows, LOG2E)[:, :l].reshape(b, h_b, l)
    fox_past = None
    if past is not None:
        pk, pv, plf = past
        p_len = pk.shape[1]
        f_past = cumsum_rows(plf.transpose(0, 2, 1).reshape(b * h_b, p_len), LOG2E, minus_total=True)
        fox_past = (pk.reshape(b, p_len * h_b, hd_b), pv.reshape(b, p_len * h_b, hd_b), f_past.reshape(b, h_b, p_len))
    ob = fox_attention(r3(qb), r3(kb16), r3(vb16), f_new, h_b, past=fox_past)
    w_out16 = w_out.astype(BF16)
    mixed = ([oa, ob], [w_out16[:a_v], w_out16[a_v:]])
    return (mixed, s_new, conv_new, kb.reshape(b, l, h_b, hd_b), vb.reshape(b, l, h_b, hd_b), logf)


def _c_layer(x, past, w_in, sinks, w_out, g_mix, dims):
    b, l, d = x.shape
    hq, hkv, hd = dims
    grp = hq // hkv
    f32o = (F32, 1.0, 1)
    q, k, v = norm_matmul(x.reshape(b * l, d), g_mix, w_in.astype(BF16),
                          ((hq * hd, ((BF16, LOG2E * hd ** -0.5, 1),)), (hkv * hd, (f32o,)), (hkv * hd, (f32o,))))
    q3, k3, v3 = q.reshape(b, l, hq * hd), k.reshape(b, l, hkv * hd), v.reshape(b, l, hkv * hd)
    if past is None:
        o = swa_attention(q3, (k3, v3), sinks, hkv, grp, hd, chunked=True)
        new_k, new_v = k3[:, l - WINDOW:], v3[:, l - WINDOW:]
    else:
        pk, pv = past
        w_len = pk.shape[1]
        o = swa_attention(q3, ((pk.reshape(b, w_len, hkv * hd), pv.reshape(b, w_len, hkv * hd)), (k3, v3)),
                          sinks, hkv, grp, hd, chunked=False)
        new_k, new_v = k3, v3
    return ([o], [w_out.astype(BF16)]), new_k.reshape(b, -1, hkv, hd), new_v.reshape(b, -1, hkv, hd)


def kernel(x_prompt, mem_prompt, x_sample, state_gdn, state_gdn_conv, cache_fox_k, cache_fox_v, cache_fox_logf, cache_swa_k, cache_swa_v, cache_mem_k, cache_mem_v, g_mix, w_in_ab, conv_w_a, a_log_a, dt_bias_a, onorm_g_a, f_bias_b, w_out_ab, w_in_c, sinks_c, w_out_c, g_mem, w_xkv, g_cross, w_xq, w_xo, g_mlp, w_up, w_down, g_final):
    depth = g_mix.shape[0]
    bp, lp, d = x_prompt.shape
    bs, ls, _ = x_sample.shape
    h_a, dk = state_gdn.shape[2], state_gdn.shape[3]
    h_b, hd_b = cache_fox_k.shape[3], cache_fox_k.shape[4]
    hkv, hd_c = cache_swa_k.shape[3], cache_swa_k.shape[4]
    hq = sinks_c.shape[1]
    m_len, h_x, hd_x = cache_mem_k.shape[2], cache_mem_k.shape[3], cache_mem_k.shape[4]
    xw = h_x * hd_x
    conv_dim = state_gdn_conv.shape[-1]

    xp, xs = x_prompt, x_sample
    mem_outs = ((F32, 1.0, h_x),)
    mem_k, mem_v = norm_matmul(mem_prompt.reshape(bp * m_len, d), g_mem, w_xkv.astype(BF16),
                               ((xw, mem_outs), (xw, mem_outs)))
    p_out = {k: [] for k in ("s", "c", "fk", "fv", "fl", "wk", "wv")}
    s_out = {k: [] for k in ("s", "c", "fk", "fv", "fl", "wk", "wv")}
    for layer in range(depth):
        i = layer // 2
        if layer % 2 == 0:
            wts = (w_in_ab[i], conv_w_a[i], a_log_a[i], dt_bias_a[i], onorm_g_a[i], f_bias_b[i], w_out_ab[i],
                   g_mix[layer], (h_a, dk, h_b, hd_b))
            mix_p, st, cv, fk, fv, fl = _ab_layer(xp, jnp.zeros((bp, CONV_W - 1, conv_dim), F32),
                                                  jnp.zeros((bp, h_a, dk, dk), F32), None, *wts)
            for key, val in zip(("s", "c", "fk", "fv", "fl"), (st, cv, fk, fv, fl)):
                p_out[key].append(val)
            mix_s, st, cv, fk, fv, fl = _ab_layer(xs, state_gdn_conv[i], state_gdn[i],
                                                  (cache_fox_k[i], cache_fox_v[i], cache_fox_logf[i]), *wts)
            for key, val in zip(("s", "c", "fk", "fv", "fl"), (st, cv, fk, fv, fl)):
                s_out[key].append(val)
        else:
            wts = (w_in_c[i], sinks_c[i], w_out_c[i], g_mix[layer], (hq, hkv, hd_c))
            mix_p, wk, wv = _c_layer(xp, None, *wts)
            p_out["wk"].append(wk); p_out["wv"].append(wv)
            mix_s, wk, wv = _c_layer(xs, (cache_swa_k[i], cache_swa_v[i]), *wts)
            s_out["wk"].append(wk); s_out["wv"].append(wv)
        mk, mv = mem_k[layer], mem_v[layer]
        post_w = (g_cross[layer], w_xq[layer].astype(BF16))
        mlp_w = (w_xo[layer].astype(BF16), g_mlp[layer], w_up[layer].astype(BF16), w_down[layer].astype(BF16), h_x,
                 g_final if layer == depth - 1 else None)
        xp = post_mixer(xp, *mix_p, *post_w, mk.reshape(bp, m_len * h_x, hd_x), mv.reshape(bp, m_len * h_x, hd_x),
                        *mlp_w)
        xs = post_mixer(xs, *mix_s, *post_w, cache_mem_k[layer].reshape(bs, m_len * h_x, hd_x),
                        cache_mem_v[layer].reshape(bs, m_len * h_x, hd_x), *mlp_w)
    st = jnp.stack
    return (xp, xs,
            st(p_out["s"]), st(p_out["c"]), st(p_out["fk"]), st(p_out["fv"]), st(p_out["fl"]),
            st(p_out["wk"]), st(p_out["wv"]),
            mem_k.reshape(depth, bp, m_len, h_x, hd_x), mem_v.reshape(depth, bp, m_len, h_x, hd_x),
            st(s_out["s"]), st(s_out["c"]), st(s_out["fk"]), st(s_out["fv"]), st(s_out["fl"]),
            st(s_out["wk"]), st(s_out["wv"]))
```

```python
import functools
import math

import jax
import jax.numpy as jnp
from jax import lax
from jax.experimental import pallas as pl
from jax.experimental.pallas import tpu as pltpu

F32 = jnp.float32
BF16 = jnp.bfloat16
LOG2E = math.log2(math.e)
EPS = 1e-6
NEG = -1e30

CHUNK = 64
CONV_W = 4
WINDOW = 128
LANE = 128
SUBLANE = 8
VMEM_BYTES = 64 * 1024 * 1024


def _params(semantics, vmem_bytes):
    limit = int(min(max(vmem_bytes * 1.25, 16 * 1024 * 1024), VMEM_BYTES - 8 * 1024 * 1024))
    return pltpu.CompilerParams(dimension_semantics=semantics, vmem_limit_bytes=limit)


def _dot(a, b):
    return jnp.dot(a, b, preferred_element_type=F32)


def _dot_nt(a, b):
    return lax.dot_general(a, b, (((1,), (1,)), ((), ())), preferred_element_type=F32)


def _dot_tn(a, b):
    return lax.dot_general(a, b, (((0,), (0,)), ((), ())), preferred_element_type=F32)


def _rms(x, g):
    return x * lax.rsqrt(jnp.mean(x * x, axis=-1, keepdims=True) + EPS) * g


def _sigmoid(x):
    return 1.0 / (1.0 + jnp.exp2(x * (-LOG2E)))


def _softplus(x):
    return jnp.maximum(x, 0.0) + jnp.log(1.0 + jnp.exp(-jnp.abs(x)))


def _token_tile(t, cap):
    tm = min(t, cap)
    assert t % tm == 0
    return tm


def _norm_matmul_body(*refs, groups, gated, col_chunk):
    x_ref, g_ref, w_ref = refs[:3]
    refs = refs[3:]
    if gated:
        gp_ref, refs = refs[0], refs[1:]
    x = x_ref[...]
    yb = _rms(x, g_ref[...]).astype(BF16)
    chunks, off, k = [], 0, 0
    for gi, (n, outs) in enumerate(groups):
        chunks += [(gi, n, outs, k, off, c, min(col_chunk, n - c)) for c in range(0, n, col_chunk)]
        off += n
        k += len(outs)

    def project(chunk):
        _, _, _, _, off, c, cw = chunk
        return _dot(yb, w_ref[:, off + c:off + c + cw])

    r_next = project(chunks[0])
    for idx, (gi, n, outs, k, off, c, cw) in enumerate(chunks):
        r = r_next
        if idx + 1 < len(chunks):
            r_next = project(chunks[idx + 1])
        if gated and gi == len(groups) - 1:
            rb = r + gp_ref[0:1, :]
            lane = lax.broadcasted_iota(jnp.int32, rb.shape, 1)
            e = jnp.exp(-jnp.abs(rb))
            l1p = jnp.log(1.0 + e)
            sig = _sigmoid(rb)
            gdec = -jnp.exp(gp_ref[1:2, :]) * (jnp.maximum(rb, 0.0) + l1p)
            logsig = jnp.minimum(rb, 0.0) - l1p
            r = jnp.where(lane < 4, sig, jnp.where(lane < 8, gdec, jnp.where(lane < 12, logsig, 0.0)))
        for j, (dtype, scale, heads) in enumerate(outs):
            o_ref = refs[k + j]
            rr = (r if scale == 1.0 else r * scale).astype(dtype)
            if heads == 1:
                o_ref[:, c:c + cw] = rr
            else:
                hw = n // heads
                tm = x_ref.shape[0]
                for h in range(c // hw, (c + cw) // hw):
                    o_ref[pl.ds(h, tm, stride=heads), :] = rr[:, h * hw - c:(h + 1) * hw - c]


def norm_matmul(x, g, w, groups, gate_params=None, tile_cap=1024):
    t, d = x.shape
    tm = _token_tile(t, tile_cap)
    gated = gate_params is not None
    stacked = w.ndim == 3
    sets = w.shape[0] if stacked else 1
    n_total = w.shape[-1]
    in_specs = [pl.BlockSpec((tm, d), lambda s, i: (i, 0)),
                pl.BlockSpec((None, 1, d), lambda s, i: (s, 0, 0)),
                pl.BlockSpec((None, d, n_total), lambda s, i: (s, 0, 0), pipeline_mode=pl.Buffered(1))]
    args = [x, g.reshape(sets, 1, d), w.reshape(sets, d, n_total)]
    if gated:
        in_specs.append(pl.BlockSpec(gate_params.shape, lambda s, i: (0, 0)))
        args.append(gate_params)
    out_shape, out_specs, out_bytes = [], [], 0
    for n, outs in groups:
        for dtype, _, heads in outs:
            out_shape.append(jax.ShapeDtypeStruct((sets, t * heads, n // heads), dtype))
            out_specs.append(pl.BlockSpec((None, tm * heads, n // heads), lambda s, i: (s, i, 0)))
            out_bytes += tm * n * jnp.dtype(dtype).itemsize
    vmem = 2 * (tm * d * 4 + out_bytes) + d * n_total * 2 + tm * d * 8
    outs = pl.pallas_call(
        functools.partial(_norm_matmul_body, groups=groups, gated=gated, col_chunk=512),
        grid=(sets, t // tm), in_specs=in_specs, out_specs=out_specs, out_shape=out_shape,
        compiler_params=_params(("parallel", "parallel"), vmem), name="norm_matmul")(*args)
    return outs if stacked else [o[0] for o in outs]


def _post_body(*refs, n_in, heads, hd, ff_chunk, final_norm, col_chunk):
    x_ref = refs[0]
    a_refs, w_refs = refs[1:1 + n_in], refs[1 + n_in:1 + 2 * n_in]
    gc_ref, wq_ref, mk_ref, mv_ref, wo_ref, gm_ref, wu_ref, wd_ref = refs[1 + 2 * n_in:9 + 2 * n_in]
    gf_ref = refs[9 + 2 * n_in] if final_norm else None
    o_ref = refs[-1]
    bt, tm, d = x_ref.shape
    rows = bt * tm
    mem_len = mk_ref.shape[1] // heads
    x0 = x_ref[...].reshape(rows, d)
    a_vals = [a[...].reshape(rows, a.shape[-1]).astype(BF16) for a in a_refs]
    x1 = jnp.concatenate(
        [x0[:, c:c + col_chunk] + sum(_dot(a, w[:, c:c + col_chunk]) for a, w in zip(a_vals, w_refs))
         for c in range(0, d, col_chunk)], axis=-1)
    hb = _rms(x1, gc_ref[...]).astype(BF16)
    q = (_dot(hb, wq_ref[...]) * (hd ** -0.5)).astype(BF16)
    att = []
    for bi in range(bt):
        outs = []
        for h in range(heads):
            mem_rows = pl.ds(h, mem_len, stride=heads)
            s = _dot_nt(q[bi * tm:(bi + 1) * tm, h * hd:(h + 1) * hd], mk_ref[bi, mem_rows, :].astype(BF16))
            m = jnp.max(s, axis=-1, keepdims=True)
            p = jnp.exp(s - m)
            l = jnp.sum(p, axis=-1, keepdims=True)
            outs.append((_dot(p.astype(BF16), mv_ref[bi, mem_rows, :].astype(BF16)) / l).astype(BF16))
        att.append(jnp.concatenate(outs, axis=-1))
    x2 = x1 + _dot(jnp.concatenate(att, axis=0), wo_ref[...])
    hb = _rms(x2, gm_ref[...]).astype(BF16)
    acc = x2
    for c in range(0, wu_ref.shape[1], ff_chunk):
        u = jnp.maximum(_dot(hb, wu_ref[:, c:c + ff_chunk]), 0.0)
        acc = acc + _dot((u * u).astype(BF16), wd_ref[c:c + ff_chunk, :])
    if final_norm:
        acc = _rms(acc, gf_ref[...])
    o_ref[...] = acc.reshape(bt, tm, d)


def post_mixer(x, a_list, w_list, g_cross, wq, mk, mv, wo, g_mlp, w_up, w_down, heads, g_final=None, tile_cap=512):
    b, l, d = x.shape
    mh, hd = mk.shape[1], mk.shape[2]
    m, hw = mh // heads, hd * heads
    tm = _token_tile(l, tile_cap)
    bt = math.gcd(b, max(1, tile_cap // tm))
    rows = bt * tm
    n_in = len(a_list)
    final_norm = g_final is not None
    once = pl.Buffered(1)

    def const(shape):
        return pl.BlockSpec(shape, lambda i, j: (0,) * len(shape), pipeline_mode=once)

    in_specs = [pl.BlockSpec((bt, tm, d), lambda i, j: (i, j, 0))]
    in_specs += [pl.BlockSpec((bt, tm, a.shape[2]), lambda i, j: (i, j, 0)) for a in a_list]
    in_specs += [const(w.shape) for w in w_list]
    in_specs += [const((1, d)), const(wq.shape),
                 pl.BlockSpec((bt, mh, hd), lambda i, j: (i, 0, 0)),
                 pl.BlockSpec((bt, mh, hd), lambda i, j: (i, 0, 0)),
                 const(wo.shape), const((1, d)), const(w_up.shape), const(w_down.shape)]
    args = [x, *a_list, *w_list, g_cross.reshape(1, d), wq, mk, mv, wo, g_mlp.reshape(1, d), w_up, w_down]
    if final_norm:
        in_specs.append(const((1, d)))
        args.append(g_final.reshape(1, d))
    weights = sum(w.size for w in w_list) + wq.size + wo.size + w_up.size + w_down.size
    vmem = (weights * 2 + 2 * (2 * rows * d * 4 + 2 * bt * m * hw * 4
                               + sum(rows * a.shape[2] * a.dtype.itemsize for a in a_list))
            + rows * (d * 16 + hw * 8 + m * 8 + 512 * 8))
    return pl.pallas_call(
        functools.partial(_post_body, n_in=n_in, heads=heads, hd=hw // heads, ff_chunk=512,
                          final_norm=final_norm, col_chunk=512),
        grid=(b // bt, l // tm), in_specs=in_specs,
        out_specs=pl.BlockSpec((bt, tm, d), lambda i, j: (i, j, 0)),
        out_shape=jax.ShapeDtypeStruct((b, l, d), F32),
        compiler_params=_params(("parallel", "parallel"), vmem), name="post_mixer")(*args)


def _cumsum_body(x_ref, o_ref, *, minus_total, scale):
    r, l = x_ref.shape
    row = lax.broadcasted_iota(jnp.int32, (LANE, LANE), 0)
    col = lax.broadcasted_iota(jnp.int32, (LANE, LANE), 1)
    upper = (row <= col).astype(BF16)
    carry = jnp.zeros((r, 1), F32)
    for c in range(0, l, LANE):
        x = x_ref[:, c:c + LANE]
        x1 = x.astype(BF16)
        res = x - x1.astype(F32)
        x2 = res.astype(BF16)
        x3 = (res - x2.astype(F32)).astype(BF16)
        y3 = _dot(jnp.concatenate([x1, x2, x3], axis=0), upper)
        y = y3[:r] + y3[r:2 * r] + y3[2 * r:] + carry
        o_ref[:, c:c + LANE] = y
        carry = y[:, LANE - 1:LANE]
    total = carry if minus_total else 0.0
    o_ref[...] = (o_ref[...] - total) * scale


def cumsum_rows(x, scale, minus_total=False):
    r, l = x.shape
    assert l % LANE == 0 and r % SUBLANE == 0
    return pl.pallas_call(
        functools.partial(_cumsum_body, minus_total=minus_total, scale=scale),
        grid=(1,), in_specs=[pl.BlockSpec((r, l), lambda i: (0, 0))],
        out_specs=pl.BlockSpec((r, l), lambda i: (0, 0)),
        out_shape=jax.ShapeDtypeStruct((r, l), F32),
        compiler_params=_params(("arbitrary",), 4 * r * l * 4), name="cumsum_rows")(x)


def _fox_body(*refs, tq, tkp, n_past, nsub, heads):
    if n_past:
        q_ref, k_ref, v_ref, fk_ref, kp_ref, vp_ref, fp_ref, o_ref = refs
    else:
        q_ref, k_ref, v_ref, fk_ref, o_ref = refs
    i = pl.program_id(2)
    sub = tq // nsub
    hd = q_ref.shape[-1]
    qs = [q_ref[c * sub:(c + 1) * sub, :] for c in range(nsub)]

    def update(carry, ss, vbs):
        ms, ls, accs = carry[0::3], carry[1::3], carry[2::3]
        m_new = [jnp.maximum(m, jnp.max(s, axis=-1, keepdims=True)) for m, s in zip(ms, ss)]
        alpha = [jnp.exp2(m - mn) for m, mn in zip(ms, m_new)]
        ps = [jnp.exp2(s - mn) for s, mn in zip(ss, m_new)]
        ls = [a * l + jnp.sum(p, axis=-1, keepdims=True) for a, l, p in zip(alpha, ls, ps)]
        pv = [_dot(p.astype(BF16), vb) for p, vb in zip(ps, vbs)]
        accs = [a * acc + x for a, acc, x in zip(alpha, accs, pv)]
        out = []
        for c in range(len(ss)):
            out += [m_new[c], ls[c], accs[c]]
        return tuple(out)

    carry = (jnp.full((sub, 1), NEG, F32), jnp.zeros((sub, 1), F32), jnp.zeros((sub, hd), F32)) * nsub

    if n_past:
        def past_step(j, carry):
            rows = pl.ds(j * tkp * heads + pl.program_id(1), tkp, stride=heads)
            kb = kp_ref[rows, :].astype(BF16)
            vb = vp_ref[rows, :].astype(BF16)
            fk = fp_ref[pl.ds(j, 1), :]
            return update(carry, [_dot_nt(q, kb) - fk for q in qs], [vb] * nsub)
        carry = lax.fori_loop(0, n_past, past_step, carry)

    kmul = 2 if nsub % 2 == 0 else 1
    tk = kmul * sub

    def self_step(j, carry):
        start = pl.multiple_of(j * tk, tk)
        kb = k_ref[pl.ds(start, tk), :]
        vb = v_ref[pl.ds(start, tk), :]
        fk = jnp.concatenate([fk_ref[pl.ds(j * kmul + t, 1), :] for t in range(kmul)], axis=1)[:, :tk]
        return update(carry, [_dot_nt(q, kb) - fk for q in qs], [vb] * nsub)
    carry = lax.fori_loop(0, i * (nsub // kmul), self_step, carry)

    start = pl.multiple_of(i * tq, tq)
    fks = [fk_ref[pl.ds(i * nsub + c, 1), :] for c in range(nsub)]
    ss, vbs = [], []
    for c in range(nsub):
        w = (c + 1) * sub
        fk = fks[0][:, :sub] if c == 0 else jnp.concatenate(fks[:c + 1], axis=1)
        s = _dot_nt(qs[c], k_ref[pl.ds(start, w), :]) - fk
        row = lax.broadcasted_iota(jnp.int32, (sub, w), 0) + c * sub
        col = lax.broadcasted_iota(jnp.int32, (sub, w), 1)
        ss.append(jnp.where(col <= row, s, NEG))
        vbs.append(v_ref[pl.ds(start, w), :])
    carry = update(carry, ss, vbs)
    for c in range(nsub):
        o_ref[c * sub:(c + 1) * sub, :] = (carry[3 * c + 2] / carry[3 * c + 1]).astype(o_ref.dtype)


def fox_attention(q, k, v, f_new, heads, past=None, tq_cap=1024, sub_cap=256, score_tile=64 * 1024):
    b, l, hw = q.shape
    hd = hw // heads
    tq = min(l, tq_cap)
    nq = l // tq
    nsub = max(1, tq // sub_cap)
    p_len = 0 if past is None else past[0].shape[1] // heads
    tkp = min(p_len, max(sub_cap, score_tile // tq))
    sub = tq // nsub
    nk = l // sub
    wk = -(-sub // LANE) * LANE
    if wk != sub:
        assert nk == 1
        f_new = jnp.pad(f_new, ((0, 0), (0, 0), (0, wk - sub)))
    in_specs = [pl.BlockSpec((None, tq, hd), lambda bi, h, i: (bi, i, h)),
                pl.BlockSpec((None, l, hd), lambda bi, h, i: (bi, 0, h)),
                pl.BlockSpec((None, l, hd), lambda bi, h, i: (bi, 0, h)),
                pl.BlockSpec((None, None, nk, wk), lambda bi, h, i: (bi, h, 0, 0))]
    args = [q, k, v, f_new.reshape(b, heads, nk, wk)]
    n_past = 0
    vmem = 2 * (2 * tq * hd * 2 + 2 * l * hd * 2) + 8 * tq * max(tq, tkp) * 4
    if past is not None:
        kp, vp, fp = past
        n_past = p_len // tkp
        in_specs += [pl.BlockSpec((None, p_len * heads, hd), lambda bi, h, i: (bi, 0, 0)),
                     pl.BlockSpec((None, p_len * heads, hd), lambda bi, h, i: (bi, 0, 0)),
                     pl.BlockSpec((None, None, n_past, tkp), lambda bi, h, i: (bi, h, 0, 0))]
        args += [kp, vp, fp.reshape(b, heads, n_past, tkp)]
        vmem += 2 * 2 * p_len * heads * hd * 4
    return pl.pallas_call(
        functools.partial(_fox_body, tq=tq, tkp=tkp, n_past=n_past, nsub=nsub, heads=heads),
        grid=(b, heads, nq), in_specs=in_specs,
        out_specs=pl.BlockSpec((None, tq, hd), lambda bi, h, i: (bi, i, h)),
        out_shape=jax.ShapeDtypeStruct((b, l, hw), BF16),
        compiler_params=_params(("parallel", "parallel", "arbitrary"), vmem), name="fox_attention")(*args)


def _gdn_body(qkv_ref, z_ref, gates_ref, cprev_ref, cw_ref, s0_ref, og_ref, o_ref, st_ref, ybuf_ref,
              *, c, nc, heads, dk):
    pad = SUBLANE
    r = c * nc
    hw = heads * dk

    @pl.when(pl.program_id(1) == 0)
    def _():
        ybuf_ref[0:pad, :] = cprev_ref[...]
        st_ref[...] = s0_ref[...]

    ybuf_ref[pad:pad + r, :] = qkv_ref[...]

    row = lax.broadcasted_iota(jnp.int32, (c, c), 0)
    col = lax.broadcasted_iota(jnp.int32, (c, c), 1)
    incl = row >= col
    strict = row > col
    tril = incl.astype(BF16)
    n_double = int(math.log2(c)) - 1
    og = og_ref[...]
    states = [st_ref[h] for h in range(heads)]

    def prep(g):
        r0 = g["ci"] * c
        conv = qkv_ref[r0:r0 + c, :] * cw_ref[CONV_W - 1:CONV_W, :]
        for k in range(1, CONV_W):
            conv = conv + ybuf_ref[pad + r0 - k:pad + r0 - k + c, :] * cw_ref[CONV_W - 1 - k:CONV_W - k, :]
        act = conv * _sigmoid(conv)
        gs = gates_ref[r0:r0 + c, :]
        g1 = gs.astype(BF16)
        res = gs - g1.astype(F32)
        g2 = res.astype(BF16)
        g3 = (res - g2.astype(F32)).astype(BF16)
        cs3 = _dot(tril, jnp.concatenate([g1, g2, g3], axis=-1))
        cs = cs3[:, :LANE] + cs3[:, LANE:2 * LANE] + cs3[:, 2 * LANE:]
        cs_t = jnp.concatenate([cs, jnp.zeros((LANE - c, LANE), F32)], axis=0).T
        g["chains"] = []
        for h in range(heads):
            qh = act[:, h * dk:(h + 1) * dk]
            kh = act[:, hw + h * dk:hw + (h + 1) * dk]
            vh = act[:, 2 * hw + h * dk:2 * hw + (h + 1) * dk]
            qh = qh * (lax.rsqrt(jnp.sum(qh * qh, axis=-1, keepdims=True) + EPS) * (dk ** -0.5))
            kh = kh * lax.rsqrt(jnp.sum(kh * kh, axis=-1, keepdims=True) + EPS)
            beta = gs[:, h:h + 1]
            gc = cs[:, heads + h:heads + h + 1]
            gc_row = cs_t[heads + h:heads + h + 1, :c]
            kb = kh * beta
            egc = jnp.exp(gc)
            g_last = gc[c - 1:c, :]
            g["chains"].append(dict(
                decay=jnp.exp(jnp.where(incl, gc - gc_row, NEG)),
                kq=jnp.concatenate([kb, qh], axis=0).astype(BF16), khb=kh.astype(BF16),
                rhs=jnp.concatenate([vh * beta, kb * egc], axis=-1), qg=qh * egc,
                kd=(kh * jnp.exp(g_last - gc)).astype(BF16), ds=jnp.exp(g_last)))

    def gram(g):
        for ch in g["chains"]:
            ch["a1"] = _dot_nt(ch["kq"], ch["khb"])

    def masks(g):
        for ch in g["chains"]:
            lmat = jnp.where(strict, ch["a1"][:c] * ch["decay"], 0.0)
            ch["attn"] = jnp.where(incl, ch["a1"][c:] * ch["decay"], 0.0).astype(BF16)
            ch["toff"] = -lmat
            ch["lb"] = lmat.astype(BF16)
        for ch in g["chains"]:
            ch["pw"] = _dot(ch["lb"], ch["lb"])

    def double(last):
        def stage(g):
            for ch in g["chains"]:
                pwb = ch["pw"].astype(BF16)
                lhs = ch["toff"].astype(BF16) if last else jnp.concatenate([ch["toff"].astype(BF16), pwb], axis=0)
                ch["tp"] = _dot(lhs, pwb)
            for ch in g["chains"]:
                ch["toff"] = ch["toff"] + ch["pw"] + ch["tp"][:c]
                if not last:
                    ch["pw"] = ch["tp"][c:]
        return stage

    def solve(g):
        for ch in g["chains"]:
            ch["uw"] = _dot(ch["toff"].astype(BF16), ch["rhs"].astype(BF16))
        for ch in g["chains"]:
            uw = ch["rhs"] + ch["uw"]
            ch["u"] = uw[:, :dk]
            ch["wq"] = jnp.concatenate([uw[:, dk:], ch["qg"]], axis=0).astype(BF16)

    def recur(g):
        group, r0 = g["chains"], g["ci"] * c
        t1 = [_dot(ch["wq"], s.astype(BF16)) for ch, s in zip(group, states)]
        vnb = [(ch["u"] - t[:c]).astype(BF16) for ch, t in zip(group, t1)]
        o2 = [_dot(ch["attn"], v) for ch, v in zip(group, vnb)]
        sd = [_dot_tn(ch["kd"], v) for ch, v in zip(group, vnb)]
        states[:] = [s * ch["ds"] + d for ch, s, d in zip(group, states, sd)]
        for h in range(heads):
            o = t1[h][c:] + o2[h]
            zh = z_ref[r0:r0 + c, h * dk:(h + 1) * dk]
            o_ref[r0:r0 + c, h * dk:(h + 1) * dk] = (_rms(o, og) * (zh * _sigmoid(zh))).astype(o_ref.dtype)

    stages = [prep, gram, masks] + [double(s == n_double - 1) for s in range(n_double)] + [solve, recur]
    groups = [dict(ci=ci) for ci in range(nc)]
    for t in range(len(stages) + nc - 1):
        for ci in range(nc):
            if 0 <= t - ci < len(stages):
                stages[t - ci](groups[ci])

    for h in range(heads):
        st_ref[h] = states[h]
    ybuf_ref[0:pad, :] = ybuf_ref[r:r + pad, :]


def gated_delta_net(qkv, z, gates, conv_prev, conv_w, s0, onorm_g, heads, dk, chunks_per_step=16):
    b, l, cd = qkv.shape
    c = min(CHUNK, l)
    nc = min(chunks_per_step, l // c)
    r = c * nc
    hw = heads * dk
    cprev = jnp.pad(conv_prev, ((0, 0), (SUBLANE - (CONV_W - 1), 0), (0, 0)))
    cw = jnp.pad(conv_w, ((0, SUBLANE - CONV_W), (0, 0)))
    vmem = 2 * (r * cd * 4 + r * hw * 4 + r * LANE * 4 + 2 * SUBLANE * cd * 4 + 2 * heads * dk * dk * 4
                + r * hw * 2) + (r + SUBLANE) * cd * 4 * 4
    return pl.pallas_call(
        functools.partial(_gdn_body, c=c, nc=nc, heads=heads, dk=dk),
        grid=(b, l // r),
        in_specs=[pl.BlockSpec((None, r, cd), lambda i, n: (i, n, 0)),
                  pl.BlockSpec((None, r, hw), lambda i, n: (i, n, 0)),
                  pl.BlockSpec((None, r, LANE), lambda i, n: (i, n, 0)),
                  pl.BlockSpec((None, SUBLANE, cd), lambda i, n: (i, 0, 0)),
                  pl.BlockSpec((SUBLANE, cd), lambda i, n: (0, 0)),
                  pl.BlockSpec((None, heads, dk, dk), lambda i, n: (i, 0, 0, 0)),
                  pl.BlockSpec((1, dk), lambda i, n: (0, 0))],
        out_specs=[pl.BlockSpec((None, r, hw), lambda i, n: (i, n, 0)),
                   pl.BlockSpec((None, heads, dk, dk), lambda i, n: (i, 0, 0, 0))],
        out_shape=[jax.ShapeDtypeStruct((b, l, hw), BF16),
                   jax.ShapeDtypeStruct((b, heads, dk, dk), F32)],
        scratch_shapes=[pltpu.VMEM((r + SUBLANE, cd), F32)],
        compiler_params=_params(("parallel", "arbitrary"), vmem), name="gated_delta_net")(
            qkv, z, gates, cprev, cw, s0, onorm_g.reshape(1, dk))


def _swa_body(*refs, n_seg, cq, nchunk, win, hkv, grp, hd, chunked):
    sink_ref, q_ref = refs[0], refs[1]
    k_refs, v_refs, o_ref = refs[2:2 + n_seg], refs[2 + n_seg:2 + 2 * n_seg], refs[2 + 2 * n_seg]
    kk = jnp.concatenate([r[...] for r in k_refs], axis=0).astype(BF16)
    vv = jnp.concatenate([r[...] for r in v_refs], axis=0).astype(BF16)
    row = lax.broadcasted_iota(jnp.int32, (cq, win), 0)
    col = lax.broadcasted_iota(jnp.int32, (cq, win), 1)
    dist = jnp.abs(row - col + WINDOW).astype(F32)
    hq = hkv * grp
    k_heads = [kk[:, j * hd:(j + 1) * hd] for j in range(hkv)]
    v_heads = [vv[:, j * hd:(j + 1) * hd] for j in range(hkv)]
    chains = []
    for c in range(nchunk):
        dist_c = dist
        if chunked and c * cq < WINDOW:
            first_valid = WINDOW - (pl.program_id(1) * nchunk + c) * cq
            dist_c = jnp.where(col >= first_valid, dist, -NEG)
        for j in range(hkv):
            q_stack = jnp.concatenate(
                [q_ref[c * cq:(c + 1) * cq, (j * grp + g) * hd:(j * grp + g + 1) * hd] for g in range(grp)], axis=0)
            chains.append(dict(c=c, j=j, dist=dist_c, q=q_stack,
                               k=k_heads[j][c * cq:c * cq + win, :], v=v_heads[j][c * cq:c * cq + win, :]))

    def scores(group):
        for ch in group:
            ch["s"] = _dot_nt(ch["q"], ch["k"])

    def softmax_pv(group):
        heads_of = [(ch, g, ch["j"] * grp + g) for ch in group for g in range(grp)]
        slopes = [LOG2E * 2.0 ** (-8.0 * (h + 1) / hq) for _, _, h in heads_of]
        sinks = [LOG2E * sink_ref[h] for _, _, h in heads_of]
        sg = [ch["s"][g * cq:(g + 1) * cq, :] - sl * ch["dist"] for (ch, g, _), sl in zip(heads_of, slopes)]
        m = [jnp.maximum(jnp.max(x, axis=-1, keepdims=True), sk) for x, sk in zip(sg, sinks)]
        p = [jnp.exp2(x - mm) for x, mm in zip(sg, m)]
        l = [jnp.sum(x, axis=-1, keepdims=True) + jnp.exp2(sk - mm) for x, mm, sk in zip(p, m, sinks)]
        pb = [x.astype(BF16) for x in p]
        for ci, ch in enumerate(group):
            ch["l"] = l[ci * grp:(ci + 1) * grp]
            ch["pv"] = _dot(jnp.concatenate(pb[ci * grp:(ci + 1) * grp], axis=0), ch["v"])

    def emit(group):
        for ch in group:
            c, j = ch["c"], ch["j"]
            outs = [(ch["pv"][g * cq:(g + 1) * cq, :] / ch["l"][g]).astype(o_ref.dtype) for g in range(grp)]
            o_ref[c * cq:(c + 1) * cq, j * grp * hd:(j + 1) * grp * hd] = jnp.concatenate(outs, axis=-1)

    gsz = 2
    groups = [chains[i:i + gsz] for i in range(0, len(chains), gsz)]
    scores(groups[0])
    for i, group in enumerate(groups):
        if i + 1 < len(groups):
            scores(groups[i + 1])
        softmax_pv(group)
        if i > 0:
            emit(groups[i - 1])
    emit(groups[-1])


def swa_attention(q, segs, sinks, hkv, grp, hd, chunked, chunks_per_step=16):
    b, l, qw = q.shape
    kw = hkv * hd
    if chunked:
        cq, nchunk, n_seg, win = CHUNK, chunks_per_step, 2, WINDOW + CHUNK
        rows = cq * nchunk
        assert l % rows == 0 and rows % WINDOW == 0
        k, v = segs
        k_specs = [pl.BlockSpec((None, WINDOW, kw), lambda i, n: (i, jnp.maximum(n * (rows // WINDOW) - 1, 0), 0)),
                   pl.BlockSpec((None, rows, kw), lambda i, n: (i, n, 0))]
        k_args, v_args = [k] * n_seg, [v] * n_seg
        key_rows = WINDOW + rows
    else:
        cq, nchunk, n_seg = l, 1, len(segs)
        rows = l
        k_specs = [pl.BlockSpec((None, kk.shape[1], kw), lambda i, n: (i, 0, 0)) for kk, _ in segs]
        k_args, v_args = [kk for kk, _ in segs], [vv for _, vv in segs]
        key_rows = win = sum(kk.shape[1] for kk in k_args)
    vmem = 2 * (2 * rows * qw * 2 + 2 * key_rows * kw * 4) + 3 * nchunk * hkv * grp * cq * 256 * 4
    return pl.pallas_call(
        functools.partial(_swa_body, n_seg=n_seg, cq=cq, nchunk=nchunk, win=win, hkv=hkv, grp=grp, hd=hd,
                          chunked=chunked),
        grid=(b, l // rows),
        in_specs=[pl.BlockSpec(memory_space=pltpu.SMEM),
                  pl.BlockSpec((None, rows, qw), lambda i, n: (i, n, 0))] + k_specs + k_specs,
        out_specs=pl.BlockSpec((None, rows, qw), lambda i, n: (i, n, 0)),
        out_shape=jax.ShapeDtypeStruct((b, l, qw), BF16),
        compiler_params=_params(("parallel", "parallel"), vmem), name="swa_attention")(
            sinks, q, *k_args, *v_args)


def _ab_layer(x, conv_prev, s0, past, w_in, conv_w, a_log, dt_bias, onorm_g, f_bias, w_out, g_mix, dims):
    b, l, d = x.shape
    h_a, dk, h_b, hd_b = dims
    a_qk, a_v, b_w = h_a * dk, h_a * dk, h_b * hd_b
    conv_dim = 2 * a_qk + a_v
    o1 = conv_dim + a_v
    o2 = o1 + 2 * h_a
    small = jnp.concatenate([w_in[:, o1:o2], w_in[:, o2 + 3 * b_w:],
                             jnp.zeros((d, LANE - 2 * h_a - h_b), F32)], axis=1)
    w_pack = jnp.concatenate([w_in[:, :o1], w_in[:, o2:o2 + 3 * b_w], small], axis=1).astype(BF16)
    zeros4 = jnp.zeros((h_a,), F32)
    gp = jnp.zeros((SUBLANE, LANE), F32)
    gp = gp.at[0, :3 * h_a].set(jnp.concatenate([zeros4, dt_bias, f_bias]))
    gp = gp.at[1, h_a:2 * h_a].set(a_log)
    f32o, b16, cache = (F32, 1.0, 1), (BF16, 1.0, 1), (F32, 1.0, h_b)
    groups = ((conv_dim, (f32o,)), (a_v, (f32o,)), (b_w, ((BF16, LOG2E * hd_b ** -0.5, 1),)),
              (b_w, (cache, b16)), (b_w, (cache, b16)), (LANE, (f32o,)))
    qkv, z, qb, kb, kb16, vb, vb16, gates = norm_matmul(x.reshape(b * l, d), g_mix, w_pack, groups, gate_params=gp)
    r3 = lambda a: a.reshape(b, l, a.shape[-1])
    qkv3, gates3 = r3(qkv), r3(gates)
    oa, s_new = gated_delta_net(qkv3, r3(z), gates3, conv_prev, conv_w, s0, onorm_g, h_a, dk)
    conv_new = qkv3[:, l - (CONV_W - 1):, :]
    logf = gates3[:, :, 2 * h_a:2 * h_a + h_b]
    lf_rows = jnp.pad(logf.transpose(0, 2, 1).reshape(b * h_b, l), ((0, 0), (0, -l % LANE)))
    f_new = cumsum_rows(lf_rows, LOG2E)[:, :l].reshape(b, h_b, l)
    fox_past = None
    if past is not None:
        pk, pv, plf = past
        p_len = pk.shape[1]
        f_past = cumsum_rows(plf.transpose(0, 2, 1).reshape(b * h_b, p_len), LOG2E, minus_total=True)
        fox_past = (pk.reshape(b, p_len * h_b, hd_b), pv.reshape(b, p_len * h_b, hd_b), f_past.reshape(b, h_b, p_len))
    ob = fox_attention(r3(qb), r3(kb16), r3(vb16), f_new, h_b, past=fox_past)
    w_out16 = w_out.astype(BF16)
    mixed = ([oa, ob], [w_out16[:a_v], w_out16[a_v:]])
    return (mixed, s_new, conv_new, kb.reshape(b, l, h_b, hd_b), vb.reshape(b, l, h_b, hd_b), logf)


def _c_layer(x, past, w_in, sinks, w_out, g_mix, dims):
    b, l, d = x.shape
    hq, hkv, hd = dims
    grp = hq // hkv
    f32o = (F32, 1.0, 1)
    q, k, v = norm_matmul(x.reshape(b * l, d), g_mix, w_in.astype(BF16),
                          ((hq * hd, ((BF16, LOG2E * hd ** -0.5, 1),)), (hkv * hd, (f32o,)), (hkv * hd, (f32o,))))
    q3, k3, v3 = q.reshape(b, l, hq * hd), k.reshape(b, l, hkv * hd), v.reshape(b, l, hkv * hd)
    if past is None:
        o = swa_attention(q3, (k3, v3), sinks, hkv, grp, hd, chunked=True)
        new_k, new_v = k3[:, l - WINDOW:], v3[:, l - WINDOW:]
    else:
        pk, pv = past
        w_len = pk.shape[1]
        o = swa_attention(q3, ((pk.reshape(b, w_len, hkv * hd), pv.reshape(b, w_len, hkv * hd)), (k3, v3)),
                          sinks, hkv, grp, hd, chunked=False)
        new_k, new_v = k3, v3
    return ([o], [w_out.astype(BF16)]), new_k.reshape(b, -1, hkv, hd), new_v.reshape(b, -1, hkv, hd)


def kernel(x_prompt, mem_prompt, x_sample, state_gdn, state_gdn_conv, cache_fox_k, cache_fox_v, cache_fox_logf, cache_swa_k, cache_swa_v, cache_mem_k, cache_mem_v, g_mix, w_in_ab, conv_w_a, a_log_a, dt_bias_a, onorm_g_a, f_bias_b, w_out_ab, w_in_c, sinks_c, w_out_c, g_mem, w_xkv, g_cross, w_xq, w_xo, g_mlp, w_up, w_down, g_final):
    depth = g_mix.shape[0]
    bp, lp, d = x_prompt.shape
    bs, ls, _ = x_sample.shape
    h_a, dk = state_gdn.shape[2], state_gdn.shape[3]
    h_b, hd_b = cache_fox_k.shape[3], cache_fox_k.shape[4]
    hkv, hd_c = cache_swa_k.shape[3], cache_swa_k.shape[4]
    hq = sinks_c.shape[1]
    m_len, h_x, hd_x = cache_mem_k.shape[2], cache_mem_k.shape[3], cache_mem_k.shape[4]
    xw = h_x * hd_x
    conv_dim = state_gdn_conv.shape[-1]

    xp, xs = x_prompt, x_sample
    mem_outs = ((F32, 1.0, h_x),)
    mem_k, mem_v = norm_matmul(mem_prompt.reshape(bp * m_len, d), g_mem, w_xkv.astype(BF16),
                               ((xw, mem_outs), (xw, mem_outs)))
    p_out = {k: [] for k in ("s", "c", "fk", "fv", "fl", "wk", "wv")}
    s_out = {k: [] for k in ("s", "c", "fk", "fv", "fl", "wk", "wv")}
    for layer in range(depth):
        i = layer // 2
        if layer % 2 == 0:
            wts = (w_in_ab[i], conv_w_a[i], a_log_a[i], dt_bias_a[i], onorm_g_a[i], f_bias_b[i], w_out_ab[i],
                   g_mix[layer], (h_a, dk, h_b, hd_b))
            mix_p, st, cv, fk, fv, fl = _ab_layer(xp, jnp.zeros((bp, CONV_W - 1, conv_dim), F32),
                                                  jnp.zeros((bp, h_a, dk, dk), F32), None, *wts)
            for key, val in zip(("s", "c", "fk", "fv", "fl"), (st, cv, fk, fv, fl)):
                p_out[key].append(val)
            mix_s, st, cv, fk, fv, fl = _ab_layer(xs, state_gdn_conv[i], state_gdn[i],
                                                  (cache_fox_k[i], cache_fox_v[i], cache_fox_logf[i]), *wts)
            for key, val in zip(("s", "c", "fk", "fv", "fl"), (st, cv, fk, fv, fl)):
                s_out[key].append(val)
        else:
            wts = (w_in_c[i], sinks_c[i], w_out_c[i], g_mix[layer], (hq, hkv, hd_c))
            mix_p, wk, wv = _c_layer(xp, None, *wts)
            p_out["wk"].append(wk); p_out["wv"].append(wv)
            mix_s, wk, wv = _c_layer(xs, (cache_swa_k[i], cache_swa_v[i]), *wts)
            s_out["wk"].append(wk); s_out["wv"].append(wv)
        mk, mv = mem_k[layer], mem_v[layer]
        post_w = (g_cross[layer], w_xq[layer].astype(BF16))
        mlp_w = (w_xo[layer].astype(BF16), g_mlp[layer], w_up[layer].astype(BF16), w_down[layer].astype(BF16), h_x,
                 g_final if layer == depth - 1 else None)
        xp = post_mixer(xp, *mix_p, *post_w, mk.reshape(bp, m_len * h_x, hd_x), mv.reshape(bp, m_len * h_x, hd_x),
                        *mlp_w)
        xs = post_mixer(xs, *mix_s, *post_w, cache_mem_k[layer].reshape(bs, m_len * h_x, hd_x),
                        cache_mem_v[layer].reshape(bs, m_len * h_x, hd_x), *mlp_w)
    st = jnp.stack
    return (xp, xs,
            st(p_out["s"]), st(p_out["c"]), st(p_out["fk"]), st(p_out["fv"]), st(p_out["fl"]),
            st(p_out["wk"]), st(p_out["wv"]),
            mem_k.reshape(depth, bp, m_len, h_x, hd_x), mem_v.reshape(depth, bp, m_len, h_x, hd_x),
            st(s_out["s"]), st(s_out["c"]), st(s_out["fk"]), st(s_out["fv"]), st(s_out["fl"]),
            st(s_out["wk"]), st(s_out["wv"]))
```

```python
import functools
import math

import jax
import jax.numpy as jnp
from jax import lax
from jax.experimental import pallas as pl
from jax.experimental.pallas import tpu as pltpu

F32 = jnp.float32
BF16 = jnp.bfloat16
LOG2E = math.log2(math.e)
EPS = 1e-6
NEG = -1e30

CHUNK = 64
CONV_W = 4
WINDOW = 128
LANE = 128
SUBLANE = 8
VMEM_BYTES = 64 * 1024 * 1024


def _params(semantics, vmem_bytes):
    limit = int(min(max(vmem_bytes * 1.25, 16 * 1024 * 1024), VMEM_BYTES - 8 * 1024 * 1024))
    return pltpu.CompilerParams(dimension_semantics=semantics, vmem_limit_bytes=limit)


def _dot(a, b):
    return jnp.dot(a, b, preferred_element_type=F32)


def _dot_nt(a, b):
    return lax.dot_general(a, b, (((1,), (1,)), ((), ())), preferred_element_type=F32)


def _dot_tn(a, b):
    return lax.dot_general(a, b, (((0,), (0,)), ((), ())), preferred_element_type=F32)


def _rms(x, g):
    return x * lax.rsqrt(jnp.mean(x * x, axis=-1, keepdims=True) + EPS) * g


def _sigmoid(x):
    return 1.0 / (1.0 + jnp.exp2(x * (-LOG2E)))


def _softplus(x):
    return jnp.maximum(x, 0.0) + jnp.log(1.0 + jnp.exp(-jnp.abs(x)))


def _token_tile(t, cap):
    tm = min(t, cap)
    assert t % tm == 0
    return tm


def _norm_matmul_body(*refs, groups, gated, col_chunk):
    x_ref, g_ref, w_ref = refs[:3]
    refs = refs[3:]
    if gated:
        gp_ref, refs = refs[0], refs[1:]
    x = x_ref[...]
    yb = _rms(x, g_ref[...]).astype(BF16)
    chunks, off, k = [], 0, 0
    for gi, (n, outs) in enumerate(groups):
        chunks += [(gi, n, outs, k, off, c, min(col_chunk, n - c)) for c in range(0, n, col_chunk)]
        off += n
        k += len(outs)

    def project(chunk):
        _, _, _, _, off, c, cw = chunk
        return _dot(yb, w_ref[:, off + c:off + c + cw])

    r_next = project(chunks[0])
    for idx, (gi, n, outs, k, off, c, cw) in enumerate(chunks):
        r = r_next
        if idx + 1 < len(chunks):
            r_next = project(chunks[idx + 1])
        if gated and gi == len(groups) - 1:
            rb = r + gp_ref[0:1, :]
            lane = lax.broadcasted_iota(jnp.int32, rb.shape, 1)
            e = jnp.exp(-jnp.abs(rb))
            l1p = jnp.log(1.0 + e)
            sig = _sigmoid(rb)
            gdec = -jnp.exp(gp_ref[1:2, :]) * (jnp.maximum(rb, 0.0) + l1p)
            logsig = jnp.minimum(rb, 0.0) - l1p
            r = jnp.where(lane < 4, sig, jnp.where(lane < 8, gdec, jnp.where(lane < 12, logsig, 0.0)))
        for j, (dtype, scale, heads) in enumerate(outs):
            o_ref = refs[k + j]
            rr = (r if scale == 1.0 else r * scale).astype(dtype)
            if heads == 1:
                o_ref[:, c:c + cw] = rr
            else:
                hw = n // heads
                tm = x_ref.shape[0]
                for h in range(c // hw, (c + cw) // hw):
                    o_ref[pl.ds(h, tm, stride=heads), :] = rr[:, h * hw - c:(h + 1) * hw - c]


def norm_matmul(x, g, w, groups, gate_params=None, tile_cap=1024):
    t, d = x.shape
    tm = _token_tile(t, tile_cap)
    gated = gate_params is not None
    stacked = w.ndim == 3
    sets = w.shape[0] if stacked else 1
    n_total = w.shape[-1]
    in_specs = [pl.BlockSpec((tm, d), lambda s, i: (i, 0)),
                pl.BlockSpec((None, 1, d), lambda s, i: (s, 0, 0)),
                pl.BlockSpec((None, d, n_total), lambda s, i: (s, 0, 0), pipeline_mode=pl.Buffered(1))]
    args = [x, g.reshape(sets, 1, d), w.reshape(sets, d, n_total)]
    if gated:
        in_specs.append(pl.BlockSpec(gate_params.shape, lambda s, i: (0, 0)))
        args.append(gate_params)
    out_shape, out_specs, out_bytes = [], [], 0
    for n, outs in groups:
        for dtype, _, heads in outs:
            out_shape.append(jax.ShapeDtypeStruct((sets, t * heads, n // heads), dtype))
            out_specs.append(pl.BlockSpec((None, tm * heads, n // heads), lambda s, i: (s, i, 0)))
            out_bytes += tm * n * jnp.dtype(dtype).itemsize
    vmem = 2 * (tm * d * 4 + out_bytes) + d * n_total * 2 + tm * d * 8
    outs = pl.pallas_call(
        functools.partial(_norm_matmul_body, groups=groups, gated=gated, col_chunk=512),
        grid=(sets, t // tm), in_specs=in_specs, out_specs=out_specs, out_shape=out_shape,
        compiler_params=_params(("parallel", "parallel"), vmem), name="norm_matmul")(*args)
    return outs if stacked else [o[0] for o in outs]


def _post_body(*refs, n_in, heads, hd, ff_chunk, final_norm, col_chunk):
    x_ref = refs[0]
    a_refs, w_refs = refs[1:1 + n_in], refs[1 + n_in:1 + 2 * n_in]
    gc_ref, wq_ref, mk_ref, mv_ref, wo_ref, gm_ref, wu_ref, wd_ref = refs[1 + 2 * n_in:9 + 2 * n_in]
    gf_ref = refs[9 + 2 * n_in] if final_norm else None
    o_ref = refs[-1]
    bt, tm, d = x_ref.shape
    rows = bt * tm
    mem_len = mk_ref.shape[1] // heads
    x0 = x_ref[...].reshape(rows, d)
    a_vals = [a[...].reshape(rows, a.shape[-1]).astype(BF16) for a in a_refs]
    x1 = jnp.concatenate(
        [x0[:, c:c + col_chunk] + sum(_dot(a, w[:, c:c + col_chunk]) for a, w in zip(a_vals, w_refs))
         for c in range(0, d, col_chunk)], axis=-1)
    hb = _rms(x1, gc_ref[...]).astype(BF16)
    q = (_dot(hb, wq_ref[...]) * (hd ** -0.5)).astype(BF16)
    att = []
    for bi in range(bt):
        outs = []
        for h in range(heads):
            mem_rows = pl.ds(h, mem_len, stride=heads)
            s = _dot_nt(q[bi * tm:(bi + 1) * tm, h * hd:(h + 1) * hd], mk_ref[bi, mem_rows, :].astype(BF16))
            m = jnp.max(s, axis=-1, keepdims=True)
            p = jnp.exp(s - m)
            l = jnp.sum(p, axis=-1, keepdims=True)
            outs.append((_dot(p.astype(BF16), mv_ref[bi, mem_rows, :].astype(BF16)) / l).astype(BF16))
        att.append(jnp.concatenate(outs, axis=-1))
    x2 = x1 + _dot(jnp.concatenate(att, axis=0), wo_ref[...])
    hb = _rms(x2, gm_ref[...]).astype(BF16)
    acc = x2
    for c in range(0, wu_ref.shape[1], ff_chunk):
        u = jnp.maximum(_dot(hb, wu_ref[:, c:c + ff_chunk]), 0.0)
        acc = acc + _dot((u * u).astype(BF16), wd_ref[c:c + ff_chunk, :])
    if final_norm:
        acc = _rms(acc, gf_ref[...])
    o_ref[...] = acc.reshape(bt, tm, d)


def post_mixer(x, a_list, w_list, g_cross, wq, mk, mv, wo, g_mlp, w_up, w_down, heads, g_final=None, tile_cap=512):
    b, l, d = x.shape
    mh, hd = mk.shape[1], mk.shape[2]
    m, hw = mh // heads, hd * heads
    tm = _token_tile(l, tile_cap)
    bt = math.gcd(b, max(1, tile_cap // tm))
    rows = bt * tm
    n_in = len(a_list)
    final_norm = g_final is not None
    once = pl.Buffered(1)

    def const(shape):
        return pl.BlockSpec(shape, lambda i, j: (0,) * len(shape), pipeline_mode=once)

    in_specs = [pl.BlockSpec((bt, tm, d), lambda i, j: (i, j, 0))]
    in_specs += [pl.BlockSpec((bt, tm, a.shape[2]), lambda i, j: (i, j, 0)) for a in a_list]
    in_specs += [const(w.shape) for w in w_list]
    in_specs += [const((1, d)), const(wq.shape),
                 pl.BlockSpec((bt, mh, hd), lambda i, j: (i, 0, 0)),
                 pl.BlockSpec((bt, mh, hd), lambda i, j: (i, 0, 0)),
                 const(wo.shape), const((1, d)), const(w_up.shape), const(w_down.shape)]
    args = [x, *a_list, *w_list, g_cross.reshape(1, d), wq, mk, mv, wo, g_mlp.reshape(1, d), w_up, w_down]
    if final_norm:
        in_specs.append(const((1, d)))
        args.append(g_final.reshape(1, d))
    weights = sum(w.size for w in w_list) + wq.size + wo.size + w_up.size + w_down.size
    vmem = (weights * 2 + 2 * (2 * rows * d * 4 + 2 * bt * m * hw * 4
                               + sum(rows * a.shape[2] * a.dtype.itemsize for a in a_list))
            + rows * (d * 16 + hw * 8 + m * 8 + 512 * 8))
    return pl.pallas_call(
        functools.partial(_post_body, n_in=n_in, heads=heads, hd=hw // heads, ff_chunk=512,
                          final_norm=final_norm, col_chunk=512),
        grid=(b // bt, l // tm), in_specs=in_specs,
        out_specs=pl.BlockSpec((bt, tm, d), lambda i, j: (i, j, 0)),
        out_shape=jax.ShapeDtypeStruct((b, l, d), F32),
        compiler_params=_params(("parallel", "parallel"), vmem), name="post_mixer")(*args)


def _cumsum_body(x_ref, o_ref, *, minus_total, scale):
    r, l = x_ref.shape
    row = lax.broadcasted_iota(jnp.int32, (LANE, LANE), 0)
    col = lax.broadcasted_iota(jnp.int32, (LANE, LANE), 1)
    upper = (row <= col).astype(BF16)
    carry = jnp.zeros((r, 1), F32)
    for c in range(0, l, LANE):
        x = x_ref[:, c:c + LANE]
        x1 = x.astype(BF16)
        res = x - x1.astype(F32)
        x2 = res.astype(BF16)
        x3 = (res - x2.astype(F32)).astype(BF16)
        y3 = _dot(jnp.concatenate([x1, x2, x3], axis=0), upper)
        y = y3[:r] + y3[r:2 * r] + y3[2 * r:] + carry
        o_ref[:, c:c + LANE] = y
        carry = y[:, LANE - 1:LANE]
    total = carry if minus_total else 0.0
    o_ref[...] = (o_ref[...] - total) * scale


def cumsum_rows(x, scale, minus_total=False):
    r, l = x.shape
    assert l % LANE == 0 and r % SUBLANE == 0
    return pl.pallas_call(
        functools.partial(_cumsum_body, minus_total=minus_total, scale=scale),
        grid=(1,), in_specs=[pl.BlockSpec((r, l), lambda i: (0, 0))],
        out_specs=pl.BlockSpec((r, l), lambda i: (0, 0)),
        out_shape=jax.ShapeDtypeStruct((r, l), F32),
        compiler_params=_params(("arbitrary",), 4 * r * l * 4), name="cumsum_rows")(x)


def _fox_body(*refs, tq, tkp, n_past, nsub, heads):
    if n_past:
        q_ref, k_ref, v_ref, fk_ref, kp_ref, vp_ref, fp_ref, o_ref = refs
    else:
        q_ref, k_ref, v_ref, fk_ref, o_ref = refs
    i = pl.program_id(2)
    sub = tq // nsub
    hd = q_ref.shape[-1]
    qs = [q_ref[c * sub:(c + 1) * sub, :] for c in range(nsub)]

    def update(carry, ss, vbs):
        ms, ls, accs = carry[0::3], carry[1::3], carry[2::3]
        m_new = [jnp.maximum(m, jnp.max(s, axis=-1, keepdims=True)) for m, s in zip(ms, ss)]
        alpha = [jnp.exp2(m - mn) for m, mn in zip(ms, m_new)]
        ps = [jnp.exp2(s - mn) for s, mn in zip(ss, m_new)]
        ls = [a * l + jnp.sum(p, axis=-1, keepdims=True) for a, l, p in zip(alpha, ls, ps)]
        pv = [_dot(p.astype(BF16), vb) for p, vb in zip(ps, vbs)]
        accs = [a * acc + x for a, acc, x in zip(alpha, accs, pv)]
        out = []
        for c in range(len(ss)):
            out += [m_new[c], ls[c], accs[c]]
        return tuple(out)

    carry = (jnp.full((sub, 1), NEG, F32), jnp.zeros((sub, 1), F32), jnp.zeros((sub, hd), F32)) * nsub

    if n_past:
        def past_step(j, carry):
            rows = pl.ds(j * tkp * heads + pl.program_id(1), tkp, stride=heads)
            kb = kp_ref[rows, :].astype(BF16)
            vb = vp_ref[rows, :].astype(BF16)
            fk = fp_ref[pl.ds(j, 1), :]
            return update(carry, [_dot_nt(q, kb) - fk for q in qs], [vb] * nsub)
        carry = lax.fori_loop(0, n_past, past_step, carry)

    kmul = 2 if nsub % 2 == 0 else 1
    tk = kmul * sub

    def self_step(j, carry):
        start = pl.multiple_of(j * tk, tk)
        kb = k_ref[pl.ds(start, tk), :]
        vb = v_ref[pl.ds(start, tk), :]
        fk = jnp.concatenate([fk_ref[pl.ds(j * kmul + t, 1), :] for t in range(kmul)], axis=1)[:, :tk]
        return update(carry, [_dot_nt(q, kb) - fk for q in qs], [vb] * nsub)
    carry = lax.fori_loop(0, i * (nsub // kmul), self_step, carry)

    start = pl.multiple_of(i * tq, tq)
    fks = [fk_ref[pl.ds(i * nsub + c, 1), :] for c in range(nsub)]
    ss, vbs = [], []
    for c in range(nsub):
        w = (c + 1) * sub
        fk = fks[0][:, :sub] if c == 0 else jnp.concatenate(fks[:c + 1], axis=1)
        s = _dot_nt(qs[c], k_ref[pl.ds(start, w), :]) - fk
        row = lax.broadcasted_iota(jnp.int32, (sub, w), 0) + c * sub
        col = lax.broadcasted_iota(jnp.int32, (sub, w), 1)
        ss.append(jnp.where(col <= row, s, NEG))
        vbs.append(v_ref[pl.ds(start, w), :])
    carry = update(carry, ss, vbs)
    for c in range(nsub):
        o_ref[c * sub:(c + 1) * sub, :] = (carry[3 * c + 2] / carry[3 * c + 1]).astype(o_ref.dtype)


def fox_attention(q, k, v, f_new, heads, past=None, tq_cap=1024, sub_cap=256, score_tile=64 * 1024):
    b, l, hw = q.shape
    hd = hw // heads
    tq = min(l, tq_cap)
    nq = l // tq
    nsub = max(1, tq // sub_cap)
    p_len = 0 if past is None else past[0].shape[1] // heads
    tkp = min(p_len, max(sub_cap, score_tile // tq))
    sub = tq // nsub
    nk = l // sub
    wk = -(-sub // LANE) * LANE
    if wk != sub:
        assert nk == 1
        f_new = jnp.pad(f_new, ((0, 0), (0, 0), (0, wk - sub)))
    in_specs = [pl.BlockSpec((None, tq, hd), lambda bi, h, i: (bi, i, h)),
                pl.BlockSpec((None, l, hd), lambda bi, h, i: (bi, 0, h)),
                pl.BlockSpec((None, l, hd), lambda bi, h, i: (bi, 0, h)),
                pl.BlockSpec((None, None, nk, wk), lambda bi, h, i: (bi, h, 0, 0))]
    args = [q, k, v, f_new.reshape(b, heads, nk, wk)]
    n_past = 0
    vmem = 2 * (2 * tq * hd * 2 + 2 * l * hd * 2) + 8 * tq * max(tq, tkp) * 4
    if past is not None:
        kp, vp, fp = past
        n_past = p_len // tkp
        in_specs += [pl.BlockSpec((None, p_len * heads, hd), lambda bi, h, i: (bi, 0, 0)),
                     pl.BlockSpec((None, p_len * heads, hd), lambda bi, h, i: (bi, 0, 0)),
                     pl.BlockSpec((None, None, n_past, tkp), lambda bi, h, i: (bi, h, 0, 0))]
        args += [kp, vp, fp.reshape(b, heads, n_past, tkp)]
        vmem += 2 * 2 * p_len * heads * hd * 4
    return pl.pallas_call(
        functools.partial(_fox_body, tq=tq, tkp=tkp, n_past=n_past, nsub=nsub, heads=heads),
        grid=(b, heads, nq), in_specs=in_specs,
        out_specs=pl.BlockSpec((None, tq, hd), lambda bi, h, i: (bi, i, h)),
        out_shape=jax.ShapeDtypeStruct((b, l, hw), BF16),
        compiler_params=_params(("parallel", "parallel", "arbitrary"), vmem), name="fox_attention")(*args)


def _gdn_body(qkv_ref, z_ref, gates_ref, cprev_ref, cw_ref, s0_ref, og_ref, o_ref, st_ref, ybuf_ref,
              *, c, nc, heads, dk):
    pad = SUBLANE
    r = c * nc
    hw = heads * dk

    @pl.when(pl.program_id(1) == 0)
    def _():
        ybuf_ref[0:pad, :] = cprev_ref[...]
        st_ref[...] = s0_ref[...]

    ybuf_ref[pad:pad + r, :] = qkv_ref[...]

    row = lax.broadcasted_iota(jnp.int32, (c, c), 0)
    col = lax.broadcasted_iota(jnp.int32, (c, c), 1)
    incl = row >= col
    strict = row > col
    tril = incl.astype(BF16)
    n_double = int(math.log2(c)) - 1
    og = og_ref[...]
    states = [st_ref[h] for h in range(heads)]

    def prep(g):
        r0 = g["ci"] * c
        conv = qkv_ref[r0:r0 + c, :] * cw_ref[CONV_W - 1:CONV_W, :]
        for k in range(1, CONV_W):
            conv = conv + ybuf_ref[pad + r0 - k:pad + r0 - k + c, :] * cw_ref[CONV_W - 1 - k:CONV_W - k, :]
        act = conv * _sigmoid(conv)
        gs = gates_ref[r0:r0 + c, :]
        g1 = gs.astype(BF16)
        res = gs - g1.astype(F32)
        g2 = res.astype(BF16)
        g3 = (res - g2.astype(F32)).astype(BF16)
        cs3 = _dot(tril, jnp.concatenate([g1, g2, g3], axis=-1))
        cs = cs3[:, :LANE] + cs3[:, LANE:2 * LANE] + cs3[:, 2 * LANE:]
        cs_t = jnp.concatenate([cs, jnp.zeros((LANE - c, LANE), F32)], axis=0).T
        g["chains"] = []
        for h in range(heads):
            qh = act[:, h * dk:(h + 1) * dk]
            kh = act[:, hw + h * dk:hw + (h + 1) * dk]
            vh = act[:, 2 * hw + h * dk:2 * hw + (h + 1) * dk]
            qh = qh * (lax.rsqrt(jnp.sum(qh * qh, axis=-1, keepdims=True) + EPS) * (dk ** -0.5))
            kh = kh * lax.rsqrt(jnp.sum(kh * kh, axis=-1, keepdims=True) + EPS)
            beta = gs[:, h:h + 1]
            gc = cs[:, heads + h:heads + h + 1]
            gc_row = cs_t[heads + h:heads + h + 1, :c]
            kb = kh * beta
            egc = jnp.exp(gc)
            g_last = gc[c - 1:c, :]
            g["chains"].append(dict(
                decay=jnp.exp(jnp.where(incl, gc - gc_row, NEG)),
                kq=jnp.concatenate([kb, qh], axis=0).astype(BF16), khb=kh.astype(BF16),
                rhs=jnp.concatenate([vh * beta, kb * egc], axis=-1), qg=qh * egc,
                kd=(kh * jnp.exp(g_last - gc)).astype(BF16), ds=jnp.exp(g_last)))

    def gram(g):
        for ch in g["chains"]:
            ch["a1"] = _dot_nt(ch["kq"], ch["khb"])

    def masks(g):
        for ch in g["chains"]:
            lmat = jnp.where(strict, ch["a1"][:c] * ch["decay"], 0.0)
            ch["attn"] = jnp.where(incl, ch["a1"][c:] * ch["decay"], 0.0).astype(BF16)
            ch["toff"] = -lmat
            ch["lb"] = lmat.astype(BF16)
        for ch in g["chains"]:
            ch["pw"] = _dot(ch["lb"], ch["lb"])

    def double(last):
        def stage(g):
            for ch in g["chains"]:
                pwb = ch["pw"].astype(BF16)
                lhs = ch["toff"].astype(BF16) if last else jnp.concatenate([ch["toff"].astype(BF16), pwb], axis=0)
                ch["tp"] = _dot(lhs, pwb)
            for ch in g["chains"]:
                ch["toff"] = ch["toff"] + ch["pw"] + ch["tp"][:c]
                if not last:
                    ch["pw"] = ch["tp"][c:]
        return stage

    def solve(g):
        for ch in g["chains"]:
            ch["uw"] = _dot(ch["toff"].astype(BF16), ch["rhs"].astype(BF16))
        for ch in g["chains"]:
            uw = ch["rhs"] + ch["uw"]
            ch["u"] = uw[:, :dk]
            ch["wq"] = jnp.concatenate([uw[:, dk:], ch["qg"]], axis=0).astype(BF16)

    def recur(g):
        group, r0 = g["chains"], g["ci"] * c
        t1 = [_dot(ch["wq"], s.astype(BF16)) for ch, s in zip(group, states)]
        vnb = [(ch["u"] - t[:c]).astype(BF16) for ch, t in zip(group, t1)]
        o2 = [_dot(ch["attn"], v) for ch, v in zip(group, vnb)]
        sd = [_dot_tn(ch["kd"], v) for ch, v in zip(group, vnb)]
        states[:] = [s * ch["ds"] + d for ch, s, d in zip(group, states, sd)]
        for h in range(heads):
            o = t1[h][c:] + o2[h]
            zh = z_ref[r0:r0 + c, h * dk:(h + 1) * dk]
            o_ref[r0:r0 + c, h * dk:(h + 1) * dk] = (_rms(o, og) * (zh * _sigmoid(zh))).astype(o_ref.dtype)

    stages = [prep, gram, masks] + [double(s == n_double - 1) for s in range(n_double)] + [solve, recur]
    groups = [dict(ci=ci) for ci in range(nc)]
    for t in range(len(stages) + nc - 1):
        for ci in range(nc):
            if 0 <= t - ci < len(stages):
                stages[t - ci](groups[ci])

    for h in range(heads):
        st_ref[h] = states[h]
    ybuf_ref[0:pad, :] = ybuf_ref[r:r + pad, :]


def gated_delta_net(qkv, z, gates, conv_prev, conv_w, s0, onorm_g, heads, dk, chunks_per_step=16):
    b, l, cd = qkv.shape
    c = min(CHUNK, l)
    nc = min(chunks_per_step, l // c)
    r = c * nc
    hw = heads * dk
    cprev = jnp.pad(conv_prev, ((0, 0), (SUBLANE - (CONV_W - 1), 0), (0, 0)))
    cw = jnp.pad(conv_w, ((0, SUBLANE - CONV_W), (0, 0)))
    vmem = 2 * (r * cd * 4 + r * hw * 4 + r * LANE * 4 + 2 * SUBLANE * cd * 4 + 2 * heads * dk * dk * 4
                + r * hw * 2) + (r + SUBLANE) * cd * 4 * 4
    return pl.pallas_call(
        functools.partial(_gdn_body, c=c, nc=nc, heads=heads, dk=dk),
        grid=(b, l // r),
        in_specs=[pl.BlockSpec((None, r, cd), lambda i, n: (i, n, 0)),
                  pl.BlockSpec((None, r, hw), lambda i, n: (i, n, 0)),
                  pl.BlockSpec((None, r, LANE), lambda i, n: (i, n, 0)),
                  pl.BlockSpec((None, SUBLANE, cd), lambda i, n: (i, 0, 0)),
                  pl.BlockSpec((SUBLANE, cd), lambda i, n: (0, 0)),
                  pl.BlockSpec((None, heads, dk, dk), lambda i, n: (i, 0, 0, 0)),
                  pl.BlockSpec((1, dk), lambda i, n: (0, 0))],
        out_specs=[pl.BlockSpec((None, r, hw), lambda i, n: (i, n, 0)),
                   pl.BlockSpec((None, heads, dk, dk), lambda i, n: (i, 0, 0, 0))],
        out_shape=[jax.ShapeDtypeStruct((b, l, hw), BF16),
                   jax.ShapeDtypeStruct((b, heads, dk, dk), F32)],
        scratch_shapes=[pltpu.VMEM((r + SUBLANE, cd), F32)],
        compiler_params=_params(("parallel", "arbitrary"), vmem), name="gated_delta_net")(
            qkv, z, gates, cprev, cw, s0, onorm_g.reshape(1, dk))


def _swa_body(*refs, n_seg, cq, nchunk, win, hkv, grp, hd, chunked):
    sink_ref, q_ref = refs[0], refs[1]
    k_refs, v_refs, o_ref = refs[2:2 + n_seg], refs[2 + n_seg:2 + 2 * n_seg], refs[2 + 2 * n_seg]
    kk = jnp.concatenate([r[...] for r in k_refs], axis=0).astype(BF16)
    vv = jnp.concatenate([r[...] for r in v_refs], axis=0).astype(BF16)
    row = lax.broadcasted_iota(jnp.int32, (cq, win), 0)
    col = lax.broadcasted_iota(jnp.int32, (cq, win), 1)
    dist = jnp.abs(row - col + WINDOW).astype(F32)
    hq = hkv * grp
    k_heads = [kk[:, j * hd:(j + 1) * hd] for j in range(hkv)]
    v_heads = [vv[:, j * hd:(j + 1) * hd] for j in range(hkv)]
    chains = []
    for c in range(nchunk):
        dist_c = dist
        if chunked and c * cq < WINDOW:
            first_valid = WINDOW - (pl.program_id(1) * nchunk + c) * cq
            dist_c = jnp.where(col >= first_valid, dist, -NEG)
        for j in range(hkv):
            q_stack = jnp.concatenate(
                [q_ref[c * cq:(c + 1) * cq, (j * grp + g) * hd:(j * grp + g + 1) * hd] for g in range(grp)], axis=0)
            chains.append(dict(c=c, j=j, dist=dist_c, q=q_stack,
                               k=k_heads[j][c * cq:c * cq + win, :], v=v_heads[j][c * cq:c * cq + win, :]))

    def scores(group):
        for ch in group:
            ch["s"] = _dot_nt(ch["q"], ch["k"])

    def softmax_pv(group):
        heads_of = [(ch, g, ch["j"] * grp + g) for ch in group for g in range(grp)]
        slopes = [LOG2E * 2.0 ** (-8.0 * (h + 1) / hq) for _, _, h in heads_of]
        sinks = [LOG2E * sink_ref[h] for _, _, h in heads_of]
        sg = [ch["s"][g * cq:(g + 1) * cq, :] - sl * ch["dist"] for (ch, g, _), sl in zip(heads_of, slopes)]
        m = [jnp.maximum(jnp.max(x, axis=-1, keepdims=True), sk) for x, sk in zip(sg, sinks)]
        p = [jnp.exp2(x - mm) for x, mm in zip(sg, m)]
        l = [jnp.sum(x, axis=-1, keepdims=True) + jnp.exp2(sk - mm) for x, mm, sk in zip(p, m, sinks)]
        pb = [x.astype(BF16) for x in p]
        for ci, ch in enumerate(group):
            ch["l"] = l[ci * grp:(ci + 1) * grp]
            ch["pv"] = _dot(jnp.concatenate(pb[ci * grp:(ci + 1) * grp], axis=0), ch["v"])

    def emit(group):
        for ch in group:
            c, j = ch["c"], ch["j"]
            outs = [(ch["pv"][g * cq:(g + 1) * cq, :] / ch["l"][g]).astype(o_ref.dtype) for g in range(grp)]
            o_ref[c * cq:(c + 1) * cq, j * grp * hd:(j + 1) * grp * hd] = jnp.concatenate(outs, axis=-1)

    gsz = 2
    groups = [chains[i:i + gsz] for i in range(0, len(chains), gsz)]
    scores(groups[0])
    for i, group in enumerate(groups):
        if i + 1 < len(groups):
            scores(groups[i + 1])
        softmax_pv(group)
        if i > 0:
            emit(groups[i - 1])
    emit(groups[-1])


def swa_attention(q, segs, sinks, hkv, grp, hd, chunked, chunks_per_step=16):
    b, l, qw = q.shape
    kw = hkv * hd
    if chunked:
        cq, nchunk, n_seg, win = CHUNK, chunks_per_step, 2, WINDOW + CHUNK
        rows = cq * nchunk
        assert l % rows == 0 and rows % WINDOW == 0
        k, v = segs
        k_specs = [pl.BlockSpec((None, WINDOW, kw), lambda i, n: (i, jnp.maximum(n * (rows // WINDOW) - 1, 0), 0)),
                   pl.BlockSpec((None, rows, kw), lambda i, n: (i, n, 0))]
        k_args, v_args = [k] * n_seg, [v] * n_seg
        key_rows = WINDOW + rows
    else:
        cq, nchunk, n_seg = l, 1, len(segs)
        rows = l
        k_specs = [pl.BlockSpec((None, kk.shape[1], kw), lambda i, n: (i, 0, 0)) for kk, _ in segs]
        k_args, v_args = [kk for kk, _ in segs], [vv for _, vv in segs]
        key_rows = win = sum(kk.shape[1] for kk in k_args)
    vmem = 2 * (2 * rows * qw * 2 + 2 * key_rows * kw * 4) + 3 * nchunk * hkv * grp * cq * 256 * 4
    return pl.pallas_call(
        functools.partial(_swa_body, n_seg=n_seg, cq=cq, nchunk=nchunk, win=win, hkv=hkv, grp=grp, hd=hd,
                          chunked=chunked),
        grid=(b, l // rows),
        in_specs=[pl.BlockSpec(memory_space=pltpu.SMEM),
                  pl.BlockSpec((None, rows, qw), lambda i, n: (i, n, 0))] + k_specs + k_specs,
        out_specs=pl.BlockSpec((None, rows, qw), lambda i, n: (i, n, 0)),
        out_shape=jax.ShapeDtypeStruct((b, l, qw), BF16),
        compiler_params=_params(("parallel", "parallel"), vmem), name="swa_attention")(
            sinks, q, *k_args, *v_args)


def _ab_layer(x, conv_prev, s0, past, w_in, conv_w, a_log, dt_bias, onorm_g, f_bias, w_out, g_mix, dims):
    b, l, d = x.shape
    h_a, dk, h_b, hd_b = dims
    a_qk, a_v, b_w = h_a * dk, h_a * dk, h_b * hd_b
    conv_dim = 2 * a_qk + a_v
    o1 = conv_dim + a_v
    o2 = o1 + 2 * h_a
    small = jnp.concatenate([w_in[:, o1:o2], w_in[:, o2 + 3 * b_w:],
                             jnp.zeros((d, LANE - 2 * h_a - h_b), F32)], axis=1)
    w_pack = lax.optimization_barrier(jnp.concatenate([w_in[:, :o1], w_in[:, o2:o2 + 3 * b_w], small], axis=1))
    w_pack = w_pack.astype(BF16)
    zeros4 = jnp.zeros((h_a,), F32)
    gp = jnp.zeros((SUBLANE, LANE), F32)
    gp = gp.at[0, :3 * h_a].set(jnp.concatenate([zeros4, dt_bias, f_bias]))
    gp = gp.at[1, h_a:2 * h_a].set(a_log)
    f32o, b16, cache = (F32, 1.0, 1), (BF16, 1.0, 1), (F32, 1.0, h_b)
    groups = ((conv_dim, (f32o,)), (a_v, (f32o,)), (b_w, ((BF16, LOG2E * hd_b ** -0.5, 1),)),
              (b_w, (cache, b16)), (b_w, (cache, b16)), (LANE, (f32o,)))
    qkv, z, qb, kb, kb16, vb, vb16, gates = norm_matmul(x.reshape(b * l, d), g_mix, w_pack, groups, gate_params=gp)
    r3 = lambda a: a.reshape(b, l, a.shape[-1])
    qkv3, gates3 = r3(qkv), r3(gates)
    oa, s_new = gated_delta_net(qkv3, r3(z), gates3, conv_prev, conv_w, s0, onorm_g, h_a, dk)
    conv_new = qkv3[:, l - (CONV_W - 1):, :]
    logf = gates3[:, :, 2 * h_a:2 * h_a + h_b]
    lf_rows = jnp.pad(logf.transpose(0, 2, 1).reshape(b * h_b, l), ((0, 0), (0, -l % LANE)))
    f_new = cumsum_rows(lf_rows, LOG2E)[:, :l].reshape(b, h_b, l)
    fox_past = None
    if past is not None:
        pk, pv, plf = past
        p_len = pk.shape[1]
        f_past = cumsum_rows(plf.transpose(0, 2, 1).reshape(b * h_b, p_len), LOG2E, minus_total=True)
        fox_past = (pk.reshape(b, p_len * h_b, hd_b), pv.reshape(b, p_len * h_b, hd_b), f_past.reshape(b, h_b, p_len))
    ob = fox_attention(r3(qb), r3(kb16), r3(vb16), f_new, h_b, past=fox_past)
    w_out16 = w_out.astype(BF16)
    mixed = ([oa, ob], [w_out16[:a_v], w_out16[a_v:]])
    return (mixed, s_new, conv_new, kb.reshape(b, l, h_b, hd_b), vb.reshape(b, l, h_b, hd_b), logf)


def _c_layer(x, past, w_in, sinks, w_out, g_mix, dims):
    b, l, d = x.shape
    hq, hkv, hd = dims
    grp = hq // hkv
    f32o = (F32, 1.0, 1)
    q, k, v = norm_matmul(x.reshape(b * l, d), g_mix, w_in.astype(BF16),
                          ((hq * hd, ((BF16, LOG2E * hd ** -0.5, 1),)), (hkv * hd, (f32o,)), (hkv * hd, (f32o,))))
    q3, k3, v3 = q.reshape(b, l, hq * hd), k.reshape(b, l, hkv * hd), v.reshape(b, l, hkv * hd)
    if past is None:
        o = swa_attention(q3, (k3, v3), sinks, hkv, grp, hd, chunked=True)
        new_k, new_v = k3[:, l - WINDOW:], v3[:, l - WINDOW:]
    else:
        pk, pv = past
        w_len = pk.shape[1]
        o = swa_attention(q3, ((pk.reshape(b, w_len, hkv * hd), pv.reshape(b, w_len, hkv * hd)), (k3, v3)),
                          sinks, hkv, grp, hd, chunked=False)
        new_k, new_v = k3, v3
    return ([o], [w_out.astype(BF16)]), new_k.reshape(b, -1, hkv, hd), new_v.reshape(b, -1, hkv, hd)


def kernel(x_prompt, mem_prompt, x_sample, state_gdn, state_gdn_conv, cache_fox_k, cache_fox_v, cache_fox_logf, cache_swa_k, cache_swa_v, cache_mem_k, cache_mem_v, g_mix, w_in_ab, conv_w_a, a_log_a, dt_bias_a, onorm_g_a, f_bias_b, w_out_ab, w_in_c, sinks_c, w_out_c, g_mem, w_xkv, g_cross, w_xq, w_xo, g_mlp, w_up, w_down, g_final):
    depth = g_mix.shape[0]
    bp, lp, d = x_prompt.shape
    bs, ls, _ = x_sample.shape
    h_a, dk = state_gdn.shape[2], state_gdn.shape[3]
    h_b, hd_b = cache_fox_k.shape[3], cache_fox_k.shape[4]
    hkv, hd_c = cache_swa_k.shape[3], cache_swa_k.shape[4]
    hq = sinks_c.shape[1]
    m_len, h_x, hd_x = cache_mem_k.shape[2], cache_mem_k.shape[3], cache_mem_k.shape[4]
    xw = h_x * hd_x
    conv_dim = state_gdn_conv.shape[-1]

    xp, xs = x_prompt, x_sample
    mem_outs = ((F32, 1.0, h_x),)
    mem_k, mem_v = norm_matmul(mem_prompt.reshape(bp * m_len, d), g_mem, w_xkv.astype(BF16),
                               ((xw, mem_outs), (xw, mem_outs)))
    p_out = {k: [] for k in ("s", "c", "fk", "fv", "fl", "wk", "wv")}
    s_out = {k: [] for k in ("s", "c", "fk", "fv", "fl", "wk", "wv")}
    for layer in range(depth):
        i = layer // 2
        if layer % 2 == 0:
            wts = (w_in_ab[i], conv_w_a[i], a_log_a[i], dt_bias_a[i], onorm_g_a[i], f_bias_b[i], w_out_ab[i],
                   g_mix[layer], (h_a, dk, h_b, hd_b))
            mix_p, st, cv, fk, fv, fl = _ab_layer(xp, jnp.zeros((bp, CONV_W - 1, conv_dim), F32),
                                                  jnp.zeros((bp, h_a, dk, dk), F32), None, *wts)
            for key, val in zip(("s", "c", "fk", "fv", "fl"), (st, cv, fk, fv, fl)):
                p_out[key].append(val)
            mix_s, st, cv, fk, fv, fl = _ab_layer(xs, state_gdn_conv[i], state_gdn[i],
                                                  (cache_fox_k[i], cache_fox_v[i], cache_fox_logf[i]), *wts)
            for key, val in zip(("s", "c", "fk", "fv", "fl"), (st, cv, fk, fv, fl)):
                s_out[key].append(val)
        else:
            wts = (w_in_c[i], sinks_c[i], w_out_c[i], g_mix[layer], (hq, hkv, hd_c))
            mix_p, wk, wv = _c_layer(xp, None, *wts)
            p_out["wk"].append(wk); p_out["wv"].append(wv)
            mix_s, wk, wv = _c_layer(xs, (cache_swa_k[i], cache_swa_v[i]), *wts)
            s_out["wk"].append(wk); s_out["wv"].append(wv)
        mk, mv = mem_k[layer], mem_v[layer]
        post_w = (g_cross[layer], w_xq[layer].astype(BF16))
        mlp_w = (w_xo[layer].astype(BF16), g_mlp[layer], w_up[layer].astype(BF16), w_down[layer].astype(BF16), h_x,
                 g_final if layer == depth - 1 else None)
        xp = post_mixer(xp, *mix_p, *post_w, mk.reshape(bp, m_len * h_x, hd_x), mv.reshape(bp, m_len * h_x, hd_x),
                        *mlp_w)
        xs = post_mixer(xs, *mix_s, *post_w, cache_mem_k[layer].reshape(bs, m_len * h_x, hd_x),
                        cache_mem_v[layer].reshape(bs, m_len * h_x, hd_x), *mlp_w)
    st = jnp.stack
    return (xp, xs,
            st(p_out["s"]), st(p_out["c"]), st(p_out["fk"]), st(p_out["fv"]), st(p_out["fl"]),
            st(p_out["wk"]), st(p_out["wv"]),
            mem_k.reshape(depth, bp, m_len, h_x, hd_x), mem_v.reshape(depth, bp, m_len, h_x, hd_x),
            st(s_out["s"]), st(s_out["c"]), st(s_out["fk"]), st(s_out["fv"]), st(s_out["fl"]),
            st(s_out["wk"]), st(s_out["wv"]))
```

```python
import functools
import math

import jax
import jax.numpy as jnp
from jax import lax
from jax.experimental import pallas as pl
from jax.experimental.pallas import tpu as pltpu

F32 = jnp.float32
BF16 = jnp.bfloat16
LOG2E = math.log2(math.e)
EPS = 1e-6
NEG = -1e30

CHUNK = 64
CONV_W = 4
WINDOW = 128
LANE = 128
SUBLANE = 8
VMEM_BYTES = 64 * 1024 * 1024


def _params(semantics, vmem_bytes):
    limit = int(min(max(vmem_bytes * 1.25, 16 * 1024 * 1024), VMEM_BYTES - 8 * 1024 * 1024))
    return pltpu.CompilerParams(dimension_semantics=semantics, vmem_limit_bytes=limit)


def _dot(a, b):
    return jnp.dot(a, b, preferred_element_type=F32)


def _dot_nt(a, b):
    return lax.dot_general(a, b, (((1,), (1,)), ((), ())), preferred_element_type=F32)


def _dot_tn(a, b):
    return lax.dot_general(a, b, (((0,), (0,)), ((), ())), preferred_element_type=F32)


def _rms(x, g):
    return x * lax.rsqrt(jnp.mean(x * x, axis=-1, keepdims=True) + EPS) * g


def _sigmoid(x):
    return 1.0 / (1.0 + jnp.exp2(x * (-LOG2E)))


def _softplus(x):
    return jnp.maximum(x, 0.0) + jnp.log(1.0 + jnp.exp(-jnp.abs(x)))


def _token_tile(t, cap):
    tm = min(t, cap)
    assert t % tm == 0
    return tm


def _norm_matmul_body(*refs, groups, gated, col_chunk):
    x_ref, g_ref, w_ref = refs[:3]
    refs = refs[3:]
    if gated:
        gp_ref, refs = refs[0], refs[1:]
    x = x_ref[...]
    yb = _rms(x, g_ref[...]).astype(BF16)
    chunks, off, k = [], 0, 0
    for gi, (n, outs) in enumerate(groups):
        chunks += [(gi, n, outs, k, off, c, min(col_chunk, n - c)) for c in range(0, n, col_chunk)]
        off += n
        k += len(outs)

    def project(chunk):
        _, _, _, _, off, c, cw = chunk
        return _dot(yb, w_ref[:, off + c:off + c + cw])

    r_next = project(chunks[0])
    for idx, (gi, n, outs, k, off, c, cw) in enumerate(chunks):
        r = r_next
        if idx + 1 < len(chunks):
            r_next = project(chunks[idx + 1])
        if gated and gi == len(groups) - 1:
            rb = r + gp_ref[0:1, :]
            lane = lax.broadcasted_iota(jnp.int32, rb.shape, 1)
            e = jnp.exp(-jnp.abs(rb))
            l1p = jnp.log(1.0 + e)
            sig = _sigmoid(rb)
            gdec = -jnp.exp(gp_ref[1:2, :]) * (jnp.maximum(rb, 0.0) + l1p)
            logsig = jnp.minimum(rb, 0.0) - l1p
            r = jnp.where(lane < 4, sig, jnp.where(lane < 8, gdec, jnp.where(lane < 12, logsig, 0.0)))
        for j, (dtype, scale, heads) in enumerate(outs):
            o_ref = refs[k + j]
            rr = (r if scale == 1.0 else r * scale).astype(dtype)
            if heads == 1:
                o_ref[:, c:c + cw] = rr
            else:
                hw = n // heads
                tm = x_ref.shape[0]
                for h in range(c // hw, (c + cw) // hw):
                    o_ref[pl.ds(h, tm, stride=heads), :] = rr[:, h * hw - c:(h + 1) * hw - c]


def norm_matmul(x, g, w, groups, gate_params=None, tile_cap=1024):
    t, d = x.shape
    tm = _token_tile(t, tile_cap)
    gated = gate_params is not None
    stacked = w.ndim == 3
    sets = w.shape[0] if stacked else 1
    n_total = w.shape[-1]
    in_specs = [pl.BlockSpec((tm, d), lambda s, i: (i, 0)),
                pl.BlockSpec((None, 1, d), lambda s, i: (s, 0, 0)),
                pl.BlockSpec((None, d, n_total), lambda s, i: (s, 0, 0), pipeline_mode=pl.Buffered(1))]
    args = [x, g.reshape(sets, 1, d), w.reshape(sets, d, n_total)]
    if gated:
        in_specs.append(pl.BlockSpec(gate_params.shape, lambda s, i: (0, 0)))
        args.append(gate_params)
    out_shape, out_specs, out_bytes = [], [], 0
    for n, outs in groups:
        for dtype, _, heads in outs:
            out_shape.append(jax.ShapeDtypeStruct((sets, t * heads, n // heads), dtype))
            out_specs.append(pl.BlockSpec((None, tm * heads, n // heads), lambda s, i: (s, i, 0)))
            out_bytes += tm * n * jnp.dtype(dtype).itemsize
    vmem = 2 * (tm * d * 4 + out_bytes) + d * n_total * 2 + tm * d * 8
    outs = pl.pallas_call(
        functools.partial(_norm_matmul_body, groups=groups, gated=gated, col_chunk=512),
        grid=(sets, t // tm), in_specs=in_specs, out_specs=out_specs, out_shape=out_shape,
        compiler_params=_params(("parallel", "parallel"), vmem), name="norm_matmul")(*args)
    return outs if stacked else [o[0] for o in outs]


def _post_body(*refs, n_in, heads, hd, ff_chunk, final_norm, col_chunk):
    x_ref = refs[0]
    a_refs, w_refs = refs[1:1 + n_in], refs[1 + n_in:1 + 2 * n_in]
    gc_ref, wq_ref, mk_ref, mv_ref, wo_ref, gm_ref, wu_ref, wd_ref = refs[1 + 2 * n_in:9 + 2 * n_in]
    gf_ref = refs[9 + 2 * n_in] if final_norm else None
    o_ref = refs[-1]
    bt, tm, d = x_ref.shape
    rows = bt * tm
    mem_len = mk_ref.shape[1] // heads
    x0 = x_ref[...].reshape(rows, d)
    a_vals = [a[...].reshape(rows, a.shape[-1]).astype(BF16) for a in a_refs]
    x1 = jnp.concatenate(
        [x0[:, c:c + col_chunk] + sum(_dot(a, w[:, c:c + col_chunk]) for a, w in zip(a_vals, w_refs))
         for c in range(0, d, col_chunk)], axis=-1)
    hb = _rms(x1, gc_ref[...]).astype(BF16)
    q = (_dot(hb, wq_ref[...]) * (hd ** -0.5)).astype(BF16)
    att = []
    for bi in range(bt):
        outs = []
        for h in range(heads):
            mem_rows = pl.ds(h, mem_len, stride=heads)
            s = _dot_nt(q[bi * tm:(bi + 1) * tm, h * hd:(h + 1) * hd], mk_ref[bi, mem_rows, :].astype(BF16))
            m = jnp.max(s, axis=-1, keepdims=True)
            p = jnp.exp(s - m)
            l = jnp.sum(p, axis=-1, keepdims=True)
            outs.append((_dot(p.astype(BF16), mv_ref[bi, mem_rows, :].astype(BF16)) / l).astype(BF16))
        att.append(jnp.concatenate(outs, axis=-1))
    x2 = x1 + _dot(jnp.concatenate(att, axis=0), wo_ref[...])
    hb = _rms(x2, gm_ref[...]).astype(BF16)
    acc = x2
    for c in range(0, wu_ref.shape[1], ff_chunk):
        u = jnp.maximum(_dot(hb, wu_ref[:, c:c + ff_chunk]), 0.0)
        acc = acc + _dot((u * u).astype(BF16), wd_ref[c:c + ff_chunk, :])
    if final_norm:
        acc = _rms(acc, gf_ref[...])
    o_ref[...] = acc.reshape(bt, tm, d)


def post_mixer(x, a_list, w_list, g_cross, wq, mk, mv, layer, wo, g_mlp, w_up, w_down, heads, g_final=None,
               tile_cap=512):
    b, l, d = x.shape
    mh, hd = mk.shape[2], mk.shape[3]
    m, hw = mh // heads, hd * heads
    tm = _token_tile(l, tile_cap)
    bt = math.gcd(b, max(1, tile_cap // tm))
    rows = bt * tm
    n_in = len(a_list)
    final_norm = g_final is not None
    once = pl.Buffered(1)

    def const(shape):
        return pl.BlockSpec(shape, lambda i, j: (0,) * len(shape), pipeline_mode=once)

    in_specs = [pl.BlockSpec((bt, tm, d), lambda i, j: (i, j, 0))]
    in_specs += [pl.BlockSpec((bt, tm, a.shape[2]), lambda i, j: (i, j, 0)) for a in a_list]
    in_specs += [const(w.shape) for w in w_list]
    in_specs += [const((1, d)), const(wq.shape),
                 pl.BlockSpec((None, bt, mh, hd), lambda i, j: (layer, i, 0, 0)),
                 pl.BlockSpec((None, bt, mh, hd), lambda i, j: (layer, i, 0, 0)),
                 const(wo.shape), const((1, d)), const(w_up.shape), const(w_down.shape)]
    args = [x, *a_list, *w_list, g_cross.reshape(1, d), wq, mk, mv, wo, g_mlp.reshape(1, d), w_up, w_down]
    if final_norm:
        in_specs.append(const((1, d)))
        args.append(g_final.reshape(1, d))
    weights = sum(w.size for w in w_list) + wq.size + wo.size + w_up.size + w_down.size
    vmem = (weights * 2 + 2 * (2 * rows * d * 4 + 2 * bt * m * hw * 4
                               + sum(rows * a.shape[2] * a.dtype.itemsize for a in a_list))
            + rows * (d * 16 + hw * 8 + m * 8 + 512 * 8))
    return pl.pallas_call(
        functools.partial(_post_body, n_in=n_in, heads=heads, hd=hw // heads, ff_chunk=512,
                          final_norm=final_norm, col_chunk=512),
        grid=(b // bt, l // tm), in_specs=in_specs,
        out_specs=pl.BlockSpec((bt, tm, d), lambda i, j: (i, j, 0)),
        out_shape=jax.ShapeDtypeStruct((b, l, d), F32),
        compiler_params=_params(("parallel", "parallel"), vmem), name="post_mixer")(*args)


def _cumsum_body(x_ref, o_ref, *, minus_total, scale):
    r, l = x_ref.shape
    row = lax.broadcasted_iota(jnp.int32, (LANE, LANE), 0)
    col = lax.broadcasted_iota(jnp.int32, (LANE, LANE), 1)
    upper = (row <= col).astype(BF16)
    carry = jnp.zeros((r, 1), F32)
    for c in range(0, l, LANE):
        x = x_ref[:, c:c + LANE]
        x1 = x.astype(BF16)
        res = x - x1.astype(F32)
        x2 = res.astype(BF16)
        x3 = (res - x2.astype(F32)).astype(BF16)
        y3 = _dot(jnp.concatenate([x1, x2, x3], axis=0), upper)
        y = y3[:r] + y3[r:2 * r] + y3[2 * r:] + carry
        o_ref[:, c:c + LANE] = y
        carry = y[:, LANE - 1:LANE]
    total = carry if minus_total else 0.0
    o_ref[...] = (o_ref[...] - total) * scale


def cumsum_rows(x, scale, minus_total=False):
    r, l = x.shape
    assert l % LANE == 0 and r % SUBLANE == 0
    return pl.pallas_call(
        functools.partial(_cumsum_body, minus_total=minus_total, scale=scale),
        grid=(1,), in_specs=[pl.BlockSpec((r, l), lambda i: (0, 0))],
        out_specs=pl.BlockSpec((r, l), lambda i: (0, 0)),
        out_shape=jax.ShapeDtypeStruct((r, l), F32),
        compiler_params=_params(("arbitrary",), 4 * r * l * 4), name="cumsum_rows")(x)


def _fox_body(*refs, tq, tkp, n_past, nsub, heads):
    if n_past:
        q_ref, k_ref, v_ref, fk_ref, kp_ref, vp_ref, fp_ref, o_ref = refs
    else:
        q_ref, k_ref, v_ref, fk_ref, o_ref = refs
    i = pl.program_id(2)
    sub = tq // nsub
    hd = q_ref.shape[-1]
    qs = [q_ref[c * sub:(c + 1) * sub, :] for c in range(nsub)]

    def update(carry, ss, vbs):
        ms, ls, accs = carry[0::3], carry[1::3], carry[2::3]
        m_new = [jnp.maximum(m, jnp.max(s, axis=-1, keepdims=True)) for m, s in zip(ms, ss)]
        alpha = [jnp.exp2(m - mn) for m, mn in zip(ms, m_new)]
        ps = [jnp.exp2(s - mn) for s, mn in zip(ss, m_new)]
        ls = [a * l + jnp.sum(p, axis=-1, keepdims=True) for a, l, p in zip(alpha, ls, ps)]
        pv = [_dot(p.astype(BF16), vb) for p, vb in zip(ps, vbs)]
        accs = [a * acc + x for a, acc, x in zip(alpha, accs, pv)]
        out = []
        for c in range(len(ss)):
            out += [m_new[c], ls[c], accs[c]]
        return tuple(out)

    carry = (jnp.full((sub, 1), NEG, F32), jnp.zeros((sub, 1), F32), jnp.zeros((sub, hd), F32)) * nsub

    if n_past:
        def past_step(j, carry):
            rows = pl.ds(j * tkp * heads + pl.program_id(1), tkp, stride=heads)
            kb = kp_ref[rows, :].astype(BF16)
            vb = vp_ref[rows, :].astype(BF16)
            fk = fp_ref[pl.ds(j, 1), :]
            return update(carry, [_dot_nt(q, kb) - fk for q in qs], [vb] * nsub)
        carry = lax.fori_loop(0, n_past, past_step, carry)

    kmul = 2 if nsub % 2 == 0 else 1
    tk = kmul * sub

    def self_step(j, carry):
        start = pl.multiple_of(j * tk, tk)
        kb = k_ref[pl.ds(start, tk), :]
        vb = v_ref[pl.ds(start, tk), :]
        fk = jnp.concatenate([fk_ref[pl.ds(j * kmul + t, 1), :] for t in range(kmul)], axis=1)[:, :tk]
        return update(carry, [_dot_nt(q, kb) - fk for q in qs], [vb] * nsub)
    carry = lax.fori_loop(0, i * (nsub // kmul), self_step, carry)

    start = pl.multiple_of(i * tq, tq)
    fks = [fk_ref[pl.ds(i * nsub + c, 1), :] for c in range(nsub)]
    ss, vbs = [], []
    for c in range(nsub):
        w = (c + 1) * sub
        fk = fks[0][:, :sub] if c == 0 else jnp.concatenate(fks[:c + 1], axis=1)
        s = _dot_nt(qs[c], k_ref[pl.ds(start, w), :]) - fk
        row = lax.broadcasted_iota(jnp.int32, (sub, w), 0) + c * sub
        col = lax.broadcasted_iota(jnp.int32, (sub, w), 1)
        ss.append(jnp.where(col <= row, s, NEG))
        vbs.append(v_ref[pl.ds(start, w), :])
    carry = update(carry, ss, vbs)
    for c in range(nsub):
        o_ref[c * sub:(c + 1) * sub, :] = (carry[3 * c + 2] / carry[3 * c + 1]).astype(o_ref.dtype)


def fox_attention(q, k, v, f_new, heads, past=None, tq_cap=1024, sub_cap=256, score_tile=64 * 1024):
    b, l, hw = q.shape
    hd = hw // heads
    tq = min(l, tq_cap)
    nq = l // tq
    nsub = max(1, tq // sub_cap)
    p_len = 0 if past is None else past[0].shape[1] // heads
    tkp = min(p_len, max(sub_cap, score_tile // tq))
    sub = tq // nsub
    nk = l // sub
    wk = -(-sub // LANE) * LANE
    if wk != sub:
        assert nk == 1
        f_new = jnp.pad(f_new, ((0, 0), (0, 0), (0, wk - sub)))
    in_specs = [pl.BlockSpec((None, tq, hd), lambda bi, h, i: (bi, i, h)),
                pl.BlockSpec((None, l, hd), lambda bi, h, i: (bi, 0, h)),
                pl.BlockSpec((None, l, hd), lambda bi, h, i: (bi, 0, h)),
                pl.BlockSpec((None, None, nk, wk), lambda bi, h, i: (bi, h, 0, 0))]
    args = [q, k, v, f_new.reshape(b, heads, nk, wk)]
    n_past = 0
    vmem = 2 * (2 * tq * hd * 2 + 2 * l * hd * 2) + 8 * tq * max(tq, tkp) * 4
    if past is not None:
        kp, vp, fp = past
        n_past = p_len // tkp
        in_specs += [pl.BlockSpec((None, p_len * heads, hd), lambda bi, h, i: (bi, 0, 0)),
                     pl.BlockSpec((None, p_len * heads, hd), lambda bi, h, i: (bi, 0, 0)),
                     pl.BlockSpec((None, None, n_past, tkp), lambda bi, h, i: (bi, h, 0, 0))]
        args += [kp, vp, fp.reshape(b, heads, n_past, tkp)]
        vmem += 2 * 2 * p_len * heads * hd * 4
    return pl.pallas_call(
        functools.partial(_fox_body, tq=tq, tkp=tkp, n_past=n_past, nsub=nsub, heads=heads),
        grid=(b, heads, nq), in_specs=in_specs,
        out_specs=pl.BlockSpec((None, tq, hd), lambda bi, h, i: (bi, i, h)),
        out_shape=jax.ShapeDtypeStruct((b, l, hw), BF16),
        compiler_params=_params(("parallel", "parallel", "arbitrary"), vmem), name="fox_attention")(*args)


def _gdn_body(qkv_ref, z_ref, gates_ref, cprev_ref, cw_ref, s0_ref, og_ref, o_ref, st_ref, ybuf_ref,
              *, c, nc, heads, dk):
    pad = SUBLANE
    r = c * nc
    hw = heads * dk

    @pl.when(pl.program_id(1) == 0)
    def _():
        ybuf_ref[0:pad, :] = cprev_ref[...]
        st_ref[...] = s0_ref[...]

    ybuf_ref[pad:pad + r, :] = qkv_ref[...]

    row = lax.broadcasted_iota(jnp.int32, (c, c), 0)
    col = lax.broadcasted_iota(jnp.int32, (c, c), 1)
    incl = row >= col
    strict = row > col
    tril = incl.astype(BF16)
    n_double = int(math.log2(c)) - 1
    og = og_ref[...]
    states = [st_ref[h] for h in range(heads)]

    def prep(g):
        r0 = g["ci"] * c
        conv = qkv_ref[r0:r0 + c, :] * cw_ref[CONV_W - 1:CONV_W, :]
        for k in range(1, CONV_W):
            conv = conv + ybuf_ref[pad + r0 - k:pad + r0 - k + c, :] * cw_ref[CONV_W - 1 - k:CONV_W - k, :]
        act = conv * _sigmoid(conv)
        gs = gates_ref[r0:r0 + c, :]
        g1 = gs.astype(BF16)
        res = gs - g1.astype(F32)
        g2 = res.astype(BF16)
        g3 = (res - g2.astype(F32)).astype(BF16)
        cs3 = _dot(tril, jnp.concatenate([g1, g2, g3], axis=-1))
        cs = cs3[:, :LANE] + cs3[:, LANE:2 * LANE] + cs3[:, 2 * LANE:]
        cs_t = jnp.concatenate([cs, jnp.zeros((LANE - c, LANE), F32)], axis=0).T
        g["chains"] = []
        for h in range(heads):
            qh = act[:, h * dk:(h + 1) * dk]
            kh = act[:, hw + h * dk:hw + (h + 1) * dk]
            vh = act[:, 2 * hw + h * dk:2 * hw + (h + 1) * dk]
            qh = qh * (lax.rsqrt(jnp.sum(qh * qh, axis=-1, keepdims=True) + EPS) * (dk ** -0.5))
            kh = kh * lax.rsqrt(jnp.sum(kh * kh, axis=-1, keepdims=True) + EPS)
            beta = gs[:, h:h + 1]
            gc = cs[:, heads + h:heads + h + 1]
            gc_row = cs_t[heads + h:heads + h + 1, :c]
            kb = kh * beta
            egc = jnp.exp(gc)
            g_last = gc[c - 1:c, :]
            g["chains"].append(dict(
                decay=jnp.exp(jnp.where(incl, gc - gc_row, NEG)),
                kq=jnp.concatenate([kb, qh], axis=0).astype(BF16), khb=kh.astype(BF16),
                rhs=jnp.concatenate([vh * beta, kb * egc], axis=-1), qg=qh * egc,
                kd=(kh * jnp.exp(g_last - gc)).astype(BF16), ds=jnp.exp(g_last)))

    def gram(g):
        for ch in g["chains"]:
            ch["a1"] = _dot_nt(ch["kq"], ch["khb"])

    def masks(g):
        for ch in g["chains"]:
            lmat = jnp.where(strict, ch["a1"][:c] * ch["decay"], 0.0)
            ch["attn"] = jnp.where(incl, ch["a1"][c:] * ch["decay"], 0.0).astype(BF16)
            ch["toff"] = -lmat
            ch["lb"] = lmat.astype(BF16)
        for ch in g["chains"]:
            ch["pw"] = _dot(ch["lb"], ch["lb"])

    def double(last):
        def stage(g):
            for ch in g["chains"]:
                pwb = ch["pw"].astype(BF16)
                lhs = ch["toff"].astype(BF16) if last else jnp.concatenate([ch["toff"].astype(BF16), pwb], axis=0)
                ch["tp"] = _dot(lhs, pwb)
            for ch in g["chains"]:
                ch["toff"] = ch["toff"] + ch["pw"] + ch["tp"][:c]
                if not last:
                    ch["pw"] = ch["tp"][c:]
        return stage

    def solve(g):
        for ch in g["chains"]:
            ch["uw"] = _dot(ch["toff"].astype(BF16), ch["rhs"].astype(BF16))
        for ch in g["chains"]:
            uw = ch["rhs"] + ch["uw"]
            ch["u"] = uw[:, :dk]
            ch["wq"] = jnp.concatenate([uw[:, dk:], ch["qg"]], axis=0).astype(BF16)

    def recur(g):
        group, r0 = g["chains"], g["ci"] * c
        t1 = [_dot(ch["wq"], s.astype(BF16)) for ch, s in zip(group, states)]
        vnb = [(ch["u"] - t[:c]).astype(BF16) for ch, t in zip(group, t1)]
        o2 = [_dot(ch["attn"], v) for ch, v in zip(group, vnb)]
        sd = [_dot_tn(ch["kd"], v) for ch, v in zip(group, vnb)]
        states[:] = [s * ch["ds"] + d for ch, s, d in zip(group, states, sd)]
        for h in range(heads):
            o = t1[h][c:] + o2[h]
            zh = z_ref[r0:r0 + c, h * dk:(h + 1) * dk]
            o_ref[r0:r0 + c, h * dk:(h + 1) * dk] = (_rms(o, og) * (zh * _sigmoid(zh))).astype(o_ref.dtype)

    stages = [prep, gram, masks] + [double(s == n_double - 1) for s in range(n_double)] + [solve, recur]
    groups = [dict(ci=ci) for ci in range(nc)]
    for t in range(len(stages) + nc - 1):
        for ci in range(nc):
            if 0 <= t - ci < len(stages):
                stages[t - ci](groups[ci])

    for h in range(heads):
        st_ref[h] = states[h]
    ybuf_ref[0:pad, :] = ybuf_ref[r:r + pad, :]


def gated_delta_net(qkv, z, gates, conv_prev, conv_w, s0, onorm_g, heads, dk, chunks_per_step=16):
    b, l, cd = qkv.shape
    c = min(CHUNK, l)
    nc = min(chunks_per_step, l // c)
    r = c * nc
    hw = heads * dk
    cprev = jnp.pad(conv_prev, ((0, 0), (SUBLANE - (CONV_W - 1), 0), (0, 0)))
    cw = jnp.pad(conv_w, ((0, SUBLANE - CONV_W), (0, 0)))
    vmem = 2 * (r * cd * 4 + r * hw * 4 + r * LANE * 4 + 2 * SUBLANE * cd * 4 + 2 * heads * dk * dk * 4
                + r * hw * 2) + (r + SUBLANE) * cd * 4 * 4
    return pl.pallas_call(
        functools.partial(_gdn_body, c=c, nc=nc, heads=heads, dk=dk),
        grid=(b, l // r),
        in_specs=[pl.BlockSpec((None, r, cd), lambda i, n: (i, n, 0)),
                  pl.BlockSpec((None, r, hw), lambda i, n: (i, n, 0)),
                  pl.BlockSpec((None, r, LANE), lambda i, n: (i, n, 0)),
                  pl.BlockSpec((None, SUBLANE, cd), lambda i, n: (i, 0, 0)),
                  pl.BlockSpec((SUBLANE, cd), lambda i, n: (0, 0)),
                  pl.BlockSpec((None, heads, dk, dk), lambda i, n: (i, 0, 0, 0)),
                  pl.BlockSpec((1, dk), lambda i, n: (0, 0))],
        out_specs=[pl.BlockSpec((None, r, hw), lambda i, n: (i, n, 0)),
                   pl.BlockSpec((None, heads, dk, dk), lambda i, n: (i, 0, 0, 0))],
        out_shape=[jax.ShapeDtypeStruct((b, l, hw), BF16),
                   jax.ShapeDtypeStruct((b, heads, dk, dk), F32)],
        scratch_shapes=[pltpu.VMEM((r + SUBLANE, cd), F32)],
        compiler_params=_params(("parallel", "arbitrary"), vmem), name="gated_delta_net")(
            qkv, z, gates, cprev, cw, s0, onorm_g.reshape(1, dk))


def _swa_body(*refs, n_seg, cq, nchunk, win, hkv, grp, hd, chunked):
    sink_ref, q_ref = refs[0], refs[1]
    k_refs, v_refs, o_ref = refs[2:2 + n_seg], refs[2 + n_seg:2 + 2 * n_seg], refs[2 + 2 * n_seg]
    kk = jnp.concatenate([r[...] for r in k_refs], axis=0).astype(BF16)
    vv = jnp.concatenate([r[...] for r in v_refs], axis=0).astype(BF16)
    row = lax.broadcasted_iota(jnp.int32, (cq, win), 0)
    col = lax.broadcasted_iota(jnp.int32, (cq, win), 1)
    dist = jnp.abs(row - col + WINDOW).astype(F32)
    hq = hkv * grp
    k_heads = [kk[:, j * hd:(j + 1) * hd] for j in range(hkv)]
    v_heads = [vv[:, j * hd:(j + 1) * hd] for j in range(hkv)]
    chains = []
    for c in range(nchunk):
        dist_c = dist
        if chunked and c * cq < WINDOW:
            first_valid = WINDOW - (pl.program_id(1) * nchunk + c) * cq
            dist_c = jnp.where(col >= first_valid, dist, -NEG)
        for j in range(hkv):
            q_stack = jnp.concatenate(
                [q_ref[c * cq:(c + 1) * cq, (j * grp + g) * hd:(j * grp + g + 1) * hd] for g in range(grp)], axis=0)
            chains.append(dict(c=c, j=j, dist=dist_c, q=q_stack,
                               k=k_heads[j][c * cq:c * cq + win, :], v=v_heads[j][c * cq:c * cq + win, :]))

    def scores(group):
        for ch in group:
            ch["s"] = _dot_nt(ch["q"], ch["k"])

    def softmax_pv(group):
        heads_of = [(ch, g, ch["j"] * grp + g) for ch in group for g in range(grp)]
        slopes = [LOG2E * 2.0 ** (-8.0 * (h + 1) / hq) for _, _, h in heads_of]
        sinks = [LOG2E * sink_ref[h] for _, _, h in heads_of]
        sg = [ch["s"][g * cq:(g + 1) * cq, :] - sl * ch["dist"] for (ch, g, _), sl in zip(heads_of, slopes)]
        m = [jnp.maximum(jnp.max(x, axis=-1, keepdims=True), sk) for x, sk in zip(sg, sinks)]
        p = [jnp.exp2(x - mm) for x, mm in zip(sg, m)]
        l = [jnp.sum(x, axis=-1, keepdims=True) + jnp.exp2(sk - mm) for x, mm, sk in zip(p, m, sinks)]
        pb = [x.astype(BF16) for x in p]
        for ci, ch in enumerate(group):
            ch["l"] = l[ci * grp:(ci + 1) * grp]
            ch["pv"] = _dot(jnp.concatenate(pb[ci * grp:(ci + 1) * grp], axis=0), ch["v"])

    def emit(group):
        for ch in group:
            c, j = ch["c"], ch["j"]
            outs = [(ch["pv"][g * cq:(g + 1) * cq, :] / ch["l"][g]).astype(o_ref.dtype) for g in range(grp)]
            o_ref[c * cq:(c + 1) * cq, j * grp * hd:(j + 1) * grp * hd] = jnp.concatenate(outs, axis=-1)

    gsz = 2
    groups = [chains[i:i + gsz] for i in range(0, len(chains), gsz)]
    scores(groups[0])
    for i, group in enumerate(groups):
        if i + 1 < len(groups):
            scores(groups[i + 1])
        softmax_pv(group)
        if i > 0:
            emit(groups[i - 1])
    emit(groups[-1])


def swa_attention(q, segs, sinks, hkv, grp, hd, chunked, chunks_per_step=16):
    b, l, qw = q.shape
    kw = hkv * hd
    if chunked:
        cq, nchunk, n_seg, win = CHUNK, chunks_per_step, 2, WINDOW + CHUNK
        rows = cq * nchunk
        assert l % rows == 0 and rows % WINDOW == 0
        k, v = segs
        k_specs = [pl.BlockSpec((None, WINDOW, kw), lambda i, n: (i, jnp.maximum(n * (rows // WINDOW) - 1, 0), 0)),
                   pl.BlockSpec((None, rows, kw), lambda i, n: (i, n, 0))]
        k_args, v_args = [k] * n_seg, [v] * n_seg
        key_rows = WINDOW + rows
    else:
        cq, nchunk, n_seg = l, 1, len(segs)
        rows = l
        k_specs = [pl.BlockSpec((None, kk.shape[1], kw), lambda i, n: (i, 0, 0)) for kk, _ in segs]
        k_args, v_args = [kk for kk, _ in segs], [vv for _, vv in segs]
        key_rows = win = sum(kk.shape[1] for kk in k_args)
    vmem = 2 * (2 * rows * qw * 2 + 2 * key_rows * kw * 4) + 3 * nchunk * hkv * grp * cq * 256 * 4
    return pl.pallas_call(
        functools.partial(_swa_body, n_seg=n_seg, cq=cq, nchunk=nchunk, win=win, hkv=hkv, grp=grp, hd=hd,
                          chunked=chunked),
        grid=(b, l // rows),
        in_specs=[pl.BlockSpec(memory_space=pltpu.SMEM),
                  pl.BlockSpec((None, rows, qw), lambda i, n: (i, n, 0))] + k_specs + k_specs,
        out_specs=pl.BlockSpec((None, rows, qw), lambda i, n: (i, n, 0)),
        out_shape=jax.ShapeDtypeStruct((b, l, qw), BF16),
        compiler_params=_params(("parallel", "parallel"), vmem), name="swa_attention")(
            sinks, q, *k_args, *v_args)


def _ab_layer(x, conv_prev, s0, past, w_in, conv_w, a_log, dt_bias, onorm_g, f_bias, w_out, g_mix, dims):
    b, l, d = x.shape
    h_a, dk, h_b, hd_b = dims
    a_qk, a_v, b_w = h_a * dk, h_a * dk, h_b * hd_b
    conv_dim = 2 * a_qk + a_v
    o1 = conv_dim + a_v
    o2 = o1 + 2 * h_a
    small = jnp.concatenate([w_in[:, o1:o2], w_in[:, o2 + 3 * b_w:],
                             jnp.zeros((d, LANE - 2 * h_a - h_b), F32)], axis=1)
    w_pack = lax.optimization_barrier(jnp.concatenate([w_in[:, :o1], w_in[:, o2:o2 + 3 * b_w], small], axis=1))
    w_pack = w_pack.astype(BF16)
    zeros4 = jnp.zeros((h_a,), F32)
    gp = jnp.zeros((SUBLANE, LANE), F32)
    gp = gp.at[0, :3 * h_a].set(jnp.concatenate([zeros4, dt_bias, f_bias]))
    gp = gp.at[1, h_a:2 * h_a].set(a_log)
    f32o, b16, cache = (F32, 1.0, 1), (BF16, 1.0, 1), (F32, 1.0, h_b)
    groups = ((conv_dim, (f32o,)), (a_v, (f32o,)), (b_w, ((BF16, LOG2E * hd_b ** -0.5, 1),)),
              (b_w, (cache, b16)), (b_w, (cache, b16)), (LANE, (f32o,)))
    qkv, z, qb, kb, kb16, vb, vb16, gates = norm_matmul(x.reshape(b * l, d), g_mix, w_pack, groups, gate_params=gp)
    r3 = lambda a: a.reshape(b, l, a.shape[-1])
    qkv3, gates3 = r3(qkv), r3(gates)
    oa, s_new = gated_delta_net(qkv3, r3(z), gates3, conv_prev, conv_w, s0, onorm_g, h_a, dk)
    conv_new = qkv3[:, l - (CONV_W - 1):, :]
    logf = gates3[:, :, 2 * h_a:2 * h_a + h_b]
    lf_rows = jnp.pad(logf.transpose(0, 2, 1).reshape(b * h_b, l), ((0, 0), (0, -l % LANE)))
    f_new = cumsum_rows(lf_rows, LOG2E)[:, :l].reshape(b, h_b, l)
    fox_past = None
    if past is not None:
        pk, pv, plf = past
        p_len = pk.shape[1]
        f_past = cumsum_rows(plf.transpose(0, 2, 1).reshape(b * h_b, p_len), LOG2E, minus_total=True)
        fox_past = (pk.reshape(b, p_len * h_b, hd_b), pv.reshape(b, p_len * h_b, hd_b), f_past.reshape(b, h_b, p_len))
    ob = fox_attention(r3(qb), r3(kb16), r3(vb16), f_new, h_b, past=fox_past)
    w_out16 = w_out.astype(BF16)
    mixed = ([oa, ob], [w_out16[:a_v], w_out16[a_v:]])
    return (mixed, s_new, conv_new, kb.reshape(b, l, h_b, hd_b), vb.reshape(b, l, h_b, hd_b), logf)


def _c_layer(x, past, w_in, sinks, w_out, g_mix, dims):
    b, l, d = x.shape
    hq, hkv, hd = dims
    grp = hq // hkv
    f32o = (F32, 1.0, 1)
    q, k, v = norm_matmul(x.reshape(b * l, d), g_mix, w_in.astype(BF16),
                          ((hq * hd, ((BF16, LOG2E * hd ** -0.5, 1),)), (hkv * hd, (f32o,)), (hkv * hd, (f32o,))))
    q3, k3, v3 = q.reshape(b, l, hq * hd), k.reshape(b, l, hkv * hd), v.reshape(b, l, hkv * hd)
    if past is None:
        o = swa_attention(q3, (k3, v3), sinks, hkv, grp, hd, chunked=True)
        new_k, new_v = k3[:, l - WINDOW:], v3[:, l - WINDOW:]
    else:
        pk, pv = past
        w_len = pk.shape[1]
        o = swa_attention(q3, ((pk.reshape(b, w_len, hkv * hd), pv.reshape(b, w_len, hkv * hd)), (k3, v3)),
                          sinks, hkv, grp, hd, chunked=False)
        new_k, new_v = k3, v3
    return ([o], [w_out.astype(BF16)]), new_k.reshape(b, -1, hkv, hd), new_v.reshape(b, -1, hkv, hd)


def kernel(x_prompt, mem_prompt, x_sample, state_gdn, state_gdn_conv, cache_fox_k, cache_fox_v, cache_fox_logf, cache_swa_k, cache_swa_v, cache_mem_k, cache_mem_v, g_mix, w_in_ab, conv_w_a, a_log_a, dt_bias_a, onorm_g_a, f_bias_b, w_out_ab, w_in_c, sinks_c, w_out_c, g_mem, w_xkv, g_cross, w_xq, w_xo, g_mlp, w_up, w_down, g_final):
    depth = g_mix.shape[0]
    bp, lp, d = x_prompt.shape
    bs, ls, _ = x_sample.shape
    h_a, dk = state_gdn.shape[2], state_gdn.shape[3]
    h_b, hd_b = cache_fox_k.shape[3], cache_fox_k.shape[4]
    hkv, hd_c = cache_swa_k.shape[3], cache_swa_k.shape[4]
    hq = sinks_c.shape[1]
    m_len, h_x, hd_x = cache_mem_k.shape[2], cache_mem_k.shape[3], cache_mem_k.shape[4]
    xw = h_x * hd_x
    conv_dim = state_gdn_conv.shape[-1]

    xp, xs = x_prompt, x_sample
    mem_outs = ((F32, 1.0, h_x),)
    mem_k, mem_v = norm_matmul(mem_prompt.reshape(bp * m_len, d), g_mem, w_xkv.astype(BF16),
                               ((xw, mem_outs), (xw, mem_outs)))
    p_out = {k: [] for k in ("s", "c", "fk", "fv", "fl", "wk", "wv")}
    s_out = {k: [] for k in ("s", "c", "fk", "fv", "fl", "wk", "wv")}
    for layer in range(depth):
        i = layer // 2
        if layer % 2 == 0:
            wts = (w_in_ab[i], conv_w_a[i], a_log_a[i], dt_bias_a[i], onorm_g_a[i], f_bias_b[i], w_out_ab[i],
                   g_mix[layer], (h_a, dk, h_b, hd_b))
            mix_p, st, cv, fk, fv, fl = _ab_layer(xp, jnp.zeros((bp, CONV_W - 1, conv_dim), F32),
                                                  jnp.zeros((bp, h_a, dk, dk), F32), None, *wts)
            for key, val in zip(("s", "c", "fk", "fv", "fl"), (st, cv, fk, fv, fl)):
                p_out[key].append(val)
            mix_s, st, cv, fk, fv, fl = _ab_layer(xs, state_gdn_conv[i], state_gdn[i],
                                                  (cache_fox_k[i], cache_fox_v[i], cache_fox_logf[i]), *wts)
            for key, val in zip(("s", "c", "fk", "fv", "fl"), (st, cv, fk, fv, fl)):
                s_out[key].append(val)
        else:
            wts = (w_in_c[i], sinks_c[i], w_out_c[i], g_mix[layer], (hq, hkv, hd_c))
            mix_p, wk, wv = _c_layer(xp, None, *wts)
            p_out["wk"].append(wk); p_out["wv"].append(wv)
            mix_s, wk, wv = _c_layer(xs, (cache_swa_k[i], cache_swa_v[i]), *wts)
            s_out["wk"].append(wk); s_out["wv"].append(wv)
        post_w = (g_cross[layer], w_xq[layer].astype(BF16))
        mlp_w = (w_xo[layer].astype(BF16), g_mlp[layer], w_up[layer].astype(BF16), w_down[layer].astype(BF16), h_x,
                 g_final if layer == depth - 1 else None)
        mem_shape = (depth, -1, m_len * h_x, hd_x)
        xp = post_mixer(xp, *mix_p, *post_w, mem_k.reshape(mem_shape), mem_v.reshape(mem_shape), layer, *mlp_w)
        xs = post_mixer(xs, *mix_s, *post_w, cache_mem_k.reshape(mem_shape), cache_mem_v.reshape(mem_shape), layer,
                        *mlp_w)
    st = jnp.stack
    return (xp, xs,
            st(p_out["s"]), st(p_out["c"]), st(p_out["fk"]), st(p_out["fv"]), st(p_out["fl"]),
            st(p_out["wk"]), st(p_out["wv"]),
            mem_k.reshape(depth, bp, m_len, h_x, hd_x), mem_v.reshape(depth, bp, m_len, h_x, hd_x),
            st(s_out["s"]), st(s_out["c"]), st(s_out["fk"]), st(s_out["fv"]), st(s_out["fl"]),
            st(s_out["wk"]), st(s_out["wv"]))
```
